```python
import math
import jax
import jax.numpy as jnp
from jax import lax
import numpy as np

D_MODEL = 2048
BATCH = 32
SEQ = 256
DEPTH = 2
DEC_BATCH = 2
DEC_SEQ = 2048
PAST_LEN = 512

GRID_W = 64
NORM_EPS = 1e-6
SSD_WIDTH = D_MODEL // 2
SSD_HEAD_DIM = 64
SSD_HEADS = SSD_WIDTH // SSD_HEAD_DIM
SSD_GROUPS = 2
SSD_STATE = 128
SSD_CONV = 4
SSD_CHUNK = 128
SSD_CONV_CH = SSD_WIDTH + 2 * SSD_GROUPS * SSD_STATE
DA_WIDTH = D_MODEL // 2
DA_HEAD_DIM = 64
DA_HEADS = DA_WIDTH // (2 * DA_HEAD_DIM)
DA_LAMBDA_INIT = 0.8 - 0.6 * math.exp(-0.3 * 0)
Q_BLOCK = 128
ROPE_BASE = 10000.0
ROPE_FREQS = DA_HEAD_DIM // 4
L0_SPLITS = (SSD_WIDTH, SSD_WIDTH, SSD_GROUPS * SSD_STATE, SSD_GROUPS * SSD_STATE,
             SSD_HEADS, SSD_HEADS, DA_WIDTH, DA_WIDTH, DA_WIDTH)
L0_IN = sum(L0_SPLITS)
L0_MIX_OUT = SSD_WIDTH + DA_WIDTH
HY_WIDTH = D_MODEL
HY_SHORT = 3
HY_BANDS = 16
HY_EMB = 1 + 2 * HY_BANDS
HY_FILTER_HIDDEN = 64
HY_FAST_DECAY = 0.3
HY_SLOW_DECAY = 1.5
HY_TARGET = 1e-2
N_EXPERTS = 32
TOP_K = 4
D_FF = D_MODEL
SWIGLU_LIMIT = 7.0
SWIGLU_ALPHA = 1.702
MOE_BLOCK = 128

kernel_name = 'hybrid_ssd_diffattn_hyena_moe_diffusion_step'


def rms_norm(x, g):
    xf = x.astype(jnp.float32)
    y = xf * lax.rsqrt(jnp.mean(xf * xf, axis=-1, keepdims=True) + NORM_EPS)
    return (y * g.astype(jnp.float32)).astype(x.dtype)


def split_cols(z, sizes):
    idx = [int(i) for i in np.cumsum(sizes)[:-1]]
    return jnp.split(z, idx, axis=-1)


def _flip(t):
    return jnp.flip(t, axis=1)


def adaln(cond, w, b):
    m = jax.nn.silu(cond) @ w + b
    return jnp.split(m[:, None, :], 6, axis=-1)


def dw_conv_centred(x, w, b):
    width = w.shape[0]
    left = (width - 1) // 2
    y = lax.conv_general_dilated(x, w[:, None, :].astype(x.dtype), window_strides=(1,),
                                 padding=[(left, width - 1 - left)],
                                 dimension_numbers=('NWC', 'WIO', 'NWC'),
                                 feature_group_count=x.shape[-1])
    return y + b


def axial_rope(n_tokens):
    rows = n_tokens // GRID_W
    row_id = jnp.repeat(jnp.arange(rows), GRID_W)
    col_id = jnp.tile(jnp.arange(GRID_W), rows)
    inv = 1.0 / (ROPE_BASE ** (jnp.arange(ROPE_FREQS, dtype=jnp.float32) / ROPE_FREQS))
    ang = jnp.stack([row_id, col_id], axis=-1).astype(jnp.float32)[:, :, None] * inv
    return jnp.cos(ang), jnp.sin(ang)


def apply_axial_rope(x, cos, sin):
    xs = x.astype(jnp.float32).reshape(x.shape[:-1] + (2, 2, ROPE_FREQS))
    x1, x2 = xs[..., 0, :], xs[..., 1, :]
    c, s = cos[None, :, None], sin[None, :, None]
    out = jnp.stack([x1 * c - x2 * s, x2 * c + x1 * s], axis=-2)
    return out.reshape(x.shape).astype(x.dtype)


def ssd_scan(x, dt, a, bm, cm, h0):
    b, L, H, P = x.shape
    G, N = bm.shape[-2:]
    R = H // G
    Q = SSD_CHUNK
    nc = L // Q
    xdt = (x.astype(jnp.float32) * dt[..., None]).reshape(b, nc, Q, G, R, P)
    a_cs = jnp.cumsum((dt * a).reshape(b, nc, Q, G, R), axis=2)
    bc = bm.astype(jnp.float32).reshape(b, nc, Q, G, N)
    cc = cm.astype(jnp.float32).reshape(b, nc, Q, G, N)
    seg = a_cs[:, :, :, None] - a_cs[:, :, None, :]
    tri = jnp.tril(jnp.ones((Q, Q), bool))[:, :, None, None]
    decay = jnp.exp(jnp.where(tri, seg, -jnp.inf))
    cb = jnp.einsum('bcign,bcjgn->bcijg', cc, bc)
    y_diag = jnp.einsum('bcijgr,bcjgrp->bcigrp', decay * cb[..., None], xdt)
    decay_end = jnp.exp(a_cs[:, :, -1:] - a_cs)
    states = jnp.einsum('bcjgn,bcjgrp->bcgrpn', bc, xdt * decay_end[..., None])
    chunk_decay = jnp.exp(a_cs[:, :, -1])

    def carry_step(h, inp):
        st, dec = inp
        return h * dec[..., None, None] + st, h

    h_last, h_in = lax.scan(carry_step, h0.astype(jnp.float32).reshape(b, G, R, P, N),
                            (jnp.moveaxis(states, 1, 0), jnp.moveaxis(chunk_decay, 1, 0)))
    h_in = jnp.moveaxis(h_in, 0, 1)
    y_off = jnp.einsum('bcign,bcgrpn->bcigrp', cc, h_in) * jnp.exp(a_cs)[..., None]
    y = (y_diag + y_off).reshape(b, L, H, P)
    return y.astype(x.dtype), h_last.reshape(b, H, P, N)


def diff_attention(q, k, v, lam):
    b, Lq = q.shape[:2]
    qb = jnp.moveaxis(q.reshape((b, Lq // Q_BLOCK, Q_BLOCK) + q.shape[2:]), 1, 0)
    scale = DA_HEAD_DIM ** -0.5

    def one_block(qblk):
        s = jnp.einsum('bqhmd,bkhmd->bhmqk', qblk, k).astype(jnp.float32) * scale
        p = jax.nn.softmax(s, axis=-1)
        w = p[:, :, 0] - lam * p[:, :, 1]
        return jnp.einsum('bhqk,bkhe->bqhe', w.astype(v.dtype), v)

    out = lax.map(one_block, qb)
    return jnp.moveaxis(out, 0, 1).reshape(b, Lq, q.shape[2], v.shape[-1])


def ssd_diff_mixer(hn, w_in, conv_w, conv_b, a_log_fwd, a_log_bwd, dt_bias_fwd, dt_bias_bwd,
                   d_skip, gnorm_w, lambda_q1, lambda_k1, lambda_q2, lambda_k2, subln_w, w_out,
                   rope=None, ctx=None):
    b, L, _ = hn.shape
    s_x, s_z, s_b, s_c, s_dtf, s_dtb, q, k, v = split_cols(hn @ w_in, L0_SPLITS)
    xbc = jax.nn.silu(dw_conv_centred(jnp.concatenate([s_x, s_b, s_c], axis=-1), conv_w, conv_b))
    xs, bm, cm = split_cols(xbc, (SSD_WIDTH, SSD_GROUPS * SSD_STATE, SSD_GROUPS * SSD_STATE))
    xs = xs.reshape(b, L, SSD_HEADS, SSD_HEAD_DIM)
    bm = bm.reshape(b, L, SSD_GROUPS, SSD_STATE)
    cm = cm.reshape(b, L, SSD_GROUPS, SSD_STATE)
    if ctx is None:
        h0_fwd = jnp.zeros((b, SSD_HEADS, SSD_HEAD_DIM, SSD_STATE), jnp.float32)
        h0_bwd = h0_fwd
    else:
        h0_fwd, h0_bwd, k_ctx, v_ctx = ctx
    dt_fwd = jax.nn.softplus(s_dtf.astype(jnp.float32) + dt_bias_fwd.astype(jnp.float32))
    dt_bwd = jax.nn.softplus(s_dtb.astype(jnp.float32) + dt_bias_bwd.astype(jnp.float32))
    y_fwd, hT_fwd = ssd_scan(xs, dt_fwd, -jnp.exp(a_log_fwd.astype(jnp.float32)), bm, cm, h0_fwd)
    y_bwd, hT_bwd = ssd_scan(_flip(xs), _flip(dt_bwd), -jnp.exp(a_log_bwd.astype(jnp.float32)),
                             _flip(bm), _flip(cm), h0_bwd)
    y = y_fwd + _flip(y_bwd) + d_skip[:, None] * xs
    y = y.reshape(b, L, SSD_WIDTH) * jax.nn.silu(s_z)
    y_ssd = rms_norm(y.reshape(b, L, SSD_GROUPS, -1),
                     gnorm_w.reshape(SSD_GROUPS, -1)).reshape(b, L, SSD_WIDTH)
    q = q.reshape(b, L, DA_HEADS, 2, DA_HEAD_DIM)
    k = k.reshape(b, L, DA_HEADS, 2, DA_HEAD_DIM)
    v = v.reshape(b, L, DA_HEADS, 2 * DA_HEAD_DIM)
    if ctx is None:
        k_all, v_all = k, v
    else:
        cos, sin = rope
        q = apply_axial_rope(q.reshape(b, L, 2 * DA_HEADS, DA_HEAD_DIM), cos, sin).reshape(q.shape)
        k = apply_axial_rope(k.reshape(b, L, 2 * DA_HEADS, DA_HEAD_DIM), cos, sin).reshape(k.shape)
        k_all = jnp.concatenate([k_ctx.reshape(b, -1, DA_HEADS, 2, DA_HEAD_DIM), k], axis=1)
        v_all = jnp.concatenate([v_ctx, v], axis=1)
    lam = (jnp.exp(jnp.sum(lambda_q1 * lambda_k1).astype(jnp.float32))
           - jnp.exp(jnp.sum(lambda_q2 * lambda_k2).astype(jnp.float32)) + DA_LAMBDA_INIT)
    o = diff_attention(q, k_all, v_all, lam)
    o = (rms_norm(o, subln_w) * (1.0 - DA_LAMBDA_INIT)).reshape(b, L, DA_WIDTH)
    out = jnp.concatenate([y_ssd, o], axis=-1) @ w_out
    if ctx is None:
        return out, (hT_fwd.astype(hn.dtype), hT_bwd.astype(hn.dtype),
                     k.reshape(b, L, DA_HEADS, 2 * DA_HEAD_DIM), v)
    return out, None


def hyena_filter(L, w1, b1, w2, b2, w3, b3, w4, freq):
    t = jnp.arange(L, dtype=jnp.float32)
    bands = jnp.linspace(1e-4, HY_BANDS - 1, HY_BANDS, dtype=jnp.float32)
    ang = (2.0 * math.pi / L) * t[:, None] * bands
    z = jnp.concatenate([(t / L)[:, None], jnp.cos(ang), jnp.sin(ang)], axis=-1).astype(w1.dtype)
    h = jnp.sin(freq * (z @ w1 + b1))
    h = jnp.sin(freq * (h @ w2 + b2))
    h = jnp.sin(freq * (h @ w3 + b3))
    h = h @ w4
    deltas = jnp.abs(jnp.linspace(math.log(HY_TARGET) / HY_SLOW_DECAY,
                                  math.log(HY_TARGET) / HY_FAST_DECAY, w4.shape[1], dtype=jnp.float32))
    dist = jnp.abs(t - L // 2) / L
    return h * jnp.exp(-dist[:, None] * deltas).astype(h.dtype)


def fft_conv_centred(v, filt):
    L = v.shape[1]
    vf = jnp.fft.rfft(v.astype(jnp.float32), n=2 * L, axis=1)
    hf = jnp.fft.rfft(filt.astype(jnp.float32), n=2 * L, axis=0)
    y = jnp.fft.irfft(vf * hf[None], n=2 * L, axis=1)[:, L // 2: L // 2 + L]
    return y.astype(v.dtype)


def hyena_mixer(hn, w_in, b_in, short_w, short_b, f_w1, f_b1, f_w2, f_b2, f_w3, f_b3, f_w4,
                f_freq, f_bias, w_out, b_out):
    L = hn.shape[1]
    u = dw_conv_centred(hn @ w_in + b_in, short_w, short_b)
    x0, x1, v = jnp.split(u, 3, axis=-1)
    filt = hyena_filter(L, f_w1, f_b1, f_w2, f_b2, f_w3, f_b3, f_w4, f_freq)
    v = v * x1
    y = (fft_conv_centred(v, filt) + v * f_bias) * x0
    return y @ w_out + b_out


def moe_ffn(x, router_w, router_b, w_gu, b_gu, w_dn, b_dn):
    b, L, D = x.shape
    T = b * L
    x2d = x.reshape(T, D)
    logits = (x2d @ router_w).astype(jnp.float32) + router_b.astype(jnp.float32)
    top_val, top_idx = lax.top_k(logits, TOP_K)
    gates = jax.nn.softmax(top_val, axis=-1)
    flat_e = top_idx.reshape(-1)
    flat_tok = jnp.repeat(jnp.arange(T, dtype=jnp.int32), TOP_K)
    order = jnp.argsort(flat_e)
    e_sorted = flat_e[order]
    counts = jnp.bincount(flat_e, length=N_EXPERTS)
    padded = (counts + MOE_BLOCK - 1) // MOE_BLOCK * MOE_BLOCK
    start = jnp.cumsum(counts) - counts
    padded_end = jnp.cumsum(padded)
    padded_start = padded_end - padded
    dest = padded_start[e_sorted] + jnp.arange(T * TOP_K) - start[e_sorted]
    n_rows = T * TOP_K + N_EXPERTS * MOE_BLOCK
    row_tok = jnp.zeros((n_rows,), jnp.int32).at[dest].set(flat_tok[order])
    row_gate = jnp.zeros((n_rows,), jnp.float32).at[dest].set(gates.reshape(-1)[order])
    n_blocks = n_rows // MOE_BLOCK
    block_e = jnp.minimum(jnp.searchsorted(padded_end, jnp.arange(n_blocks) * MOE_BLOCK, side='right'),
                          N_EXPERTS - 1)

    def expert_block(args):
        tok, e = args
        gu = x2d[tok] @ w_gu[e] + b_gu[e]
        gate = jnp.minimum(gu[:, 0::2], SWIGLU_LIMIT)
        up = jnp.clip(gu[:, 1::2], -SWIGLU_LIMIT, SWIGLU_LIMIT)
        act = (up + 1.0) * (gate * jax.nn.sigmoid(SWIGLU_ALPHA * gate))
        return act @ w_dn[e] + b_dn[e]

    y_rows = lax.map(expert_block, (row_tok.reshape(n_blocks, MOE_BLOCK), block_e))
    y_rows = y_rows.reshape(n_rows, D) * row_gate[:, None].astype(y_rows.dtype)
    return jnp.zeros((T, D), x.dtype).at[row_tok].add(y_rows.astype(x.dtype)).reshape(b, L, D)


def setup_inputs(seed: int = 0) -> dict:
    key = jax.random.key(seed)
    ks = iter(jax.random.split(key, 96))
    d = D_MODEL

    def nrm(shape, scale):
        return jax.random.normal(next(ks), shape, jnp.float32) * scale

    def gain(n):
        return 1.0 + nrm((n,), 0.01)

    def a_log():
        return jnp.log(jax.random.uniform(next(ks), (SSD_HEADS,), jnp.float32, 1.0, 16.0))

    def dt_bias():
        dt = jnp.exp(jax.random.uniform(next(ks), (SSD_HEADS,), jnp.float32,
                                        math.log(1e-3), math.log(1e-1)))
        return dt + jnp.log(-jnp.expm1(-dt))

    inp = {}
    inp['x_prompt'] = nrm((BATCH, SEQ, d), 1.0)
    inp['x_sample'] = nrm((DEC_BATCH, DEC_SEQ, d), 1.0)
    inp['c'] = nrm((DEC_BATCH, d), 1.0)
    inp['state_l0_ssm_fwd'] = nrm((DEC_BATCH, SSD_HEADS, SSD_HEAD_DIM, SSD_STATE), 0.5)
    inp['state_l0_ssm_bwd'] = nrm((DEC_BATCH, SSD_HEADS, SSD_HEAD_DIM, SSD_STATE), 0.5)
    inp['cache_l0_k'] = nrm((DEC_BATCH, PAST_LEN, DA_HEADS, 2 * DA_HEAD_DIM), 1.0)
    inp['cache_l0_v'] = nrm((DEC_BATCH, PAST_LEN, DA_HEADS, 2 * DA_HEAD_DIM), 1.0)
    inp['c_ctx'] = nrm((d,), 1.0)

    def moe_params(p):
        inp[p + 'router_w'] = nrm((d, N_EXPERTS), d ** -0.5)
        inp[p + 'router_b'] = nrm((N_EXPERTS,), 0.01)
        inp[p + 'w_gate_up'] = nrm((N_EXPERTS, d, 2 * D_FF), d ** -0.5)
        inp[p + 'b_gate_up'] = nrm((N_EXPERTS, 2 * D_FF), 0.01)
        inp[p + 'w_down'] = nrm((N_EXPERTS, D_FF, d), D_FF ** -0.5)
        inp[p + 'b_down'] = nrm((N_EXPERTS, d), 0.01)

    inp['l0_norm_mix'] = gain(d)
    inp['l0_norm_ffn'] = gain(d)
    inp['l0_ada_w'] = nrm((d, 6 * d), 0.5 * d ** -0.5)
    inp['l0_ada_b'] = nrm((6 * d,), 0.01)
    inp['l0_w_in'] = nrm((d, L0_IN), d ** -0.5)
    inp['l0_conv_w'] = nrm((SSD_CONV, SSD_CONV_CH), SSD_CONV ** -0.5)
    inp['l0_conv_b'] = nrm((SSD_CONV_CH,), 0.01)
    inp['l0_a_log_fwd'] = a_log()
    inp['l0_a_log_bwd'] = a_log()
    inp['l0_dt_bias_fwd'] = dt_bias()
    inp['l0_dt_bias_bwd'] = dt_bias()
    inp['l0_d_skip'] = gain(SSD_HEADS)
    inp['l0_gnorm_w'] = gain(SSD_WIDTH)
    inp['l0_lambda_q1'] = nrm((DA_HEAD_DIM,), 0.1)
    inp['l0_lambda_k1'] = nrm((DA_HEAD_DIM,), 0.1)
    inp['l0_lambda_q2'] = nrm((DA_HEAD_DIM,), 0.1)
    inp['l0_lambda_k2'] = nrm((DA_HEAD_DIM,), 0.1)
    inp['l0_subln_w'] = gain(2 * DA_HEAD_DIM)
    inp['l0_w_out'] = nrm((L0_MIX_OUT, d), L0_MIX_OUT ** -0.5)
    moe_params('l0_')
    inp['l1_norm_mix'] = gain(d)
    inp['l1_norm_ffn'] = gain(d)
    inp['l1_ada_w'] = nrm((d, 6 * d), 0.5 * d ** -0.5)
    inp['l1_ada_b'] = nrm((6 * d,), 0.01)
    inp['l1_w_in'] = nrm((d, 3 * HY_WIDTH), d ** -0.5)
    inp['l1_b_in'] = nrm((3 * HY_WIDTH,), 0.01)
    inp['l1_short_w'] = nrm((HY_SHORT, 3 * HY_WIDTH), HY_SHORT ** -0.5)
    inp['l1_short_b'] = nrm((3 * HY_WIDTH,), 0.01)
    inp['l1_filt_w1'] = nrm((HY_EMB, HY_FILTER_HIDDEN), HY_EMB ** -0.5)
    inp['l1_filt_b1'] = nrm((HY_FILTER_HIDDEN,), 0.1)
    inp['l1_filt_w2'] = nrm((HY_FILTER_HIDDEN, HY_FILTER_HIDDEN), HY_FILTER_HIDDEN ** -0.5)
    inp['l1_filt_b2'] = nrm((HY_FILTER_HIDDEN,), 0.1)
    inp['l1_filt_w3'] = nrm((HY_FILTER_HIDDEN, HY_FILTER_HIDDEN), HY_FILTER_HIDDEN ** -0.5)
    inp['l1_filt_b3'] = nrm((HY_FILTER_HIDDEN,), 0.1)
    inp['l1_filt_w4'] = nrm((HY_FILTER_HIDDEN, HY_WIDTH), 0.05 * HY_FILTER_HIDDEN ** -0.5)
    inp['l1_filt_freq'] = gain(HY_FILTER_HIDDEN)
    inp['l1_filt_bias'] = nrm((HY_WIDTH,), 1.0)
    inp['l1_w_out'] = nrm((HY_WIDTH, d), HY_WIDTH ** -0.5)
    inp['l1_b_out'] = nrm((d,), 0.01)
    moe_params('l1_')
    inp['final_norm'] = gain(d)
    return inp


def reference(x_prompt, x_sample, c, state_l0_ssm_fwd, state_l0_ssm_bwd, cache_l0_k, cache_l0_v, c_ctx,
              l0_norm_mix, l0_norm_ffn, l0_ada_w, l0_ada_b, l0_w_in, l0_conv_w, l0_conv_b,
              l0_a_log_fwd, l0_a_log_bwd, l0_dt_bias_fwd, l0_dt_bias_bwd, l0_d_skip, l0_gnorm_w,
              l0_lambda_q1, l0_lambda_k1, l0_lambda_q2, l0_lambda_k2, l0_subln_w, l0_w_out,
              l0_router_w, l0_router_b, l0_w_gate_up, l0_b_gate_up, l0_w_down, l0_b_down,
              l1_norm_mix, l1_norm_ffn, l1_ada_w, l1_ada_b, l1_w_in, l1_b_in, l1_short_w, l1_short_b,
              l1_filt_w1, l1_filt_b1, l1_filt_w2, l1_filt_b2, l1_filt_w3, l1_filt_b3, l1_filt_w4,
              l1_filt_freq, l1_filt_bias, l1_w_out, l1_b_out,
              l1_router_w, l1_router_b, l1_w_gate_up, l1_b_gate_up, l1_w_down, l1_b_down,
              final_norm):
    norms = ((l0_norm_mix, l0_norm_ffn), (l1_norm_mix, l1_norm_ffn))
    ada = ((l0_ada_w, l0_ada_b), (l1_ada_w, l1_ada_b))
    moe_p = ((l0_router_w, l0_router_b, l0_w_gate_up, l0_b_gate_up, l0_w_down, l0_b_down),
             (l1_router_w, l1_router_b, l1_w_gate_up, l1_b_gate_up, l1_w_down, l1_b_down))
    mixer_p = ((l0_w_in, l0_conv_w, l0_conv_b, l0_a_log_fwd, l0_a_log_bwd, l0_dt_bias_fwd,
                l0_dt_bias_bwd, l0_d_skip, l0_gnorm_w, l0_lambda_q1, l0_lambda_k1, l0_lambda_q2,
                l0_lambda_k2, l0_subln_w, l0_w_out),
               (l1_w_in, l1_b_in, l1_short_w, l1_short_b, l1_filt_w1, l1_filt_b1, l1_filt_w2,
                l1_filt_b2, l1_filt_w3, l1_filt_b3, l1_filt_w4, l1_filt_freq, l1_filt_bias,
                l1_w_out, l1_b_out))

    def run_layers(h, cond, ctx, rope):
        produced = None
        for layer in range(DEPTH):
            sh1, sc1, g1, sh2, sc2, g2 = adaln(cond, *ada[layer])
            hn = rms_norm(h, norms[layer][0]) * (1.0 + sc1) + sh1
            if layer % 2 == 0:
                mix, produced = ssd_diff_mixer(hn, *mixer_p[layer], rope=rope, ctx=ctx)
            else:
                mix = hyena_mixer(hn, *mixer_p[layer])
            h = h + g1 * mix
            hn = rms_norm(h, norms[layer][1]) * (1.0 + sc2) + sh2
            h = h + g2 * moe_ffn(hn, *moe_p[layer])
        return rms_norm(h, final_norm), produced

    y_prompt, ctx_out = run_layers(x_prompt, c_ctx[None, :], None, None)
    new_l0_ssm_fwd, new_l0_ssm_bwd, new_l0_k, new_l0_v = ctx_out
    y_sample, _ = run_layers(x_sample, c,
                             (state_l0_ssm_fwd, state_l0_ssm_bwd, cache_l0_k, cache_l0_v),
                             axial_rope(x_sample.shape[1]))
    return (y_prompt, y_sample, new_l0_ssm_fwd, new_l0_ssm_bwd, new_l0_k, new_l0_v)
```

```python
import functools
import math

import numpy as np
import jax
import jax.numpy as jnp
from jax import lax
from jax.experimental import pallas as pl
from jax.experimental.pallas import tpu as pltpu

D_MODEL = 2048
BATCH = 32
SEQ = 256
DEC_BATCH = 2
DEC_SEQ = 2048
PAST_LEN = 512
GRID_W = 64
NORM_EPS = 1e-6
SSD_WIDTH = D_MODEL // 2
SSD_HEAD_DIM = 64
SSD_HEADS = SSD_WIDTH // SSD_HEAD_DIM
SSD_GROUPS = 2
SSD_STATE = 128
SSD_CONV = 4
SSD_CHUNK = 128
SSD_CONV_CH = SSD_WIDTH + 2 * SSD_GROUPS * SSD_STATE
DA_WIDTH = D_MODEL // 2
DA_HEAD_DIM = 64
DA_HEADS = DA_WIDTH // (2 * DA_HEAD_DIM)
DA_LAMBDA_INIT = 0.8 - 0.6 * math.exp(-0.3 * 0)
Q_BLOCK = 128
ROPE_BASE = 10000.0
ROPE_FREQS = DA_HEAD_DIM // 4
L0_SPLITS = (SSD_WIDTH, SSD_WIDTH, SSD_GROUPS * SSD_STATE, SSD_GROUPS * SSD_STATE,
             SSD_HEADS, SSD_HEADS, DA_WIDTH, DA_WIDTH, DA_WIDTH)
HY_WIDTH = D_MODEL
HY_BANDS = 16
HY_FAST_DECAY = 0.3
HY_SLOW_DECAY = 1.5
HY_TARGET = 1e-2
N_EXPERTS = 32
TOP_K = 4
D_FF = D_MODEL
SWIGLU_LIMIT = 7.0
SWIGLU_ALPHA = 1.702

T_CTX = BATCH * SEQ
T_DEC = DEC_BATCH * DEC_SEQ
T_ALL = T_CTX + T_DEC
N_COND = 1 + DEC_BATCH
MOD_ROWS = 8

V7X_LANES = 128
V7X_MXU_DIM = 256
V7X_VMEM_BYTES = 64 * 1024 * 1024
VMEM_LIMIT = 56 * 1024 * 1024

MOE_TM = 256
MOE_ROWS = T_ALL * TOP_K + N_EXPERTS * MOE_TM
MOE_BLOCKS = MOE_ROWS // MOE_TM


def _cparams(n_grid):
    return pltpu.CompilerParams(dimension_semantics=("arbitrary",) * n_grid,
                                vmem_limit_bytes=VMEM_LIMIT)


def _cond_of_row(row):
    return jnp.where(row < T_CTX, 0, (row - T_CTX) // DEC_SEQ + 1)


def _split_bf16(x):
    hi = x.astype(jnp.bfloat16)
    lo = (x - hi.astype(jnp.float32)).astype(jnp.bfloat16)
    return hi, lo


def _dot3(a, w):
    a_hi, a_lo = _split_bf16(a)
    w_hi, w_lo = _split_bf16(w)
    f32 = jnp.float32
    return (jnp.dot(a_hi, w_hi, preferred_element_type=f32)
            + jnp.dot(a_lo, w_hi, preferred_element_type=f32)
            + jnp.dot(a_hi, w_lo, preferred_element_type=f32))


def _adaln_kernel(c_ref, w_ref, b_ref, o_ref):
    c = c_ref[...]
    a = c * jax.nn.sigmoid(c)
    o_ref[...] = _dot3(a, w_ref[...]) + b_ref[...]


def _adaln(cond8, w, b):
    n = w.shape[1]
    tn = 512
    out = pl.pallas_call(
        _adaln_kernel,
        grid=(n // tn,),
        in_specs=[pl.BlockSpec((MOD_ROWS, D_MODEL), lambda j: (0, 0)),
                  pl.BlockSpec((D_MODEL, tn), lambda j: (0, j)),
                  pl.BlockSpec((1, tn), lambda j: (0, j))],
        out_specs=pl.BlockSpec((MOD_ROWS, tn), lambda j: (0, j)),
        out_shape=jax.ShapeDtypeStruct((MOD_ROWS, n), jnp.float32),
        compiler_params=_cparams(1),
        name="adaln",
    )(cond8, w, b.reshape(1, n))
    m = out[:N_COND].reshape(N_COND, 6, D_MODEL)
    return jnp.pad(m, ((0, 0), (0, MOD_ROWS - 6), (0, 0)))


def _norm_rows(x, g):
    var = jnp.mean(x * x, axis=-1, keepdims=True)
    return x * lax.rsqrt(var + NORM_EPS) * g


def _normmod_kernel(h_ref, g_ref, mod_ref, o_ref, *, sh_row):
    y = _norm_rows(h_ref[...], g_ref[...])
    y = y * (1.0 + mod_ref[sh_row + 1:sh_row + 2, :]) + mod_ref[sh_row:sh_row + 1, :]
    o_ref[...] = y.astype(o_ref.dtype)


def _normmod_router_kernel(h_ref, g_ref, mod_ref, rw_ref, rb_ref, o_ref, lg_ref, *, sh_row):
    y = _norm_rows(h_ref[...], g_ref[...])
    y = y * (1.0 + mod_ref[sh_row + 1:sh_row + 2, :]) + mod_ref[sh_row:sh_row + 1, :]
    o_ref[...] = y.astype(o_ref.dtype)
    lg_ref[...] = _dot3(y, rw_ref[...]) + rb_ref[...]


def _plain_norm_kernel(h_ref, g_ref, o_ref):
    o_ref[...] = _norm_rows(h_ref[...], g_ref[...])


def _normmod(h, g, mod, sh_row, router=None):
    tm = 256
    grid = (T_ALL // tm,)
    h_spec = pl.BlockSpec((tm, D_MODEL), lambda i: (i, 0))
    g_spec = pl.BlockSpec((1, D_MODEL), lambda i: (0, 0))
    mod_spec = pl.BlockSpec((None, MOD_ROWS, D_MODEL), lambda i: (_cond_of_row(i * tm), 0, 0))
    o_spec = pl.BlockSpec((tm, D_MODEL), lambda i: (i, 0))
    o_shape = jax.ShapeDtypeStruct((T_ALL, D_MODEL), jnp.bfloat16)
    if router is None:
        return pl.pallas_call(
            functools.partial(_normmod_kernel, sh_row=sh_row),
            grid=grid, in_specs=[h_spec, g_spec, mod_spec], out_specs=o_spec,
            out_shape=o_shape, compiler_params=_cparams(1), name="normmod",
        )(h, g.reshape(1, D_MODEL), mod)
    rw, rb = router
    return pl.pallas_call(
        functools.partial(_normmod_router_kernel, sh_row=sh_row),
        grid=grid,
        in_specs=[h_spec, g_spec, mod_spec,
                  pl.BlockSpec((D_MODEL, V7X_LANES), lambda i: (0, 0)),
                  pl.BlockSpec((1, V7X_LANES), lambda i: (0, 0))],
        out_specs=[o_spec, pl.BlockSpec((tm, V7X_LANES), lambda i: (i, 0))],
        out_shape=[o_shape, jax.ShapeDtypeStruct((T_ALL, V7X_LANES), jnp.float32)],
        compiler_params=_cparams(1), name="normmod_router",
    )(h, g.reshape(1, D_MODEL), mod, rw, rb)


def _final_norm(h, g, row0, n_rows):
    tm = 256
    blk0 = row0 // tm
    return pl.pallas_call(
        _plain_norm_kernel,
        grid=(n_rows // tm,),
        in_specs=[pl.BlockSpec((tm, D_MODEL), lambda i: (i + blk0, 0)),
                  pl.BlockSpec((1, D_MODEL), lambda i: (0, 0))],
        out_specs=pl.BlockSpec((tm, D_MODEL), lambda i: (i, 0)),
        out_shape=jax.ShapeDtypeStruct((n_rows, D_MODEL), jnp.float32),
        compiler_params=_cparams(1), name="final_norm",
    )(h, g.reshape(1, D_MODEL))


def _mm_kernel(*refs, has_bias, gated, g_row):
    it = iter(refs)
    a_ref, w_ref = next(it), next(it)
    b_ref = next(it) if has_bias else None
    res_ref, mod_ref = (next(it), next(it)) if gated else (None, None)
    o_ref, wbf_ref = next(it), next(it)

    @pl.when(pl.program_id(1) == 0)
    def _():
        wbf_ref[...] = w_ref[...].astype(jnp.bfloat16)

    acc = jnp.dot(a_ref[...], wbf_ref[...], preferred_element_type=jnp.float32)
    if has_bias:
        acc = acc + b_ref[...]
    if gated:
        acc = res_ref[...] + mod_ref[g_row:g_row + 1, :] * acc
    o_ref[...] = acc.astype(o_ref.dtype)


def _matmul(a, w, n_cols, *, col_blk0=0, bias=None, gated=None, out_dtype=jnp.float32,
            tm=1024, tn=512, name="matmul"):
    m, k = a.shape
    assert m % tm == 0 and n_cols % tn == 0
    in_specs = [pl.BlockSpec((tm, k), lambda j, i: (i, 0)),
                pl.BlockSpec((k, tn), lambda j, i: (0, j + col_blk0))]
    args = [a, w]
    if bias is not None:
        in_specs.append(pl.BlockSpec((1, tn), lambda j, i: (0, j + col_blk0)))
        args.append(bias.reshape(1, -1))
    g_row = 0
    if gated is not None:
        res, mod, g_row = gated
        in_specs.append(pl.BlockSpec((tm, tn), lambda j, i: (i, j)))
        in_specs.append(pl.BlockSpec((None, MOD_ROWS, tn),
                                     lambda j, i: (_cond_of_row(i * tm), 0, j)))
        args += [res, mod]
    return pl.pallas_call(
        functools.partial(_mm_kernel, has_bias=bias is not None, gated=gated is not None,
                          g_row=g_row),
        grid=(n_cols // tn, m // tm),
        in_specs=in_specs,
        out_specs=pl.BlockSpec((tm, tn), lambda j, i: (i, j)),
        out_shape=jax.ShapeDtypeStruct((m, n_cols), out_dtype),
        scratch_shapes=[pltpu.VMEM((k, tn), jnp.bfloat16)],
        compiler_params=_cparams(2), name=name,
    )(*args)


def _gmm_kernel(be_ref, nv_ref, x_ref, w_ref, b_ref, *rest, swiglu):
    if swiglu:
        sel_ref, o_ref, wbf_ref = rest
    else:
        o_ref, wbf_ref = rest
    i = pl.program_id(1)
    prev = be_ref[jnp.maximum(i - 1, 0)]

    @pl.when((i == 0) | (be_ref[i] != prev))
    def _():
        wbf_ref[...] = w_ref[...].astype(jnp.bfloat16)

    @pl.when(i < nv_ref[0])
    def _():
        acc = jnp.dot(x_ref[...], wbf_ref[...], preferred_element_type=jnp.float32)
        acc = acc + b_ref[...]
        if not swiglu:
            o_ref[...] = acc.astype(o_ref.dtype)
            return
        tn = acc.shape[1]
        gate = jnp.minimum(acc, SWIGLU_LIMIT)
        up = pltpu.roll(jnp.clip(acc, -SWIGLU_LIMIT, SWIGLU_LIMIT), tn - 1, 1)
        act = (up + 1.0) * (gate * jax.nn.sigmoid(SWIGLU_ALPHA * gate))
        lane = lax.broadcasted_iota(jnp.int32, act.shape, 1)
        act = jnp.where(lane % 2 == 0, act, 0.0).astype(jnp.bfloat16)
        half = V7X_MXU_DIM // 2
        for g in range(tn // V7X_MXU_DIM):
            o_ref[:, g * half:(g + 1) * half] = jnp.dot(
                act[:, g * V7X_MXU_DIM:(g + 1) * V7X_MXU_DIM], sel_ref[...],
                preferred_element_type=jnp.float32).astype(o_ref.dtype)

    @pl.when(i >= nv_ref[0])
    def _():
        o_ref[...] = jnp.zeros_like(o_ref)


def _grouped_matmul(x, w, b, block_e, n_valid, *, swiglu, tn, out_dtype, name):
    rows, k = x.shape
    n = w.shape[2]
    tn_out = tn // 2 if swiglu else tn
    n_out = n // 2 if swiglu else n
    in_specs = [pl.BlockSpec((MOE_TM, k), lambda j, i, be, nv: (i, 0)),
                pl.BlockSpec((None, k, tn), lambda j, i, be, nv: (be[i], 0, j)),
                pl.BlockSpec((None, 1, tn), lambda j, i, be, nv: (be[i], 0, j))]
    args = [x, w, b.reshape(N_EXPERTS, 1, n)]
    if swiglu:
        sel = np.zeros((V7X_MXU_DIM, V7X_MXU_DIM // 2), np.float32)
        sel[2 * np.arange(V7X_MXU_DIM // 2), np.arange(V7X_MXU_DIM // 2)] = 1.0
        in_specs.append(pl.BlockSpec(sel.shape, lambda j, i, be, nv: (0, 0)))
        args.append(jnp.asarray(sel, jnp.bfloat16))
    return pl.pallas_call(
        functools.partial(_gmm_kernel, swiglu=swiglu),
        grid_spec=pltpu.PrefetchScalarGridSpec(
            num_scalar_prefetch=2,
            grid=(n // tn, rows // MOE_TM),
            in_specs=in_specs,
            out_specs=pl.BlockSpec((MOE_TM, tn_out), lambda j, i, be, nv: (i, j)),
            scratch_shapes=[pltpu.VMEM((k, tn), jnp.bfloat16)]),
        out_shape=jax.ShapeDtypeStruct((rows, n_out), out_dtype),
        compiler_params=_cparams(2), name=name,
    )(block_e, n_valid, *args)


def _moe(hn, logits, w_gu, b_gu, w_dn, b_dn):
    t = hn.shape[0]
    top_val, top_idx = lax.top_k(logits[:, :N_EXPERTS], TOP_K)
    gates = jax.nn.softmax(top_val, axis=-1)
    flat_e = top_idx.reshape(-1)
    order = jnp.argsort(flat_e)
    e_sorted = flat_e[order]
    counts = jnp.bincount(flat_e, length=N_EXPERTS)
    padded = (counts + MOE_TM - 1) // MOE_TM * MOE_TM
    start = jnp.cumsum(counts) - counts
    padded_end = jnp.cumsum(padded)
    padded_start = padded_end - padded
    dest = padded_start[e_sorted] + jnp.arange(t * TOP_K) - start[e_sorted]
    row_tok = jnp.zeros((MOE_ROWS,), jnp.int32).at[dest].set((order // TOP_K).astype(jnp.int32))
    pos = jnp.zeros((t * TOP_K,), jnp.int32).at[order].set(dest.astype(jnp.int32))
    block_e = jnp.minimum(
        jnp.searchsorted(padded_end, jnp.arange(MOE_BLOCKS) * MOE_TM, side='right'),
        N_EXPERTS - 1).astype(jnp.int32)
    n_valid = (padded_end[-1] // MOE_TM).astype(jnp.int32).reshape(1)

    x_rows = hn[row_tok]
    act = _grouped_matmul(x_rows, w_gu, b_gu, block_e, n_valid, swiglu=True, tn=1024,
                          out_dtype=jnp.bfloat16, name="moe_gate_up")
    y_rows = _grouped_matmul(act, w_dn, b_dn, block_e, n_valid, swiglu=False, tn=1024,
                             out_dtype=jnp.float32, name="moe_down")
    y = y_rows[pos].reshape(t, TOP_K, D_MODEL)
    return jnp.einsum('tk,tkd->td', gates, y)


def _dw_conv_centred(x, w, b):
    width = w.shape[0]
    left = (width - 1) // 2
    y = lax.conv_general_dilated(x, w[:, None, :].astype(x.dtype), window_strides=(1,),
                                 padding=[(left, width - 1 - left)],
                                 dimension_numbers=('NWC', 'WIO', 'NWC'),
                                 feature_group_count=x.shape[-1])
    return y + b


def _axial_rope(n_tokens):
    rows = n_tokens // GRID_W
    row_id = jnp.repeat(jnp.arange(rows), GRID_W)
    col_id = jnp.tile(jnp.arange(GRID_W), rows)
    inv = 1.0 / (ROPE_BASE ** (jnp.arange(ROPE_FREQS, dtype=jnp.float32) / ROPE_FREQS))
    ang = jnp.stack([row_id, col_id], axis=-1).astype(jnp.float32)[:, :, None] * inv
    return jnp.cos(ang), jnp.sin(ang)


def _apply_axial_rope(x, cos, sin):
    xs = x.astype(jnp.float32).reshape(x.shape[:-1] + (2, 2, ROPE_FREQS))
    x1, x2 = xs[..., 0, :], xs[..., 1, :]
    c, s = cos[None, :, None], sin[None, :, None]
    out = jnp.stack([x1 * c - x2 * s, x2 * c + x1 * s], axis=-2)
    return out.reshape(x.shape).astype(x.dtype)


def _ssd_scan(x, dt, a, bm, cm, h0):
    b, L, H, P = x.shape
    G, N = bm.shape[-2:]
    R = H // G
    Q = SSD_CHUNK
    nc = L // Q
    xdt = (x.astype(jnp.float32) * dt[..., None]).reshape(b, nc, Q, G, R, P)
    a_cs = jnp.cumsum((dt * a).reshape(b, nc, Q, G, R), axis=2)
    bc = bm.astype(jnp.float32).reshape(b, nc, Q, G, N)
    cc = cm.astype(jnp.float32).reshape(b, nc, Q, G, N)
    seg = a_cs[:, :, :, None] - a_cs[:, :, None, :]
    tri = jnp.tril(jnp.ones((Q, Q), bool))[:, :, None, None]
    decay = jnp.exp(jnp.where(tri, seg, -jnp.inf))
    cb = jnp.einsum('bcign,bcjgn->bcijg', cc, bc)
    y_diag = jnp.einsum('bcijgr,bcjgrp->bcigrp', decay * cb[..., None], xdt)
    decay_end = jnp.exp(a_cs[:, :, -1:] - a_cs)
    states = jnp.einsum('bcjgn,bcjgrp->bcgrpn', bc, xdt * decay_end[..., None])
    chunk_decay = jnp.exp(a_cs[:, :, -1])

    def carry_step(h, inp):
        st, dec = inp
        return h * dec[..., None, None] + st, h

    h_last, h_in = lax.scan(carry_step, h0.astype(jnp.float32).reshape(b, G, R, P, N),
                            (jnp.moveaxis(states, 1, 0), jnp.moveaxis(chunk_decay, 1, 0)))
    h_in = jnp.moveaxis(h_in, 0, 1)
    y_off = jnp.einsum('bcign,bcgrpn->bcigrp', cc, h_in) * jnp.exp(a_cs)[..., None]
    y = (y_diag + y_off).reshape(b, L, H, P)
    return y.astype(x.dtype), h_last.reshape(b, H, P, N)


def _diff_attention(q, k, v, lam):
    b, Lq = q.shape[:2]
    qb = jnp.moveaxis(q.reshape((b, Lq // Q_BLOCK, Q_BLOCK) + q.shape[2:]), 1, 0)
    scale = DA_HEAD_DIM ** -0.5

    def one_block(qblk):
        s = jnp.einsum('bqhmd,bkhmd->bhmqk', qblk, k).astype(jnp.float32) * scale
        p = jax.nn.softmax(s, axis=-1)
        w = p[:, :, 0] - lam * p[:, :, 1]
        return jnp.einsum('bhqk,bkhe->bqhe', w.astype(v.dtype), v)

    out = lax.map(one_block, qb)
    return jnp.moveaxis(out, 0, 1).reshape(b, Lq, q.shape[2], v.shape[-1])


def _rms(x, g):
    return x * lax.rsqrt(jnp.mean(x * x, axis=-1, keepdims=True) + NORM_EPS) * g


def _ssd_diff_core(z_ssd, z_dt, z_qkv, p, rope=None, ctx=None):
    (conv_w, conv_b, a_log_fwd, a_log_bwd, dt_bias_fwd, dt_bias_bwd, d_skip, gnorm_w,
     lq1, lk1, lq2, lk2, subln_w) = p
    b, L, _ = z_ssd.shape
    s_x, s_z, s_b, s_c = jnp.split(z_ssd, [1024, 2048, 2304], axis=-1)
    s_dtf, s_dtb = z_dt[..., :SSD_HEADS], z_dt[..., SSD_HEADS:2 * SSD_HEADS]
    q, k, v = jnp.split(z_qkv, 3, axis=-1)
    xbc = jax.nn.silu(_dw_conv_centred(jnp.concatenate([s_x, s_b, s_c], axis=-1), conv_w, conv_b))
    xs, bm, cm = jnp.split(xbc, [SSD_WIDTH, SSD_WIDTH + SSD_GROUPS * SSD_STATE], axis=-1)
    xs = xs.reshape(b, L, SSD_HEADS, SSD_HEAD_DIM)
    bm = bm.reshape(b, L, SSD_GROUPS, SSD_STATE)
    cm = cm.reshape(b, L, SSD_GROUPS, SSD_STATE)
    if ctx is None:
        h0_fwd = jnp.zeros((b, SSD_HEADS, SSD_HEAD_DIM, SSD_STATE), jnp.float32)
        h0_bwd = h0_fwd
    else:
        h0_fwd, h0_bwd, k_ctx, v_ctx = ctx
    dt_fwd = jax.nn.softplus(s_dtf + dt_bias_fwd)
    dt_bwd = jax.nn.softplus(s_dtb + dt_bias_bwd)
    flip = lambda t: jnp.flip(t, axis=1)
    y_fwd, hT_fwd = _ssd_scan(xs, dt_fwd, -jnp.exp(a_log_fwd), bm, cm, h0_fwd)
    y_bwd, hT_bwd = _ssd_scan(flip(xs), flip(dt_bwd), -jnp.exp(a_log_bwd), flip(bm), flip(cm),
                              h0_bwd)
    y = y_fwd + flip(y_bwd) + d_skip[:, None] * xs
    y = y.reshape(b, L, SSD_WIDTH) * jax.nn.silu(s_z)
    y_ssd = _rms(y.reshape(b, L, SSD_GROUPS, -1),
                 gnorm_w.reshape(SSD_GROUPS, -1)).reshape(b, L, SSD_WIDTH)
    q = q.reshape(b, L, DA_HEADS, 2, DA_HEAD_DIM)
    k = k.reshape(b, L, DA_HEADS, 2, DA_HEAD_DIM)
    v = v.reshape(b, L, DA_HEADS, 2 * DA_HEAD_DIM)
    if ctx is None:
        k_all, v_all = k, v
    else:
        cos, sin = rope
        q = _apply_axial_rope(q.reshape(b, L, 2 * DA_HEADS, DA_HEAD_DIM), cos, sin).reshape(q.shape)
        k = _apply_axial_rope(k.reshape(b, L, 2 * DA_HEADS, DA_HEAD_DIM), cos, sin).reshape(k.shape)
        k_all = jnp.concatenate([k_ctx.reshape(b, -1, DA_HEADS, 2, DA_HEAD_DIM), k], axis=1)
        v_all = jnp.concatenate([v_ctx, v], axis=1)
    lam = jnp.exp(jnp.sum(lq1 * lk1)) - jnp.exp(jnp.sum(lq2 * lk2)) + DA_LAMBDA_INIT
    o = _diff_attention(q, k_all, v_all, lam)
    o = (_rms(o, subln_w) * (1.0 - DA_LAMBDA_INIT)).reshape(b, L, DA_WIDTH)
    mix = jnp.concatenate([y_ssd, o], axis=-1).reshape(b * L, 2 * SSD_WIDTH)
    produced = (hT_fwd, hT_bwd, k.reshape(b, L, DA_HEADS, 2 * DA_HEAD_DIM), v)
    return mix, produced


def _hyena_filter(L, w1, b1, w2, b2, w3, b3, w4, freq):
    t = jnp.arange(L, dtype=jnp.float32)
    bands = jnp.linspace(1e-4, HY_BANDS - 1, HY_BANDS, dtype=jnp.float32)
    ang = (2.0 * math.pi / L) * t[:, None] * bands
    z = jnp.concatenate([(t / L)[:, None], jnp.cos(ang), jnp.sin(ang)], axis=-1)
    h = jnp.sin(freq * (z @ w1 + b1))
    h = jnp.sin(freq * (h @ w2 + b2))
    h = jnp.sin(freq * (h @ w3 + b3))
    h = h @ w4
    deltas = jnp.abs(jnp.linspace(math.log(HY_TARGET) / HY_SLOW_DECAY,
                                  math.log(HY_TARGET) / HY_FAST_DECAY, w4.shape[1],
                                  dtype=jnp.float32))
    dist = jnp.abs(t - L // 2) / L
    return h * jnp.exp(-dist[:, None] * deltas)


def _fft_conv_centred(v, filt):
    L = v.shape[1]
    vf = jnp.fft.rfft(v, n=2 * L, axis=1)
    hf = jnp.fft.rfft(filt, n=2 * L, axis=0)
    return jnp.fft.irfft(vf * hf[None], n=2 * L, axis=1)[:, L // 2: L // 2 + L]


def _hyena_core(u, short_w, short_b, filt_p, f_bias):
    b, L, _ = u.shape
    u = _dw_conv_centred(u, short_w, short_b)
    x0, x1, v = jnp.split(u, 3, axis=-1)
    filt = _hyena_filter(L, *filt_p)
    v = v * x1
    y = (_fft_conv_centred(v, filt) + v * f_bias) * x0
    return y.reshape(b * L, HY_WIDTH)


def kernel(x_prompt, x_sample, c, state_l0_ssm_fwd, state_l0_ssm_bwd, cache_l0_k, cache_l0_v, c_ctx, l0_norm_mix, l0_norm_ffn, l0_ada_w, l0_ada_b, l0_w_in, l0_conv_w, l0_conv_b, l0_a_log_fwd, l0_a_log_bwd, l0_dt_bias_fwd, l0_dt_bias_bwd, l0_d_skip, l0_gnorm_w, l0_lambda_q1, l0_lambda_k1, l0_lambda_q2, l0_lambda_k2, l0_subln_w, l0_w_out, l0_router_w, l0_router_b, l0_w_gate_up, l0_b_gate_up, l0_w_down, l0_b_down, l1_norm_mix, l1_norm_ffn, l1_ada_w, l1_ada_b, l1_w_in, l1_b_in, l1_short_w, l1_short_b, l1_filt_w1, l1_filt_b1, l1_filt_w2, l1_filt_b2, l1_filt_w3, l1_filt_b3, l1_filt_w4, l1_filt_freq, l1_filt_bias, l1_w_out, l1_b_out, l1_router_w, l1_router_b, l1_w_gate_up, l1_b_gate_up, l1_w_down, l1_b_down, final_norm):
    h = jnp.concatenate([x_prompt.reshape(T_CTX, D_MODEL), x_sample.reshape(T_DEC, D_MODEL)], axis=0)
    cond8 = jnp.concatenate([c_ctx[None, :], c,
                             jnp.zeros((MOD_ROWS - N_COND, D_MODEL), jnp.float32)], axis=0)

    def router_args(rw, rb):
        pad = V7X_LANES - N_EXPERTS
        return (jnp.pad(rw, ((0, 0), (0, pad))),
                jnp.pad(rb, (0, pad), constant_values=-1e30).reshape(1, V7X_LANES))

    def ffn(h, mod, norm_g, rw, rb, w_gu, b_gu, w_dn, b_dn):
        hn, logits = _normmod(h, norm_g, mod, 3, router=router_args(rw, rb))
        y = _moe(hn, logits, w_gu, b_gu, w_dn, b_dn)
        g2 = jnp.repeat(mod[:, 5, :], np.array([T_CTX] + [DEC_SEQ] * DEC_BATCH), axis=0,
                        total_repeat_length=T_ALL)
        return h + g2 * y

    mod0 = _adaln(cond8, l0_ada_w, l0_ada_b)
    hn = _normmod(h, l0_norm_mix, mod0, 0)
    z_ssd = _matmul(hn, l0_w_in, 2560, name="l0_in_ssd")
    w_dt = jnp.pad(l0_w_in[:, 2560:2592], ((0, 0), (0, V7X_LANES - 2 * SSD_HEADS)))
    z_dt = _matmul(hn, w_dt, V7X_LANES, tn=V7X_LANES, name="l0_in_dt")
    z_qkv = _matmul(hn, l0_w_in[:, 2592:], 3 * DA_WIDTH, name="l0_in_qkv")
    mixer_p = (l0_conv_w, l0_conv_b, l0_a_log_fwd, l0_a_log_bwd, l0_dt_bias_fwd, l0_dt_bias_bwd,
               l0_d_skip, l0_gnorm_w, l0_lambda_q1, l0_lambda_k1, l0_lambda_q2, l0_lambda_k2,
               l0_subln_w)
    mix_ctx, produced = _ssd_diff_core(
        z_ssd[:T_CTX].reshape(BATCH, SEQ, -1), z_dt[:T_CTX].reshape(BATCH, SEQ, -1),
        z_qkv[:T_CTX].reshape(BATCH, SEQ, -1), mixer_p)
    mix_dec, _ = _ssd_diff_core(
        z_ssd[T_CTX:].reshape(DEC_BATCH, DEC_SEQ, -1), z_dt[T_CTX:].reshape(DEC_BATCH, DEC_SEQ, -1),
        z_qkv[T_CTX:].reshape(DEC_BATCH, DEC_SEQ, -1), mixer_p, rope=_axial_rope(DEC_SEQ),
        ctx=(state_l0_ssm_fwd, state_l0_ssm_bwd, cache_l0_k, cache_l0_v))
    mix = jnp.concatenate([mix_ctx, mix_dec], axis=0).astype(jnp.bfloat16)
    h = _matmul(mix, l0_w_out, D_MODEL, gated=(h, mod0, 2), name="l0_out")
    h = ffn(h, mod0, l0_norm_ffn, l0_router_w, l0_router_b, l0_w_gate_up, l0_b_gate_up,
            l0_w_down, l0_b_down)

    mod1 = _adaln(cond8, l1_ada_w, l1_ada_b)
    hn = _normmod(h, l1_norm_mix, mod1, 0)
    u = _matmul(hn, l1_w_in, 3 * HY_WIDTH, bias=l1_b_in, name="l1_in")
    filt_p = (l1_filt_w1, l1_filt_b1, l1_filt_w2, l1_filt_b2, l1_filt_w3, l1_filt_b3, l1_filt_w4,
              l1_filt_freq)
    y_ctx = _hyena_core(u[:T_CTX].reshape(BATCH, SEQ, -1), l1_short_w, l1_short_b, filt_p,
                        l1_filt_bias)
    y_dec = _hyena_core(u[T_CTX:].reshape(DEC_BATCH, DEC_SEQ, -1), l1_short_w, l1_short_b, filt_p,
                        l1_filt_bias)
    y = jnp.concatenate([y_ctx, y_dec], axis=0).astype(jnp.bfloat16)
    h = _matmul(y, l1_w_out, D_MODEL, bias=l1_b_out, gated=(h, mod1, 2), name="l1_out")
    h = ffn(h, mod1, l1_norm_ffn, l1_router_w, l1_router_b, l1_w_gate_up, l1_b_gate_up,
            l1_w_down, l1_b_down)

    y_prompt = _final_norm(h, final_norm, 0, T_CTX).reshape(BATCH, SEQ, D_MODEL)
    y_sample = _final_norm(h, final_norm, T_CTX, T_DEC).reshape(DEC_BATCH, DEC_SEQ, D_MODEL)
    return (y_prompt, y_sample) + produced
```

```python
import functools
import math

import numpy as np
import jax
import jax.numpy as jnp
from jax import lax
from jax.experimental import pallas as pl
from jax.experimental.pallas import tpu as pltpu

D_MODEL = 2048
BATCH = 32
SEQ = 256
DEC_BATCH = 2
DEC_SEQ = 2048
PAST_LEN = 512
GRID_W = 64
NORM_EPS = 1e-6
SSD_WIDTH = D_MODEL // 2
SSD_HEAD_DIM = 64
SSD_HEADS = SSD_WIDTH // SSD_HEAD_DIM
SSD_GROUPS = 2
SSD_STATE = 128
SSD_CHUNK = 128
SSD_CONV_CH = SSD_WIDTH + 2 * SSD_GROUPS * SSD_STATE
DA_WIDTH = D_MODEL // 2
DA_HEAD_DIM = 64
DA_HEADS = DA_WIDTH // (2 * DA_HEAD_DIM)
DA_LAMBDA_INIT = 0.8 - 0.6 * math.exp(-0.3 * 0)
ROPE_BASE = 10000.0
ROPE_FREQS = DA_HEAD_DIM // 4
L0_SPLITS = (SSD_WIDTH, SSD_WIDTH, SSD_GROUPS * SSD_STATE, SSD_GROUPS * SSD_STATE,
             SSD_HEADS, SSD_HEADS, DA_WIDTH, DA_WIDTH, DA_WIDTH)
HY_WIDTH = D_MODEL
HY_BANDS = 16
HY_FAST_DECAY = 0.3
HY_SLOW_DECAY = 1.5
HY_TARGET = 1e-2
N_EXPERTS = 32
TOP_K = 4
SWIGLU_LIMIT = 7.0
SWIGLU_ALPHA = 1.702

T_CTX = BATCH * SEQ
T_DEC = DEC_BATCH * DEC_SEQ
T_ALL = T_CTX + T_DEC
N_COND = 1 + DEC_BATCH
MOD_ROWS = 8

V7X_LANES = 128
V7X_MXU_DIM = 256
V7X_VMEM_BYTES = 64 * 1024 * 1024
VMEM_LIMIT = V7X_VMEM_BYTES - 8 * 1024 * 1024

MOE_TM = 256
MOE_ROWS = T_ALL * TOP_K + N_EXPERTS * MOE_TM
MOE_BLOCKS = MOE_ROWS // MOE_TM


def _cparams(n_grid):
    return pltpu.CompilerParams(dimension_semantics=("arbitrary",) * n_grid,
                                vmem_limit_bytes=VMEM_LIMIT)


def _cond_of_row(row):
    return jnp.where(row < T_CTX, 0, (row - T_CTX) // DEC_SEQ + 1)


def _split_bf16(x):
    hi = x.astype(jnp.bfloat16)
    lo = (x - hi.astype(jnp.float32)).astype(jnp.bfloat16)
    return hi, lo


def _dot3(a, w):
    a_hi, a_lo = _split_bf16(a)
    w_hi, w_lo = _split_bf16(w)
    f32 = jnp.float32
    return (jnp.dot(a_hi, w_hi, preferred_element_type=f32)
            + jnp.dot(a_lo, w_hi, preferred_element_type=f32)
            + jnp.dot(a_hi, w_lo, preferred_element_type=f32))


def _adaln_kernel(c_ref, w_ref, b_ref, o_ref):
    c = c_ref[...]
    a = c * jax.nn.sigmoid(c)
    o_ref[...] = _dot3(a, w_ref[...]) + b_ref[...]


def _adaln(cond8, w, b):
    n = w.shape[1]
    tn = 512
    out = pl.pallas_call(
        _adaln_kernel,
        grid=(n // tn,),
        in_specs=[pl.BlockSpec((MOD_ROWS, D_MODEL), lambda j: (0, 0)),
                  pl.BlockSpec((D_MODEL, tn), lambda j: (0, j)),
                  pl.BlockSpec((1, tn), lambda j: (0, j))],
        out_specs=pl.BlockSpec((MOD_ROWS, tn), lambda j: (0, j)),
        out_shape=jax.ShapeDtypeStruct((MOD_ROWS, n), jnp.float32),
        compiler_params=_cparams(1),
        name="adaln",
    )(cond8, w, b.reshape(1, n))
    m = out[:N_COND].reshape(N_COND, 6, D_MODEL)
    return jnp.pad(m, ((0, 0), (0, MOD_ROWS - 6), (0, 0)))


def _norm_rows(x, g):
    var = jnp.mean(x * x, axis=-1, keepdims=True)
    return x * lax.rsqrt(var + NORM_EPS) * g


def _normmod_kernel(h_ref, g_ref, mod_ref, o_ref, *, sh_row):
    y = _norm_rows(h_ref[...], g_ref[...])
    y = y * (1.0 + mod_ref[sh_row + 1:sh_row + 2, :]) + mod_ref[sh_row:sh_row + 1, :]
    o_ref[...] = y.astype(o_ref.dtype)


def _normmod_router_kernel(h_ref, g_ref, mod_ref, rw_ref, rb_ref, o_ref, lg_ref, *, sh_row):
    y = _norm_rows(h_ref[...], g_ref[...])
    y = y * (1.0 + mod_ref[sh_row + 1:sh_row + 2, :]) + mod_ref[sh_row:sh_row + 1, :]
    o_ref[...] = y.astype(o_ref.dtype)
    lg_ref[...] = _dot3(y, rw_ref[...]) + rb_ref[...]


def _plain_norm_kernel(h_ref, g_ref, o_ref):
    o_ref[...] = _norm_rows(h_ref[...], g_ref[...])


def _normmod(h, g, mod, sh_row, router=None):
    tm = 256
    grid = (T_ALL // tm,)
    h_spec = pl.BlockSpec((tm, D_MODEL), lambda i: (i, 0))
    g_spec = pl.BlockSpec((1, D_MODEL), lambda i: (0, 0))
    mod_spec = pl.BlockSpec((None, MOD_ROWS, D_MODEL), lambda i: (_cond_of_row(i * tm), 0, 0))
    o_spec = pl.BlockSpec((tm, D_MODEL), lambda i: (i, 0))
    o_shape = jax.ShapeDtypeStruct((T_ALL, D_MODEL), jnp.bfloat16)
    if router is None:
        return pl.pallas_call(
            functools.partial(_normmod_kernel, sh_row=sh_row),
            grid=grid, in_specs=[h_spec, g_spec, mod_spec], out_specs=o_spec,
            out_shape=o_shape, compiler_params=_cparams(1), name="normmod",
        )(h, g.reshape(1, D_MODEL), mod)
    rw, rb = router
    return pl.pallas_call(
        functools.partial(_normmod_router_kernel, sh_row=sh_row),
        grid=grid,
        in_specs=[h_spec, g_spec, mod_spec,
                  pl.BlockSpec((D_MODEL, V7X_LANES), lambda i: (0, 0)),
                  pl.BlockSpec((1, V7X_LANES), lambda i: (0, 0))],
        out_specs=[o_spec, pl.BlockSpec((tm, V7X_LANES), lambda i: (i, 0))],
        out_shape=[jax.ShapeDtypeStruct((T_ALL, D_MODEL), jnp.float32),
                   jax.ShapeDtypeStruct((T_ALL, V7X_LANES), jnp.float32)],
        compiler_params=_cparams(1), name="normmod_router",
    )(h, g.reshape(1, D_MODEL), mod, rw, rb)


def _final_norm(h, g, row0, n_rows):
    tm = 256
    blk0 = row0 // tm
    return pl.pallas_call(
        _plain_norm_kernel,
        grid=(n_rows // tm,),
        in_specs=[pl.BlockSpec((tm, D_MODEL), lambda i: (i + blk0, 0)),
                  pl.BlockSpec((1, D_MODEL), lambda i: (0, 0))],
        out_specs=pl.BlockSpec((tm, D_MODEL), lambda i: (i, 0)),
        out_shape=jax.ShapeDtypeStruct((n_rows, D_MODEL), jnp.float32),
        compiler_params=_cparams(1), name="final_norm",
    )(h, g.reshape(1, D_MODEL))


def _mm_kernel(*refs, two_a, has_bias, gated, g_row):
    it = iter(refs)
    a_ref = next(it)
    a2_ref = next(it) if two_a else None
    w_ref = next(it)
    b_ref = next(it) if has_bias else None
    res_ref, mod_ref = (next(it), next(it)) if gated else (None, None)
    o_ref, wbf_ref = next(it), next(it)

    @pl.when(pl.program_id(1) == 0)
    def _():
        wbf_ref[...] = w_ref[...].astype(jnp.bfloat16)

    k1 = a_ref.shape[1]
    acc = jnp.dot(a_ref[...], wbf_ref[:k1, :], preferred_element_type=jnp.float32)
    if two_a:
        acc = acc + jnp.dot(a2_ref[...], wbf_ref[k1:, :], preferred_element_type=jnp.float32)
    if has_bias:
        acc = acc + b_ref[...]
    if gated:
        acc = res_ref[...] + mod_ref[g_row:g_row + 1, :] * acc
    o_ref[...] = acc.astype(o_ref.dtype)


def _matmul(a, w, n_cols, *, a2=None, col_blk0=0, bias=None, gated=None,
            out_dtype=jnp.float32, tm=1024, tn=512, name="matmul"):
    m, k1 = a.shape
    k = w.shape[0]
    assert m % tm == 0 and n_cols % tn == 0
    in_specs = [pl.BlockSpec((tm, k1), lambda j, i: (i, 0))]
    args = [a]
    if a2 is not None:
        assert k1 + a2.shape[1] == k
        in_specs.append(pl.BlockSpec((tm, k - k1), lambda j, i: (i, 0)))
        args.append(a2)
    in_specs.append(pl.BlockSpec((k, tn), lambda j, i: (0, j + col_blk0)))
    args.append(w)
    if bias is not None:
        in_specs.append(pl.BlockSpec((1, tn), lambda j, i: (0, j + col_blk0)))
        args.append(bias.reshape(1, -1))
    g_row = 0
    if gated is not None:
        res, mod, g_row = gated
        in_specs.append(pl.BlockSpec((tm, tn), lambda j, i: (i, j)))
        in_specs.append(pl.BlockSpec((None, MOD_ROWS, tn),
                                     lambda j, i: (_cond_of_row(i * tm), 0, j)))
        args += [res, mod]
    return pl.pallas_call(
        functools.partial(_mm_kernel, two_a=a2 is not None, has_bias=bias is not None,
                          gated=gated is not None, g_row=g_row),
        grid=(n_cols // tn, m // tm),
        in_specs=in_specs,
        out_specs=pl.BlockSpec((tm, tn), lambda j, i: (i, j)),
        out_shape=jax.ShapeDtypeStruct((m, n_cols), out_dtype),
        scratch_shapes=[pltpu.VMEM((k, tn), jnp.bfloat16)],
        compiler_params=_cparams(2), name=name,
    )(*args)


def _gmm_kernel(be_ref, nv_ref, x_ref, w_ref, b_ref, *rest, swiglu):
    if swiglu:
        sel_ref, o_ref, wbf_ref = rest
    else:
        o_ref, wbf_ref = rest
    i = pl.program_id(1)
    prev = be_ref[jnp.maximum(i - 1, 0)]

    @pl.when((i == 0) | (be_ref[i] != prev))
    def _():
        wbf_ref[...] = w_ref[...].astype(jnp.bfloat16)

    @pl.when(i < nv_ref[0])
    def _():
        f32 = jnp.float32
        if not swiglu:
            acc = jnp.dot(x_ref[...], wbf_ref[...], preferred_element_type=f32) + b_ref[...]
            o_ref[...] = acc.astype(o_ref.dtype)
            return
        wide, half = V7X_MXU_DIM, V7X_MXU_DIM // 2
        lane = lax.broadcasted_iota(jnp.int32, (MOE_TM, wide), 1)
        for g in range(w_ref.shape[1] // wide):
            acc = jnp.dot(x_ref[...], wbf_ref[:, g * wide:(g + 1) * wide],
                          preferred_element_type=f32) + b_ref[:, g * wide:(g + 1) * wide]
            gate = jnp.minimum(acc, SWIGLU_LIMIT)
            up = pltpu.roll(jnp.clip(acc, -SWIGLU_LIMIT, SWIGLU_LIMIT), wide - 1, 1)
            act = (up + 1.0) * (gate * jax.nn.sigmoid(SWIGLU_ALPHA * gate))
            act = jnp.where(lane % 2 == 0, act, 0.0).astype(jnp.bfloat16)
            o_ref[:, g * half:(g + 1) * half] = jnp.dot(
                act, sel_ref[...], preferred_element_type=f32).astype(o_ref.dtype)

    @pl.when(i >= nv_ref[0])
    def _():
        o_ref[...] = jnp.zeros_like(o_ref)


def _grouped_matmul(x, w, b, block_e, n_valid, *, swiglu, tn, out_dtype, name):
    rows, k = x.shape
    n = w.shape[2]
    tn_out = tn // 2 if swiglu else tn
    n_out = n // 2 if swiglu else n
    in_specs = [pl.BlockSpec((MOE_TM, k), lambda j, i, be, nv: (i, 0)),
                pl.BlockSpec((None, k, tn), lambda j, i, be, nv: (be[i], 0, j)),
                pl.BlockSpec((None, 1, tn), lambda j, i, be, nv: (be[i], 0, j))]
    args = [x, w, b.reshape(N_EXPERTS, 1, n)]
    if swiglu:
        sel = np.zeros((V7X_MXU_DIM, V7X_MXU_DIM // 2), np.float32)
        sel[2 * np.arange(V7X_MXU_DIM // 2), np.arange(V7X_MXU_DIM // 2)] = 1.0
        in_specs.append(pl.BlockSpec(sel.shape, lambda j, i, be, nv: (0, 0)))
        args.append(jnp.asarray(sel, jnp.bfloat16))
    return pl.pallas_call(
        functools.partial(_gmm_kernel, swiglu=swiglu),
        grid_spec=pltpu.PrefetchScalarGridSpec(
            num_scalar_prefetch=2,
            grid=(n // tn, rows // MOE_TM),
            in_specs=in_specs,
            out_specs=pl.BlockSpec((MOE_TM, tn_out), lambda j, i, be, nv: (i, j)),
            scratch_shapes=[pltpu.VMEM((k, tn), jnp.bfloat16)]),
        out_shape=jax.ShapeDtypeStruct((rows, n_out), out_dtype),
        compiler_params=_cparams(2), name=name,
    )(block_e, n_valid, *args)


def _dispatch_kernel(nv_ref, tok_ref, tok_next_ref, hn_hbm, o_ref, buf, sem):
    i = pl.program_id(0)
    slot = i % 2

    def issue(idx_ref, s):
        def body(r, carry):
            pltpu.make_async_copy(hn_hbm.at[pl.ds(idx_ref[0, r], 1)],
                                  buf.at[s, pl.ds(r, 1)], sem.at[s]).start()
            return carry
        lax.fori_loop(0, MOE_TM, body, 0, unroll=8)

    @pl.when((i == 0) & (nv_ref[0] > 0))
    def _():
        issue(tok_ref, 0)

    @pl.when(i + 1 < nv_ref[0])
    def _():
        issue(tok_next_ref, 1 - slot)

    @pl.when(i < nv_ref[0])
    def _():
        pltpu.make_async_copy(hn_hbm.at[pl.ds(0, MOE_TM)], buf.at[slot], sem.at[slot]).wait()
        o_ref[...] = buf[slot].astype(o_ref.dtype)

    @pl.when(i >= nv_ref[0])
    def _():
        o_ref[...] = jnp.zeros_like(o_ref)


def _moe_dispatch(hn, row_tok, n_valid):
    tok3 = row_tok.reshape(MOE_BLOCKS, 1, MOE_TM)
    smem_blk = lambda f: pl.BlockSpec((None, 1, MOE_TM), f, memory_space=pltpu.SMEM)
    return pl.pallas_call(
        _dispatch_kernel,
        grid_spec=pltpu.PrefetchScalarGridSpec(
            num_scalar_prefetch=1,
            grid=(MOE_BLOCKS,),
            in_specs=[smem_blk(lambda i, nv: (i, 0, 0)),
                      smem_blk(lambda i, nv: (jnp.minimum(i + 1, MOE_BLOCKS - 1), 0, 0)),
                      pl.BlockSpec(memory_space=pl.ANY)],
            out_specs=pl.BlockSpec((MOE_TM, D_MODEL), lambda i, nv: (i, 0)),
            scratch_shapes=[pltpu.VMEM((2, MOE_TM, D_MODEL), jnp.float32),
                            pltpu.SemaphoreType.DMA((2,))]),
        out_shape=jax.ShapeDtypeStruct((MOE_ROWS, D_MODEL), jnp.bfloat16),
        compiler_params=_cparams(1), name="moe_dispatch",
    )(n_valid, tok3, tok3, hn)


COMBINE_TM = 128


def _combine_kernel(pos_ref, pos_next_ref, h_ref, gates_ref, mod_ref, y_hbm, o_ref, buf, sem,
                    *, g_row):
    i = pl.program_id(0)
    slot = i % 2
    n_rows = TOP_K * COMBINE_TM

    def issue(idx_ref, s):
        def body(r, carry):
            pltpu.make_async_copy(y_hbm.at[pl.ds(idx_ref[0, r], 1)],
                                  buf.at[s, pl.ds(r, 1)], sem.at[s]).start()
            return carry
        lax.fori_loop(0, n_rows, body, 0, unroll=8)

    @pl.when(i == 0)
    def _():
        issue(pos_ref, 0)

    @pl.when(i + 1 < pl.num_programs(0))
    def _():
        issue(pos_next_ref, 1 - slot)

    pltpu.make_async_copy(y_hbm.at[pl.ds(0, n_rows)], buf.at[slot], sem.at[slot]).wait()
    acc = gates_ref[:, 0:1] * buf[slot, 0:COMBINE_TM, :]
    for k in range(1, TOP_K):
        acc = acc + gates_ref[:, k:k + 1] * buf[slot, k * COMBINE_TM:(k + 1) * COMBINE_TM, :]
    o_ref[...] = h_ref[...] + mod_ref[g_row:g_row + 1, :] * acc


def _moe_combine(h, y_rows, pos, gates, mod, g_row):
    t = h.shape[0]
    nblk = t // COMBINE_TM
    pos3 = pos.reshape(nblk, COMBINE_TM, TOP_K).transpose(0, 2, 1).reshape(nblk, 1, TOP_K * COMBINE_TM)
    gates_pad = jnp.pad(gates, ((0, 0), (0, V7X_LANES - TOP_K)))
    smem_blk = lambda f: pl.BlockSpec((None, 1, TOP_K * COMBINE_TM), f, memory_space=pltpu.SMEM)
    return pl.pallas_call(
        functools.partial(_combine_kernel, g_row=g_row),
        grid=(nblk,),
        in_specs=[smem_blk(lambda i: (i, 0, 0)),
                  smem_blk(lambda i: (jnp.minimum(i + 1, nblk - 1), 0, 0)),
                  pl.BlockSpec((COMBINE_TM, D_MODEL), lambda i: (i, 0)),
                  pl.BlockSpec((COMBINE_TM, V7X_LANES), lambda i: (i, 0)),
                  pl.BlockSpec((None, MOD_ROWS, D_MODEL),
                               lambda i: (_cond_of_row(i * COMBINE_TM), 0, 0)),
                  pl.BlockSpec(memory_space=pl.ANY)],
        out_specs=pl.BlockSpec((COMBINE_TM, D_MODEL), lambda i: (i, 0)),
        out_shape=jax.ShapeDtypeStruct((t, D_MODEL), jnp.float32),
        scratch_shapes=[pltpu.VMEM((2, TOP_K * COMBINE_TM, D_MODEL), jnp.float32),
                        pltpu.SemaphoreType.DMA((2,))],
        compiler_params=_cparams(1), name="moe_combine",
    )(pos3, pos3, h, gates_pad, mod, y_rows)


def _moe(h, hn, logits, w_gu, b_gu, w_dn, b_dn, mod):
    t = hn.shape[0]
    top_val, top_idx = lax.top_k(logits[:, :N_EXPERTS], TOP_K)
    gates = jax.nn.softmax(top_val, axis=-1)
    flat_e = top_idx.reshape(-1)
    order = jnp.argsort(flat_e)
    e_sorted = flat_e[order]
    counts = jnp.bincount(flat_e, length=N_EXPERTS)
    padded = (counts + MOE_TM - 1) // MOE_TM * MOE_TM
    start = jnp.cumsum(counts) - counts
    padded_end = jnp.cumsum(padded)
    padded_start = padded_end - padded
    dest = padded_start[e_sorted] + jnp.arange(t * TOP_K) - start[e_sorted]
    row_tok = jnp.zeros((MOE_ROWS,), jnp.int32).at[dest].set((order // TOP_K).astype(jnp.int32))
    pos = jnp.zeros((t * TOP_K,), jnp.int32).at[order].set(dest.astype(jnp.int32))
    block_e = jnp.minimum(
        jnp.searchsorted(padded_end, jnp.arange(MOE_BLOCKS) * MOE_TM, side='right'),
        N_EXPERTS - 1).astype(jnp.int32)
    n_valid = (padded_end[-1] // MOE_TM).astype(jnp.int32).reshape(1)

    x_rows = _moe_dispatch(hn, row_tok, n_valid)
    act = _grouped_matmul(x_rows, w_gu, b_gu, block_e, n_valid, swiglu=True, tn=1024,
                          out_dtype=jnp.bfloat16, name="moe_gate_up")
    y_rows = _grouped_matmul(act, w_dn, b_dn, block_e, n_valid, swiglu=False, tn=1024,
                             out_dtype=jnp.float32, name="moe_down")
    return _moe_combine(h, y_rows, pos.reshape(t, TOP_K), gates, mod, 5)


CONV_TM = 256
N_CHUNKS = T_ALL // SSD_CHUNK
CTX_CHUNKS = T_CTX // SSD_CHUNK
N_SEQS = BATCH + DEC_BATCH


def _seq_len_of_row(row):
    return jnp.where(row < T_CTX, SEQ, DEC_SEQ)


def _seq_pos_of_row(row):
    return jnp.where(row < T_CTX, row % SEQ, (row - T_CTX) % DEC_SEQ)


def _seq_of_chunk(n):
    return jnp.where(n < CTX_CHUNKS, n // (SEQ // SSD_CHUNK),
                     BATCH + (n - CTX_CHUNKS) // (DEC_SEQ // SSD_CHUNK))


def _dwconv_kernel(x_ref, prev_ref, next_ref, w_ref, b_ref, o_ref, *, width, silu):
    row0 = pl.program_id(1) * CONV_TM
    pos0 = _seq_pos_of_row(row0)
    has_prev = pos0 != 0
    has_next = pos0 + CONV_TM != _seq_len_of_row(row0)
    x = x_ref[...]
    rid = lax.broadcasted_iota(jnp.int32, x.shape, 0)
    prev = jnp.where(has_prev, prev_ref[7:8, :], 0.0)
    nxt0 = jnp.where(has_next, next_ref[0:1, :], 0.0)
    acc = w_ref[1:2, :] * x + b_ref[...]
    xm1 = jnp.where(rid == 0, prev, pltpu.roll(x, 1, 0))
    acc = acc + w_ref[0:1, :] * xm1
    xp1 = jnp.where(rid == CONV_TM - 1, nxt0, pltpu.roll(x, CONV_TM - 1, 0))
    acc = acc + w_ref[2:3, :] * xp1
    if width == 4:
        nxt1 = jnp.where(has_next, next_ref[1:2, :], 0.0)
        xp2 = pltpu.roll(x, CONV_TM - 2, 0)
        xp2 = jnp.where(rid == CONV_TM - 2, nxt0, jnp.where(rid == CONV_TM - 1, nxt1, xp2))
        acc = acc + w_ref[3:4, :] * xp2
    if silu:
        acc = acc * jax.nn.sigmoid(acc)
    o_ref[...] = acc


def _dwconv(x, w, b, *, in_col_blocks, silu, name):
    tc = 512
    width = w.shape[0]
    nblk = len(in_col_blocks)
    assert w.shape[1] == nblk * tc and width in (3, 4)
    cols = np.asarray(in_col_blocks, np.int32)
    if np.array_equal(cols, np.arange(nblk)):
        col = lambda j: j
    else:
        assert nblk == 3 and cols[0] == 0 and cols[1] == 1
        col = lambda j: jnp.where(j < 2, j, int(cols[2]))
    w8 = jnp.pad(w, ((0, 8 - width), (0, 0)))
    sub = CONV_TM // 8
    last8 = T_ALL // 8 - 1
    return pl.pallas_call(
        functools.partial(_dwconv_kernel, width=width, silu=silu),
        grid=(nblk, T_ALL // CONV_TM),
        in_specs=[pl.BlockSpec((CONV_TM, tc), lambda j, i: (i, col(j))),
                  pl.BlockSpec((8, tc), lambda j, i: (jnp.maximum(i * sub - 1, 0), col(j))),
                  pl.BlockSpec((8, tc), lambda j, i: (jnp.minimum((i + 1) * sub, last8), col(j))),
                  pl.BlockSpec((8, tc), lambda j, i: (0, j)),
                  pl.BlockSpec((1, tc), lambda j, i: (0, j))],
        out_specs=pl.BlockSpec((CONV_TM, tc), lambda j, i: (i, j)),
        out_shape=jax.ShapeDtypeStruct((T_ALL, nblk * tc), jnp.float32),
        compiler_params=_cparams(2), name=name,
    )(x, x, x, w8, b.reshape(1, -1))


def _softplus(x):
    return jnp.maximum(x, 0.0) + jnp.log(1.0 + jnp.exp(-jnp.abs(x)))


def _ssd_kernel(*refs, reverse):
    if reverse:
        (xbc_ref, zdt_ref, bias_ref, alog_ref, h0_ref, yf_ref, z_ref, dskip_ref, gn_ref,
         y_ref, st_ref, s_scr, y_scr) = refs
    else:
        xbc_ref, zdt_ref, bias_ref, alog_ref, h0_ref, y_ref, st_ref, s_scr = refs
    f32, bf = jnp.float32, jnp.bfloat16
    q = SSD_CHUNK
    step = pl.program_id(0)
    n = (N_CHUNKS - 1 - step) if reverse else step
    row0 = n * q
    pos0 = _seq_pos_of_row(row0)
    seq_len = _seq_len_of_row(row0)
    starts_seq = (pos0 + q == seq_len) if reverse else (pos0 == 0)
    ends_seq = (pos0 == 0) if reverse else (pos0 + q == seq_len)

    @pl.when(starts_seq)
    def _():
        s_scr[...] = jnp.where(row0 < T_CTX, 0.0, h0_ref[...])

    ri = lax.broadcasted_iota(jnp.int32, (q, q), 0)
    ci = lax.broadcasted_iota(jnp.int32, (q, q), 1)
    mask = (ci >= ri) if reverse else (ci <= ri)
    lane_lo = ci < SSD_HEAD_DIM
    row_lo = ri < SSD_HEAD_DIM

    dt = _softplus(zdt_ref[...] + bias_ref[...])
    da = dt * (-jnp.exp(alog_ref[...]))
    ones = jnp.where(mask, 1.0, 0.0).astype(bf)
    d1 = da.astype(bf)
    r1 = da - d1.astype(f32)
    d2 = r1.astype(bf)
    d3 = (r1 - d2.astype(f32)).astype(bf)
    cs = (jnp.dot(ones, d1, preferred_element_type=f32)
          + jnp.dot(ones, d2, preferred_element_type=f32)
          + jnp.dot(ones, d3, preferred_element_type=f32))
    cs_t = cs.T
    tot = cs[0:1, :] if reverse else cs[q - 1:q, :]
    exp_cs = jnp.exp(cs)
    d_end = jnp.exp(tot - cs)
    exp_tot = jnp.exp(tot)

    def pair_cols(v, p):
        return jnp.where(lane_lo, v[:, 2 * p:2 * p + 1], v[:, 2 * p + 1:2 * p + 2])

    dn_t = (((1,), (1,)), ((), ()))
    for g in range(SSD_GROUPS):
        b_g = xbc_ref[:, SSD_WIDTH + g * SSD_STATE:SSD_WIDTH + (g + 1) * SSD_STATE].astype(bf)
        c_off = SSD_WIDTH + SSD_GROUPS * SSD_STATE
        c_g = xbc_ref[:, c_off + g * SSD_STATE:c_off + (g + 1) * SSD_STATE].astype(bf)
        cb = lax.dot_general(c_g, b_g, dn_t, preferred_element_type=f32)
        pairs_per_group = SSD_HEADS // SSD_GROUPS // 2
        for p in range(g * pairs_per_group, (g + 1) * pairs_per_group):
            lo, hi = p * 2 * SSD_HEAD_DIM, (p + 1) * 2 * SSD_HEAD_DIM
            xs = xbc_ref[:, lo:hi]
            xdt = xs * pair_cols(dt, p)
            y = jnp.zeros((q, 2 * SSD_HEAD_DIM), f32)
            for h, keep in ((2 * p, lane_lo), (2 * p + 1, ~lane_lo)):
                seg = cs[:, h:h + 1] - cs_t[h:h + 1, :]
                w = jnp.where(mask, jnp.exp(jnp.minimum(seg, 0.0)), 0.0) * cb
                y = y + jnp.dot(w.astype(bf), jnp.where(keep, xdt, 0.0).astype(bf),
                                preferred_element_type=f32)
            s_pair = s_scr[lo:hi, :]
            y_off = lax.dot_general(c_g, s_pair.astype(bf), dn_t, preferred_element_type=f32)
            y = y + y_off * pair_cols(exp_cs, p)
            st = jnp.dot((xdt * pair_cols(d_end, p)).T.astype(bf), b_g,
                         preferred_element_type=f32)
            keep_s = jnp.where(row_lo, exp_tot[:, 2 * p:2 * p + 1], exp_tot[:, 2 * p + 1:2 * p + 2])
            s_scr[lo:hi, :] = s_pair * keep_s + st
            if reverse:
                y_scr[:, lo:hi] = y + yf_ref[:, lo:hi] + dskip_ref[:, lo:hi] * xs
            else:
                y_ref[:, lo:hi] = y

    if reverse:
        zz = z_ref[...]
        yy = y_scr[...] * (zz * jax.nn.sigmoid(zz))
        gw = SSD_WIDTH // SSD_GROUPS
        for g in range(SSD_GROUPS):
            y_g = yy[:, g * gw:(g + 1) * gw]
            y_g = y_g * lax.rsqrt(jnp.mean(y_g * y_g, axis=-1, keepdims=True) + NORM_EPS)
            y_ref[:, g * gw:(g + 1) * gw] = (y_g * gn_ref[:, g * gw:(g + 1) * gw]).astype(y_ref.dtype)

    @pl.when(ends_seq)
    def _():
        st_ref[...] = s_scr[...]


def _ssd(xbc, z_dt, dt_bias, a_log, h0, *, reverse, finish=None):
    q = SSD_CHUNK
    d = 1 if reverse else 0
    chunk = (lambda s: N_CHUNKS - 1 - s) if reverse else (lambda s: s)
    pad_row = lambda v: jnp.pad(v, (0, V7X_LANES - SSD_HEADS)).reshape(1, V7X_LANES)
    in_specs = [pl.BlockSpec((q, SSD_CONV_CH), lambda s: (chunk(s), 0)),
                pl.BlockSpec((q, V7X_LANES), lambda s: (chunk(s), d)),
                pl.BlockSpec((1, V7X_LANES), lambda s: (0, 0)),
                pl.BlockSpec((1, V7X_LANES), lambda s: (0, 0)),
                pl.BlockSpec((None, SSD_WIDTH, SSD_STATE),
                             lambda s: (jnp.maximum(_seq_of_chunk(chunk(s)) - BATCH, 0), 0, 0))]
    args = [xbc, z_dt, pad_row(dt_bias), pad_row(a_log), h0]
    scratch = [pltpu.VMEM((SSD_WIDTH, SSD_STATE), jnp.float32)]
    y_dtype = jnp.float32
    if reverse:
        y_fwd, z_ssd, dskip_row, gn_row = finish
        in_specs += [pl.BlockSpec((q, SSD_WIDTH), lambda s: (chunk(s), 0)),
                     pl.BlockSpec((q, SSD_WIDTH), lambda s: (chunk(s), 1)),
                     pl.BlockSpec((1, SSD_WIDTH), lambda s: (0, 0)),
                     pl.BlockSpec((1, SSD_WIDTH), lambda s: (0, 0))]
        args += [y_fwd, z_ssd, dskip_row, gn_row]
        scratch.append(pltpu.VMEM((q, SSD_WIDTH), jnp.float32))
        y_dtype = jnp.bfloat16
    return pl.pallas_call(
        functools.partial(_ssd_kernel, reverse=reverse),
        grid=(N_CHUNKS,),
        in_specs=in_specs,
        out_specs=[pl.BlockSpec((q, SSD_WIDTH), lambda s: (chunk(s), 0)),
                   pl.BlockSpec((None, SSD_WIDTH, SSD_STATE),
                                lambda s: (_seq_of_chunk(chunk(s)), 0, 0))],
        out_shape=[jax.ShapeDtypeStruct((T_ALL, SSD_WIDTH), y_dtype),
                   jax.ShapeDtypeStruct((N_SEQS, SSD_WIDTH, SSD_STATE), jnp.float32)],
        scratch_shapes=scratch,
        compiler_params=_cparams(1), name="ssd_bwd" if reverse else "ssd_fwd",
    )(*args)


ATT_TQ = 256
HEAD_W = 2 * DA_HEAD_DIM


def _rope_tables():
    t = jnp.arange(DEC_SEQ)
    pos = jnp.stack([t // GRID_W, t % GRID_W], axis=-1).astype(jnp.float32)
    inv = 1.0 / (ROPE_BASE ** (jnp.arange(ROPE_FREQS, dtype=jnp.float32) / ROPE_FREQS))
    ang = pos[:, :, None] * inv
    cos, sin = jnp.cos(ang), jnp.sin(ang)
    zero = jnp.zeros_like(sin)
    lanes = lambda first, second: jnp.tile(
        jnp.stack([first, second], axis=2).reshape(DEC_SEQ, DA_HEAD_DIM), (1, 2))
    return lanes(cos, cos), lanes(-sin, zero), lanes(zero, sin)


def _attn_kernel(lam_ref, *refs, has_ctx):
    if has_ctx:
        (q_ref, k_ref, v_ref, kc_ref, vc_ref, cq, saq, sbq, ck, sak, sbk, sw_ref, o_ref) = refs
    else:
        q_ref, k_ref, v_ref, sw_ref, o_ref = refs
    f32, bf = jnp.float32, jnp.bfloat16
    half = ROPE_FREQS

    def rope(x, c, sa, sb):
        return (x * c[...] + pltpu.roll(x, HEAD_W - half, 1) * sa[...]
                + pltpu.roll(x, half, 1) * sb[...])

    q, k = q_ref[...], k_ref[...]
    if has_ctx:
        q, k = rope(q, cq, saq, sbq), rope(k, ck, sak, sbk)
    lane = lax.broadcasted_iota(jnp.int32, q.shape, 1)
    kb, vb = k.astype(bf), v_ref[...].astype(bf)
    if has_ctx:
        kb = jnp.concatenate([kc_ref[...].astype(bf), kb], axis=0)
        vb = jnp.concatenate([vc_ref[...].astype(bf), vb], axis=0)
    scale = DA_HEAD_DIM ** -0.5

    def probs(qm):
        s = lax.dot_general(qm.astype(bf), kb, (((1,), (1,)), ((), ())),
                            preferred_element_type=f32) * scale
        e = jnp.exp(s - jnp.max(s, axis=-1, keepdims=True))
        return e / jnp.sum(e, axis=-1, keepdims=True)

    w = probs(jnp.where(lane < DA_HEAD_DIM, q, 0.0)) - lam_ref[0] * probs(
        jnp.where(lane >= DA_HEAD_DIM, q, 0.0))
    o = jnp.dot(w.astype(bf), vb, preferred_element_type=f32)
    o = o * lax.rsqrt(jnp.mean(o * o, axis=-1, keepdims=True) + NORM_EPS) * sw_ref[...]
    o_ref[...] = (o * (1.0 - DA_LAMBDA_INIT)).astype(o_ref.dtype)


def _diff_attention(z_qkv, cache_k, cache_v, lam, subln_w):
    sw = subln_w.reshape(1, HEAD_W)
    o_ctx = pl.pallas_call(
        functools.partial(_attn_kernel, has_ctx=False),
        grid_spec=pltpu.PrefetchScalarGridSpec(
            num_scalar_prefetch=1, grid=(BATCH, DA_HEADS),
            in_specs=[pl.BlockSpec((SEQ, HEAD_W), lambda b, h, lam: (b, h)),
                      pl.BlockSpec((SEQ, HEAD_W), lambda b, h, lam: (b, DA_HEADS + h)),
                      pl.BlockSpec((SEQ, HEAD_W), lambda b, h, lam: (b, 2 * DA_HEADS + h)),
                      pl.BlockSpec((1, HEAD_W), lambda b, h, lam: (0, 0))],
            out_specs=pl.BlockSpec((SEQ, HEAD_W), lambda b, h, lam: (b, h))),
        out_shape=jax.ShapeDtypeStruct((T_CTX, DA_WIDTH), jnp.bfloat16),
        compiler_params=_cparams(2), name="attn_ctx",
    )(lam, z_qkv, z_qkv, z_qkv, sw)

    tabs = _rope_tables()
    nq = DEC_SEQ // ATT_TQ
    q_blk0 = T_CTX // ATT_TQ
    kv_blk0 = T_CTX // DEC_SEQ
    kc = cache_k.reshape(DEC_BATCH * PAST_LEN, DA_WIDTH)
    vc = cache_v.reshape(DEC_BATCH * PAST_LEN, DA_WIDTH)
    q_row = lambda b, h, i, lam: (q_blk0 + b * nq + i, h)
    tab_q = pl.BlockSpec((ATT_TQ, HEAD_W), lambda b, h, i, lam: (i, 0))
    tab_k = pl.BlockSpec((DEC_SEQ, HEAD_W), lambda b, h, i, lam: (0, 0))
    o_dec = pl.pallas_call(
        functools.partial(_attn_kernel, has_ctx=True),
        grid_spec=pltpu.PrefetchScalarGridSpec(
            num_scalar_prefetch=1, grid=(DEC_BATCH, DA_HEADS, nq),
            in_specs=[pl.BlockSpec((ATT_TQ, HEAD_W), q_row),
                      pl.BlockSpec((DEC_SEQ, HEAD_W), lambda b, h, i, lam: (kv_blk0 + b, DA_HEADS + h)),
                      pl.BlockSpec((DEC_SEQ, HEAD_W), lambda b, h, i, lam: (kv_blk0 + b, 2 * DA_HEADS + h)),
                      pl.BlockSpec((PAST_LEN, HEAD_W), lambda b, h, i, lam: (b, h)),
                      pl.BlockSpec((PAST_LEN, HEAD_W), lambda b, h, i, lam: (b, h)),
                      tab_q, tab_q, tab_q, tab_k, tab_k, tab_k,
                      pl.BlockSpec((1, HEAD_W), lambda b, h, i, lam: (0, 0))],
            out_specs=pl.BlockSpec((ATT_TQ, HEAD_W), lambda b, h, i, lam: (b * nq + i, h))),
        out_shape=jax.ShapeDtypeStruct((T_DEC, DA_WIDTH), jnp.bfloat16),
        compiler_params=_cparams(3), name="attn_dec",
    )(lam, z_qkv, z_qkv, z_qkv, kc, vc, tabs[0], tabs[1], tabs[2], tabs[0], tabs[1], tabs[2], sw)
    return jnp.concatenate([o_ctx, o_dec], axis=0)


def _dft_angle(L, s_offset):
    k = jnp.arange(L, dtype=jnp.int32)[:, None]
    s = jnp.arange(L, dtype=jnp.int32)[None, :] + s_offset
    m = ((2 * k + 1) * s) % (4 * L)
    return m.astype(jnp.float32) * (math.pi / (2 * L))


def _hyena_tables(L):
    bf = jnp.bfloat16
    th = _dft_angle(L, 0)
    c, s = jnp.cos(th), jnp.sin(th)
    fwd = jnp.concatenate([c, -s], axis=0).astype(bf)
    inv = (jnp.concatenate([c.T, -s.T], axis=1) * (1.0 / L)).astype(bf)
    th_f = _dft_angle(L, -(L // 2))
    filt_fwd = jnp.concatenate([jnp.cos(th_f), -jnp.sin(th_f)], axis=0).astype(bf)
    return fwd, inv, filt_fwd


def _hyena_filter_kernel(band_ref, w1t_ref, w1c_ref, w1s_ref, b1_ref, w2_ref, b2_ref, w3_ref,
                         b3_ref, fr_ref, w4_ref, dl_ref, o_ref, h_scr, *, L):
    t = lax.broadcasted_iota(jnp.int32, (L, 1), 0).astype(jnp.float32)

    @pl.when(pl.program_id(0) == 0)
    def _():
        ang = ((2.0 * math.pi / L) * t) * band_ref[...]
        fr = fr_ref[...]
        z = ((t / L) * w1t_ref[...] + _dot3(jnp.cos(ang), w1c_ref[...])
             + _dot3(jnp.sin(ang), w1s_ref[...]))
        h = jnp.sin(fr * (z + b1_ref[...]))
        h = jnp.sin(fr * (_dot3(h, w2_ref[...]) + b2_ref[...]))
        h_scr[...] = jnp.sin(fr * (_dot3(h, w3_ref[...]) + b3_ref[...]))

    dist = jnp.abs(t - (L // 2)) / L
    o_ref[...] = _dot3(h_scr[...], w4_ref[...]) * jnp.exp(-dist * dl_ref[...])


def _hyena_filter(L, w1, b1, w2, b2, w3, b3, w4, freq):
    hid = w2.shape[0]
    tn = 512
    bands = jnp.linspace(1e-4, HY_BANDS - 1, HY_BANDS, dtype=jnp.float32)
    lane_pad = V7X_LANES - HY_BANDS
    band_row = jnp.pad(bands, (0, lane_pad)).reshape(1, V7X_LANES)
    pad_rows = lambda w: jnp.pad(w, ((0, lane_pad), (0, 0)))
    deltas = jnp.abs(jnp.linspace(math.log(HY_TARGET) / HY_SLOW_DECAY,
                                  math.log(HY_TARGET) / HY_FAST_DECAY, HY_WIDTH,
                                  dtype=jnp.float32)).reshape(1, HY_WIDTH)
    full = lambda shape: pl.BlockSpec(shape, lambda j: (0,) * len(shape))
    return pl.pallas_call(
        functools.partial(_hyena_filter_kernel, L=L),
        grid=(HY_WIDTH // tn,),
        in_specs=[full((1, V7X_LANES)), full((1, hid)), full((V7X_LANES, hid)),
                  full((V7X_LANES, hid)), full((1, hid)), full((hid, hid)), full((1, hid)),
                  full((hid, hid)), full((1, hid)), full((1, hid)),
                  pl.BlockSpec((hid, tn), lambda j: (0, j)),
                  pl.BlockSpec((1, tn), lambda j: (0, j))],
        out_specs=pl.BlockSpec((L, tn), lambda j: (0, j)),
        out_shape=jax.ShapeDtypeStruct((L, HY_WIDTH), jnp.float32),
        scratch_shapes=[pltpu.VMEM((L, hid), jnp.float32)],
        compiler_params=_cparams(1), name=f"hyena_filter_{L}",
    )(band_row, w1[0:1], pad_rows(w1[1:1 + HY_BANDS]), pad_rows(w1[1 + HY_BANDS:]),
      b1.reshape(1, hid), w2, b2.reshape(1, hid), w3, b3.reshape(1, hid), freq.reshape(1, hid),
      w4, deltas)


def _hyena_conv_kernel(*refs):
    (x0_ref, x1_ref, v_ref, fr_ref, fi_ref, hr_ref, hi_ref, ir_ref, ii_ref, fb_ref) = refs[:10]
    o_ref, vbf, acc = refs[-3:]
    f32, bf = jnp.float32, jnp.bfloat16
    kb = pl.program_id(2)

    @pl.when(kb == 0)
    def _():
        vbf[...] = (v_ref[...] * x1_ref[...]).astype(bf)
        acc[...] = jnp.zeros_like(acc)

    vr = jnp.dot(fr_ref[...], vbf[...], preferred_element_type=f32)
    vi = jnp.dot(fi_ref[...], vbf[...], preferred_element_type=f32)
    hr, hi = hr_ref[...], hi_ref[...]
    yr = (vr * hr - vi * hi).astype(bf)
    yi = (vr * hi + vi * hr).astype(bf)
    acc[...] += (jnp.dot(ir_ref[...], yr, preferred_element_type=f32)
                 + jnp.dot(ii_ref[...], yi, preferred_element_type=f32))

    @pl.when(kb == pl.num_programs(2) - 1)
    def _():
        vv = v_ref[...] * x1_ref[...]
        o_ref[...] = ((acc[...] + vv * fb_ref[...]) * x0_ref[...]).astype(o_ref.dtype)


def _hyena_conv(u, filt_spec, tables, f_bias, *, L, row_blk0, n_seq, tn, kblk):
    fwd, inv, _ = tables
    nj, nk = HY_WIDTH // tn, L // kblk
    u_spec = lambda part: pl.BlockSpec((L, tn), lambda b, j, k: (row_blk0 + b, part * nj + j))
    in_specs = [u_spec(0), u_spec(1), u_spec(2),
                pl.BlockSpec((kblk, L), lambda b, j, k: (k, 0)),
                pl.BlockSpec((kblk, L), lambda b, j, k: (nk + k, 0)),
                pl.BlockSpec((kblk, tn), lambda b, j, k: (k, j)),
                pl.BlockSpec((kblk, tn), lambda b, j, k: (nk + k, j)),
                pl.BlockSpec((L, kblk), lambda b, j, k: (0, k)),
                pl.BlockSpec((L, kblk), lambda b, j, k: (0, nk + k)),
                pl.BlockSpec((1, tn), lambda b, j, k: (0, j))]
    args = [u, u, u, fwd, fwd, filt_spec, filt_spec, inv, inv, f_bias.reshape(1, HY_WIDTH)]
    return pl.pallas_call(
        _hyena_conv_kernel,
        grid=(n_seq, nj, nk),
        in_specs=in_specs,
        out_specs=pl.BlockSpec((L, tn), lambda b, j, k: (b, j)),
        out_shape=jax.ShapeDtypeStruct((n_seq * L, HY_WIDTH), jnp.bfloat16),
        scratch_shapes=[pltpu.VMEM((L, tn), jnp.bfloat16), pltpu.VMEM((L, tn), jnp.float32)],
        compiler_params=_cparams(3), name=f"hyena_conv_{L}",
    )(*args)


def _hyena_mix(u, filt_p, f_bias):
    outs = []
    for L, row_blk0, n_seq, tn, kblk in ((SEQ, 0, BATCH, 1024, SEQ),
                                        (DEC_SEQ, T_CTX // DEC_SEQ, DEC_BATCH, 256, 512)):
        tables = _hyena_tables(L)
        filt = _hyena_filter(L, *filt_p)
        spec = _matmul(tables[2], filt, HY_WIDTH, tm=min(1024, 2 * L), name=f"hyena_spec_{L}")
        outs.append(_hyena_conv(u, spec, tables, f_bias, L=L, row_blk0=row_blk0, n_seq=n_seq,
                                tn=tn, kblk=kblk))
    return jnp.concatenate(outs, axis=0)


def kernel(x_prompt, x_sample, c, state_l0_ssm_fwd, state_l0_ssm_bwd, cache_l0_k, cache_l0_v, c_ctx, l0_norm_mix, l0_norm_ffn, l0_ada_w, l0_ada_b, l0_w_in, l0_conv_w, l0_conv_b, l0_a_log_fwd, l0_a_log_bwd, l0_dt_bias_fwd, l0_dt_bias_bwd, l0_d_skip, l0_gnorm_w, l0_lambda_q1, l0_lambda_k1, l0_lambda_q2, l0_lambda_k2, l0_subln_w, l0_w_out, l0_router_w, l0_router_b, l0_w_gate_up, l0_b_gate_up, l0_w_down, l0_b_down, l1_norm_mix, l1_norm_ffn, l1_ada_w, l1_ada_b, l1_w_in, l1_b_in, l1_short_w, l1_short_b, l1_filt_w1, l1_filt_b1, l1_filt_w2, l1_filt_b2, l1_filt_w3, l1_filt_b3, l1_filt_w4, l1_filt_freq, l1_filt_bias, l1_w_out, l1_b_out, l1_router_w, l1_router_b, l1_w_gate_up, l1_b_gate_up, l1_w_down, l1_b_down, final_norm):
    h = jnp.concatenate([x_prompt.reshape(T_CTX, D_MODEL), x_sample.reshape(T_DEC, D_MODEL)], axis=0)
    cond8 = jnp.concatenate([c_ctx[None, :], c,
                             jnp.zeros((MOD_ROWS - N_COND, D_MODEL), jnp.float32)], axis=0)

    def router_args(rw, rb):
        pad = V7X_LANES - N_EXPERTS
        return (jnp.pad(rw, ((0, 0), (0, pad))),
                jnp.pad(rb, (0, pad), constant_values=-1e30).reshape(1, V7X_LANES))

    def ffn(h, mod, norm_g, rw, rb, w_gu, b_gu, w_dn, b_dn):
        hn, logits = _normmod(h, norm_g, mod, 3, router=router_args(rw, rb))
        return _moe(h, hn, logits, w_gu, b_gu, w_dn, b_dn, mod)

    mod0 = _adaln(cond8, l0_ada_w, l0_ada_b)
    hn = _normmod(h, l0_norm_mix, mod0, 0)
    c_dt = sum(L0_SPLITS[:4])
    c_q = c_dt + 2 * SSD_HEADS
    z_ssd = _matmul(hn, l0_w_in, c_dt, name="l0_in_ssd")
    lane_pad = jnp.zeros((D_MODEL, V7X_LANES - SSD_HEADS), jnp.float32)
    w_dt = jnp.concatenate([l0_w_in[:, c_dt:c_dt + SSD_HEADS], lane_pad,
                            l0_w_in[:, c_dt + SSD_HEADS:c_q], lane_pad], axis=1)
    z_dt = _matmul(hn, w_dt, 2 * V7X_LANES, tn=2 * V7X_LANES, name="l0_in_dt")
    z_qkv = _matmul(hn, l0_w_in[:, c_q:], 3 * DA_WIDTH, name="l0_in_qkv")
    xbc = _dwconv(z_ssd, l0_conv_w, l0_conv_b, in_col_blocks=(0, 1, 4), silu=True, name="ssd_conv")
    h0_shape = (DEC_BATCH, SSD_WIDTH, SSD_STATE)
    y_fwd, st_fwd = _ssd(xbc, z_dt, l0_dt_bias_fwd, l0_a_log_fwd,
                         state_l0_ssm_fwd.reshape(h0_shape), reverse=False)
    y_ssd, st_bwd = _ssd(xbc, z_dt, l0_dt_bias_bwd, l0_a_log_bwd,
                         state_l0_ssm_bwd.reshape(h0_shape), reverse=True,
                         finish=(y_fwd, z_ssd, jnp.repeat(l0_d_skip, SSD_HEAD_DIM).reshape(1, -1),
                                 l0_gnorm_w.reshape(1, -1)))
    lam = (jnp.exp(jnp.sum(l0_lambda_q1 * l0_lambda_k1))
           - jnp.exp(jnp.sum(l0_lambda_q2 * l0_lambda_k2)) + DA_LAMBDA_INIT).reshape(1)
    o_att = _diff_attention(z_qkv, cache_l0_k, cache_l0_v, lam, l0_subln_w)
    h = _matmul(y_ssd, l0_w_out, D_MODEL, a2=o_att, gated=(h, mod0, 2), name="l0_out")
    h = ffn(h, mod0, l0_norm_ffn, l0_router_w, l0_router_b, l0_w_gate_up, l0_b_gate_up,
            l0_w_down, l0_b_down)
    st_shape = (BATCH, SSD_HEADS, SSD_HEAD_DIM, SSD_STATE)
    kv_shape = (BATCH, SEQ, DA_HEADS, HEAD_W)
    produced = (st_fwd[:BATCH].reshape(st_shape), st_bwd[:BATCH].reshape(st_shape),
                z_qkv[:T_CTX, DA_WIDTH:2 * DA_WIDTH].reshape(kv_shape),
                z_qkv[:T_CTX, 2 * DA_WIDTH:].reshape(kv_shape))

    mod1 = _adaln(cond8, l1_ada_w, l1_ada_b)
    hn = _normmod(h, l1_norm_mix, mod1, 0)
    u = _matmul(hn, l1_w_in, 3 * HY_WIDTH, bias=l1_b_in, name="l1_in")
    u = _dwconv(u, l1_short_w, l1_short_b, in_col_blocks=tuple(range(3 * HY_WIDTH // 512)),
                silu=False, name="hyena_conv")
    filt_p = (l1_filt_w1, l1_filt_b1, l1_filt_w2, l1_filt_b2, l1_filt_w3, l1_filt_b3, l1_filt_w4,
              l1_filt_freq)
    y = _hyena_mix(u, filt_p, l1_filt_bias)
    h = _matmul(y, l1_w_out, D_MODEL, bias=l1_b_out, gated=(h, mod1, 2), name="l1_out")
    h = ffn(h, mod1, l1_norm_ffn, l1_router_w, l1_router_b, l1_w_gate_up, l1_b_gate_up,
            l1_w_down, l1_b_down)

    y_prompt = _final_norm(h, final_norm, 0, T_CTX).reshape(BATCH, SEQ, D_MODEL)
    y_sample = _final_norm(h, final_norm, T_CTX, T_DEC).reshape(DEC_BATCH, DEC_SEQ, D_MODEL)
    return (y_prompt, y_sample) + produced
```

```python
import functools
import math

import numpy as np
import jax
import jax.numpy as jnp
from jax import lax
from jax.experimental import pallas as pl
from jax.experimental.pallas import tpu as pltpu

D_MODEL = 2048
BATCH = 32
SEQ = 256
DEC_BATCH = 2
DEC_SEQ = 2048
PAST_LEN = 512
GRID_W = 64
NORM_EPS = 1e-6
SSD_WIDTH = D_MODEL // 2
SSD_HEAD_DIM = 64
SSD_HEADS = SSD_WIDTH // SSD_HEAD_DIM
SSD_GROUPS = 2
SSD_STATE = 128
SSD_CHUNK = 128
SSD_CONV_CH = SSD_WIDTH + 2 * SSD_GROUPS * SSD_STATE
DA_WIDTH = D_MODEL // 2
DA_HEAD_DIM = 64
DA_HEADS = DA_WIDTH // (2 * DA_HEAD_DIM)
DA_LAMBDA_INIT = 0.8 - 0.6 * math.exp(-0.3 * 0)
ROPE_BASE = 10000.0
ROPE_FREQS = DA_HEAD_DIM // 4
L0_SPLITS = (SSD_WIDTH, SSD_WIDTH, SSD_GROUPS * SSD_STATE, SSD_GROUPS * SSD_STATE,
             SSD_HEADS, SSD_HEADS, DA_WIDTH, DA_WIDTH, DA_WIDTH)
HY_WIDTH = D_MODEL
HY_BANDS = 16
HY_FAST_DECAY = 0.3
HY_SLOW_DECAY = 1.5
HY_TARGET = 1e-2
N_EXPERTS = 32
TOP_K = 4
SWIGLU_LIMIT = 7.0
SWIGLU_ALPHA = 1.702

T_CTX = BATCH * SEQ
T_DEC = DEC_BATCH * DEC_SEQ
T_ALL = T_CTX + T_DEC
N_COND = 1 + DEC_BATCH
MOD_ROWS = 8

V7X_LANES = 128
V7X_MXU_DIM = 256
V7X_VMEM_BYTES = 64 * 1024 * 1024
VMEM_LIMIT = V7X_VMEM_BYTES - 8 * 1024 * 1024

MOE_TM = 512
MOE_TN = 2048
MOE_ROWS = T_ALL * TOP_K + N_EXPERTS * MOE_TM
MOE_BLOCKS = MOE_ROWS // MOE_TM


def _cparams(n_grid):
    return pltpu.CompilerParams(dimension_semantics=("arbitrary",) * n_grid,
                                vmem_limit_bytes=VMEM_LIMIT)


def _cond_of_row(row):
    return jnp.where(row < T_CTX, 0, (row - T_CTX) // DEC_SEQ + 1)


def _split_bf16(x):
    hi = x.astype(jnp.bfloat16)
    lo = (x - hi.astype(jnp.float32)).astype(jnp.bfloat16)
    return hi, lo


def _dot3(a, w):
    a_hi, a_lo = _split_bf16(a)
    w_hi, w_lo = _split_bf16(w)
    f32 = jnp.float32
    return (jnp.dot(a_hi, w_hi, preferred_element_type=f32)
            + jnp.dot(a_lo, w_hi, preferred_element_type=f32)
            + jnp.dot(a_hi, w_lo, preferred_element_type=f32))


def _adaln_kernel(c_ref, w_ref, b_ref, o_ref):
    c = c_ref[...]
    a = c * jax.nn.sigmoid(c)
    o_ref[...] = _dot3(a, w_ref[...]) + b_ref[...]


def _adaln(cond8, w, b):
    n = w.shape[1]
    tn = 512
    out = pl.pallas_call(
        _adaln_kernel,
        grid=(n // tn,),
        in_specs=[pl.BlockSpec((MOD_ROWS, D_MODEL), lambda j: (0, 0)),
                  pl.BlockSpec((D_MODEL, tn), lambda j: (0, j)),
                  pl.BlockSpec((1, tn), lambda j: (0, j))],
        out_specs=pl.BlockSpec((MOD_ROWS, tn), lambda j: (0, j)),
        out_shape=jax.ShapeDtypeStruct((MOD_ROWS, n), jnp.float32),
        compiler_params=_cparams(1),
        name="adaln",
    )(cond8, w, b.reshape(1, n))
    m = out[:N_COND].reshape(N_COND, 6, D_MODEL)
    return jnp.pad(m, ((0, 0), (0, MOD_ROWS - 6), (0, 0)))


def _norm_rows(x, g):
    var = jnp.mean(x * x, axis=-1, keepdims=True)
    return x * lax.rsqrt(var + NORM_EPS) * g


def _normmod_kernel(h_ref, g_ref, mod_ref, o_ref, *, sh_row):
    y = _norm_rows(h_ref[...], g_ref[...])
    y = y * (1.0 + mod_ref[sh_row + 1:sh_row + 2, :]) + mod_ref[sh_row:sh_row + 1, :]
    o_ref[...] = y.astype(o_ref.dtype)


def _normmod_router_kernel(h_ref, g_ref, mod_ref, rw_ref, rb_ref, o_ref, lg_ref, *, sh_row):
    y = _norm_rows(h_ref[...], g_ref[...])
    y = y * (1.0 + mod_ref[sh_row + 1:sh_row + 2, :]) + mod_ref[sh_row:sh_row + 1, :]
    o_ref[...] = y.astype(o_ref.dtype)
    lg_ref[...] = _dot3(y, rw_ref[...]) + rb_ref[...]


def _plain_norm_kernel(h_ref, g_ref, o_ref):
    o_ref[...] = _norm_rows(h_ref[...], g_ref[...])


def _normmod(h, g, mod, sh_row, router=None):
    tm = 256
    grid = (T_ALL // tm,)
    h_spec = pl.BlockSpec((tm, D_MODEL), lambda i: (i, 0))
    g_spec = pl.BlockSpec((1, D_MODEL), lambda i: (0, 0))
    mod_spec = pl.BlockSpec((None, MOD_ROWS, D_MODEL), lambda i: (_cond_of_row(i * tm), 0, 0))
    o_spec = pl.BlockSpec((tm, D_MODEL), lambda i: (i, 0))
    o_shape = jax.ShapeDtypeStruct((T_ALL, D_MODEL), jnp.bfloat16)
    if router is None:
        return pl.pallas_call(
            functools.partial(_normmod_kernel, sh_row=sh_row),
            grid=grid, in_specs=[h_spec, g_spec, mod_spec], out_specs=o_spec,
            out_shape=o_shape, compiler_params=_cparams(1), name="normmod",
        )(h, g.reshape(1, D_MODEL), mod)
    rw, rb = router
    return pl.pallas_call(
        functools.partial(_normmod_router_kernel, sh_row=sh_row),
        grid=grid,
        in_specs=[h_spec, g_spec, mod_spec,
                  pl.BlockSpec((D_MODEL, V7X_LANES), lambda i: (0, 0)),
                  pl.BlockSpec((1, V7X_LANES), lambda i: (0, 0))],
        out_specs=[o_spec, pl.BlockSpec((tm, V7X_LANES), lambda i: (i, 0))],
        out_shape=[jax.ShapeDtypeStruct((T_ALL, D_MODEL), jnp.float32),
                   jax.ShapeDtypeStruct((T_ALL, V7X_LANES), jnp.float32)],
        compiler_params=_cparams(1), name="normmod_router",
    )(h, g.reshape(1, D_MODEL), mod, rw, rb)


def _final_norm(h, g, row0, n_rows):
    tm = 256
    blk0 = row0 // tm
    return pl.pallas_call(
        _plain_norm_kernel,
        grid=(n_rows // tm,),
        in_specs=[pl.BlockSpec((tm, D_MODEL), lambda i: (i + blk0, 0)),
                  pl.BlockSpec((1, D_MODEL), lambda i: (0, 0))],
        out_specs=pl.BlockSpec((tm, D_MODEL), lambda i: (i, 0)),
        out_shape=jax.ShapeDtypeStruct((n_rows, D_MODEL), jnp.float32),
        compiler_params=_cparams(1), name="final_norm",
    )(h, g.reshape(1, D_MODEL))


def _mm_kernel(*refs, two_a, has_bias, gated, g_row):
    it = iter(refs)
    a_ref = next(it)
    a2_ref = next(it) if two_a else None
    w_ref = next(it)
    b_ref = next(it) if has_bias else None
    res_ref, mod_ref = (next(it), next(it)) if gated else (None, None)
    o_ref, wbf_ref = next(it), next(it)

    @pl.when(pl.program_id(1) == 0)
    def _():
        wbf_ref[...] = w_ref[...].astype(jnp.bfloat16)

    k1 = a_ref.shape[1]
    acc = jnp.dot(a_ref[...], wbf_ref[:k1, :], preferred_element_type=jnp.float32)
    if two_a:
        acc = acc + jnp.dot(a2_ref[...], wbf_ref[k1:, :], preferred_element_type=jnp.float32)
    if has_bias:
        acc = acc + b_ref[...]
    if gated:
        acc = res_ref[...] + mod_ref[g_row:g_row + 1, :] * acc
    o_ref[...] = acc.astype(o_ref.dtype)


def _matmul(a, w, n_cols, *, a2=None, col_blk0=0, bias=None, gated=None,
            out_dtype=jnp.float32, tm=1024, tn=1024, name="matmul"):
    m, k1 = a.shape
    k = w.shape[0]
    assert m % tm == 0 and n_cols % tn == 0
    in_specs = [pl.BlockSpec((tm, k1), lambda j, i: (i, 0))]
    args = [a]
    if a2 is not None:
        assert k1 + a2.shape[1] == k
        in_specs.append(pl.BlockSpec((tm, k - k1), lambda j, i: (i, 0)))
        args.append(a2)
    in_specs.append(pl.BlockSpec((k, tn), lambda j, i: (0, j + col_blk0)))
    args.append(w)
    if bias is not None:
        in_specs.append(pl.BlockSpec((1, tn), lambda j, i: (0, j + col_blk0)))
        args.append(bias.reshape(1, -1))
    g_row = 0
    if gated is not None:
        res, mod, g_row = gated
        in_specs.append(pl.BlockSpec((tm, tn), lambda j, i: (i, j)))
        in_specs.append(pl.BlockSpec((None, MOD_ROWS, tn),
                                     lambda j, i: (_cond_of_row(i * tm), 0, j)))
        args += [res, mod]
    return pl.pallas_call(
        functools.partial(_mm_kernel, two_a=a2 is not None, has_bias=bias is not None,
                          gated=gated is not None, g_row=g_row),
        grid=(n_cols // tn, m // tm),
        in_specs=in_specs,
        out_specs=pl.BlockSpec((tm, tn), lambda j, i: (i, j)),
        out_shape=jax.ShapeDtypeStruct((m, n_cols), out_dtype),
        scratch_shapes=[pltpu.VMEM((k, tn), jnp.bfloat16)],
        compiler_params=_cparams(2), name=name,
    )(*args)


def _gmm_kernel(be_ref, nv_ref, x_ref, w_ref, b_ref, *rest, swiglu):
    if swiglu:
        sel_ref, o_ref, wbf_ref = rest
    else:
        o_ref, wbf_ref = rest
    i = pl.program_id(1)
    prev = be_ref[jnp.maximum(i - 1, 0)]

    @pl.when((i == 0) | (be_ref[i] != prev))
    def _():
        wbf_ref[...] = w_ref[...].astype(jnp.bfloat16)

    @pl.when(i < nv_ref[0])
    def _():
        f32 = jnp.float32
        acc = jnp.dot(x_ref[...], wbf_ref[...], preferred_element_type=f32) + b_ref[...]
        if not swiglu:
            o_ref[...] = acc.astype(o_ref.dtype)
            return
        tn = acc.shape[1]
        wide, half = V7X_MXU_DIM, V7X_MXU_DIM // 2
        gate = jnp.minimum(acc, SWIGLU_LIMIT)
        up = pltpu.roll(jnp.clip(acc, -SWIGLU_LIMIT, SWIGLU_LIMIT), tn - 1, 1)
        act = (up + 1.0) * (gate * jax.nn.sigmoid(SWIGLU_ALPHA * gate))
        lane = lax.broadcasted_iota(jnp.int32, act.shape, 1)
        act = jnp.where(lane % 2 == 0, act, 0.0).astype(jnp.bfloat16)
        for g in range(tn // wide):
            o_ref[:, g * half:(g + 1) * half] = jnp.dot(
                act[:, g * wide:(g + 1) * wide], sel_ref[...],
                preferred_element_type=f32).astype(o_ref.dtype)

    @pl.when(i >= nv_ref[0])
    def _():
        o_ref[...] = jnp.zeros_like(o_ref)


def _grouped_matmul(x, w, b, block_e, n_valid, *, swiglu, tn, out_dtype, name):
    rows, k = x.shape
    n = w.shape[2]
    tn_out = tn // 2 if swiglu else tn
    n_out = n // 2 if swiglu else n
    in_specs = [pl.BlockSpec((MOE_TM, k), lambda j, i, be, nv: (i, 0)),
                pl.BlockSpec((None, k, tn), lambda j, i, be, nv: (be[i], 0, j)),
                pl.BlockSpec((None, 1, tn), lambda j, i, be, nv: (be[i], 0, j))]
    args = [x, w, b.reshape(N_EXPERTS, 1, n)]
    if swiglu:
        sel = np.zeros((V7X_MXU_DIM, V7X_MXU_DIM // 2), np.float32)
        sel[2 * np.arange(V7X_MXU_DIM // 2), np.arange(V7X_MXU_DIM // 2)] = 1.0
        in_specs.append(pl.BlockSpec(sel.shape, lambda j, i, be, nv: (0, 0)))
        args.append(jnp.asarray(sel, jnp.bfloat16))
    return pl.pallas_call(
        functools.partial(_gmm_kernel, swiglu=swiglu),
        grid_spec=pltpu.PrefetchScalarGridSpec(
            num_scalar_prefetch=2,
            grid=(n // tn, rows // MOE_TM),
            in_specs=in_specs,
            out_specs=pl.BlockSpec((MOE_TM, tn_out), lambda j, i, be, nv: (i, j)),
            scratch_shapes=[pltpu.VMEM((k, tn), jnp.bfloat16)]),
        out_shape=jax.ShapeDtypeStruct((rows, n_out), out_dtype),
        compiler_params=_cparams(2), name=name,
    )(block_e, n_valid, *args)


def _dispatch_kernel(nv_ref, tok_ref, tok_next_ref, hn_hbm, o_ref, buf, sem):
    i = pl.program_id(0)
    slot = i % 2

    def issue(idx_ref, s):
        def body(r, carry):
            pltpu.make_async_copy(hn_hbm.at[pl.ds(idx_ref[0, r], 1)],
                                  buf.at[s, pl.ds(r, 1)], sem.at[s]).start()
            return carry
        lax.fori_loop(0, MOE_TM, body, 0, unroll=8)

    @pl.when((i == 0) & (nv_ref[0] > 0))
    def _():
        issue(tok_ref, 0)

    @pl.when(i + 1 < nv_ref[0])
    def _():
        issue(tok_next_ref, 1 - slot)

    @pl.when(i < nv_ref[0])
    def _():
        pltpu.make_async_copy(hn_hbm.at[pl.ds(0, MOE_TM)], buf.at[slot], sem.at[slot]).wait()
        o_ref[...] = buf[slot].astype(o_ref.dtype)

    @pl.when(i >= nv_ref[0])
    def _():
        o_ref[...] = jnp.zeros_like(o_ref)


def _moe_dispatch(hn, row_tok, n_valid):
    tok3 = row_tok.reshape(MOE_BLOCKS, 1, MOE_TM)
    smem_blk = lambda f: pl.BlockSpec((None, 1, MOE_TM), f, memory_space=pltpu.SMEM)
    return pl.pallas_call(
        _dispatch_kernel,
        grid_spec=pltpu.PrefetchScalarGridSpec(
            num_scalar_prefetch=1,
            grid=(MOE_BLOCKS,),
            in_specs=[smem_blk(lambda i, nv: (i, 0, 0)),
                      smem_blk(lambda i, nv: (jnp.minimum(i + 1, MOE_BLOCKS - 1), 0, 0)),
                      pl.BlockSpec(memory_space=pl.ANY)],
            out_specs=pl.BlockSpec((MOE_TM, D_MODEL), lambda i, nv: (i, 0)),
            scratch_shapes=[pltpu.VMEM((2, MOE_TM, D_MODEL), jnp.float32),
                            pltpu.SemaphoreType.DMA((2,))]),
        out_shape=jax.ShapeDtypeStruct((MOE_ROWS, D_MODEL), jnp.bfloat16),
        compiler_params=_cparams(1), name="moe_dispatch",
    )(n_valid, tok3, tok3, hn)


COMBINE_TM = 128


def _combine_kernel(pos_ref, pos_next_ref, h_ref, gates_ref, mod_ref, y_hbm, o_ref, buf, sem,
                    *, g_row):
    i = pl.program_id(0)
    slot = i % 2
    n_rows = TOP_K * COMBINE_TM

    def issue(idx_ref, s):
        def body(r, carry):
            pltpu.make_async_copy(y_hbm.at[pl.ds(idx_ref[0, r], 1)],
                                  buf.at[s, pl.ds(r, 1)], sem.at[s]).start()
            return carry
        lax.fori_loop(0, n_rows, body, 0, unroll=8)

    @pl.when(i == 0)
    def _():
        issue(pos_ref, 0)

    @pl.when(i + 1 < pl.num_programs(0))
    def _():
        issue(pos_next_ref, 1 - slot)

    pltpu.make_async_copy(y_hbm.at[pl.ds(0, n_rows)], buf.at[slot], sem.at[slot]).wait()
    acc = gates_ref[:, 0:1] * buf[slot, 0:COMBINE_TM, :]
    for k in range(1, TOP_K):
        acc = acc + gates_ref[:, k:k + 1] * buf[slot, k * COMBINE_TM:(k + 1) * COMBINE_TM, :]
    o_ref[...] = h_ref[...] + mod_ref[g_row:g_row + 1, :] * acc


def _moe_combine(h, y_rows, pos, gates, mod, g_row):
    t = h.shape[0]
    nblk = t // COMBINE_TM
    pos3 = pos.reshape(nblk, COMBINE_TM, TOP_K).transpose(0, 2, 1).reshape(nblk, 1, TOP_K * COMBINE_TM)
    gates_pad = jnp.pad(gates, ((0, 0), (0, V7X_LANES - TOP_K)))
    smem_blk = lambda f: pl.BlockSpec((None, 1, TOP_K * COMBINE_TM), f, memory_space=pltpu.SMEM)
    return pl.pallas_call(
        functools.partial(_combine_kernel, g_row=g_row),
        grid=(nblk,),
        in_specs=[smem_blk(lambda i: (i, 0, 0)),
                  smem_blk(lambda i: (jnp.minimum(i + 1, nblk - 1), 0, 0)),
                  pl.BlockSpec((COMBINE_TM, D_MODEL), lambda i: (i, 0)),
                  pl.BlockSpec((COMBINE_TM, V7X_LANES), lambda i: (i, 0)),
                  pl.BlockSpec((None, MOD_ROWS, D_MODEL),
                               lambda i: (_cond_of_row(i * COMBINE_TM), 0, 0)),
                  pl.BlockSpec(memory_space=pl.ANY)],
        out_specs=pl.BlockSpec((COMBINE_TM, D_MODEL), lambda i: (i, 0)),
        out_shape=jax.ShapeDtypeStruct((t, D_MODEL), jnp.float32),
        scratch_shapes=[pltpu.VMEM((2, TOP_K * COMBINE_TM, D_MODEL), jnp.float32),
                        pltpu.SemaphoreType.DMA((2,))],
        compiler_params=_cparams(1), name="moe_combine",
    )(pos3, pos3, h, gates_pad, mod, y_rows)


def _moe(h, hn, logits, w_gu, b_gu, w_dn, b_dn, mod):
    t = hn.shape[0]
    top_val, top_idx = lax.top_k(logits[:, :N_EXPERTS], TOP_K)
    gates = jax.nn.softmax(top_val, axis=-1)
    chosen = jnp.sum(jax.nn.one_hot(top_idx, N_EXPERTS, dtype=jnp.int32), axis=1)
    before = jnp.cumsum(chosen, axis=0) - chosen
    counts = jnp.sum(chosen, axis=0)
    padded = (counts + MOE_TM - 1) // MOE_TM * MOE_TM
    padded_end = jnp.cumsum(padded)
    padded_start = padded_end - padded
    pos = (padded_start[top_idx] + jnp.take_along_axis(before, top_idx, axis=1)).astype(jnp.int32)
    tok = jnp.broadcast_to(jnp.arange(t, dtype=jnp.int32)[:, None], (t, TOP_K))
    row_tok = jnp.zeros((MOE_ROWS,), jnp.int32).at[pos.reshape(-1)].set(tok.reshape(-1))
    block_e = jnp.minimum(
        jnp.searchsorted(padded_end, jnp.arange(MOE_BLOCKS) * MOE_TM, side='right'),
        N_EXPERTS - 1).astype(jnp.int32)
    n_valid = (padded_end[-1] // MOE_TM).astype(jnp.int32).reshape(1)

    x_rows = _moe_dispatch(hn, row_tok, n_valid)
    act = _grouped_matmul(x_rows, w_gu, b_gu, block_e, n_valid, swiglu=True, tn=MOE_TN,
                          out_dtype=jnp.bfloat16, name="moe_gate_up")
    y_rows = _grouped_matmul(act, w_dn, b_dn, block_e, n_valid, swiglu=False, tn=MOE_TN,
                             out_dtype=jnp.float32, name="moe_down")
    return _moe_combine(h, y_rows, pos, gates, mod, 5)


CONV_TM = 256
N_CHUNKS = T_ALL // SSD_CHUNK
CTX_CHUNKS = T_CTX // SSD_CHUNK
N_SEQS = BATCH + DEC_BATCH


def _seq_len_of_row(row):
    return jnp.where(row < T_CTX, SEQ, DEC_SEQ)


def _seq_pos_of_row(row):
    return jnp.where(row < T_CTX, row % SEQ, (row - T_CTX) % DEC_SEQ)


def _seq_of_chunk(n):
    return jnp.where(n < CTX_CHUNKS, n // (SEQ // SSD_CHUNK),
                     BATCH + (n - CTX_CHUNKS) // (DEC_SEQ // SSD_CHUNK))


def _dwconv_kernel(x_ref, prev_ref, next_ref, w_ref, b_ref, o_ref, *, width, silu):
    row0 = pl.program_id(1) * CONV_TM
    pos0 = _seq_pos_of_row(row0)
    has_prev = pos0 != 0
    has_next = pos0 + CONV_TM != _seq_len_of_row(row0)
    x = x_ref[...]
    rid = lax.broadcasted_iota(jnp.int32, x.shape, 0)
    prev = jnp.where(has_prev, prev_ref[7:8, :], 0.0)
    nxt0 = jnp.where(has_next, next_ref[0:1, :], 0.0)
    acc = w_ref[1:2, :] * x + b_ref[...]
    xm1 = jnp.where(rid == 0, prev, pltpu.roll(x, 1, 0))
    acc = acc + w_ref[0:1, :] * xm1
    xp1 = jnp.where(rid == CONV_TM - 1, nxt0, pltpu.roll(x, CONV_TM - 1, 0))
    acc = acc + w_ref[2:3, :] * xp1
    if width == 4:
        nxt1 = jnp.where(has_next, next_ref[1:2, :], 0.0)
        xp2 = pltpu.roll(x, CONV_TM - 2, 0)
        xp2 = jnp.where(rid == CONV_TM - 2, nxt0, jnp.where(rid == CONV_TM - 1, nxt1, xp2))
        acc = acc + w_ref[3:4, :] * xp2
    if silu:
        acc = acc * jax.nn.sigmoid(acc)
    o_ref[...] = acc


def _dwconv(x, w, b, *, in_col_blocks, tc, silu, name):
    width = w.shape[0]
    nblk = len(in_col_blocks)
    assert w.shape[1] == nblk * tc and width in (3, 4)
    cols = np.asarray(in_col_blocks, np.int32)
    if np.array_equal(cols, np.arange(nblk)):
        col = lambda j: j
    else:
        assert nblk == 3 and cols[0] == 0 and cols[1] == 1
        col = lambda j: jnp.where(j < 2, j, int(cols[2]))
    w8 = jnp.pad(w, ((0, 8 - width), (0, 0)))
    sub = CONV_TM // 8
    last8 = T_ALL // 8 - 1
    return pl.pallas_call(
        functools.partial(_dwconv_kernel, width=width, silu=silu),
        grid=(nblk, T_ALL // CONV_TM),
        in_specs=[pl.BlockSpec((CONV_TM, tc), lambda j, i: (i, col(j))),
                  pl.BlockSpec((8, tc), lambda j, i: (jnp.maximum(i * sub - 1, 0), col(j))),
                  pl.BlockSpec((8, tc), lambda j, i: (jnp.minimum((i + 1) * sub, last8), col(j))),
                  pl.BlockSpec((8, tc), lambda j, i: (0, j)),
                  pl.BlockSpec((1, tc), lambda j, i: (0, j))],
        out_specs=pl.BlockSpec((CONV_TM, tc), lambda j, i: (i, j)),
        out_shape=jax.ShapeDtypeStruct((T_ALL, nblk * tc), jnp.float32),
        compiler_params=_cparams(2), name=name,
    )(x, x, x, w8, b.reshape(1, -1))


def _softplus(x):
    return jnp.maximum(x, 0.0) + jnp.log(1.0 + jnp.exp(-jnp.abs(x)))


def _ssd_kernel(*refs, reverse):
    if reverse:
        (xbc_ref, zdt_ref, bias_ref, alog_ref, h0_ref, yf_ref, z_ref, dskip_ref, gn_ref,
         y_ref, st_ref, s_scr, y_scr) = refs
    else:
        xbc_ref, zdt_ref, bias_ref, alog_ref, h0_ref, y_ref, st_ref, s_scr = refs
    f32, bf = jnp.float32, jnp.bfloat16
    q = SSD_CHUNK
    step = pl.program_id(0)
    n = (N_CHUNKS - 1 - step) if reverse else step
    row0 = n * q
    pos0 = _seq_pos_of_row(row0)
    seq_len = _seq_len_of_row(row0)
    starts_seq = (pos0 + q == seq_len) if reverse else (pos0 == 0)
    ends_seq = (pos0 == 0) if reverse else (pos0 + q == seq_len)

    @pl.when(starts_seq)
    def _():
        s_scr[...] = jnp.where(row0 < T_CTX, 0.0, h0_ref[...])

    ri = lax.broadcasted_iota(jnp.int32, (q, q), 0)
    ci = lax.broadcasted_iota(jnp.int32, (q, q), 1)
    mask = (ci >= ri) if reverse else (ci <= ri)
    lane_lo = ci < SSD_HEAD_DIM
    row_lo = ri < SSD_HEAD_DIM

    dt = _softplus(zdt_ref[...] + bias_ref[...])
    da = dt * (-jnp.exp(alog_ref[...]))
    ones = jnp.where(mask, 1.0, 0.0).astype(bf)
    d1 = da.astype(bf)
    r1 = da - d1.astype(f32)
    d2 = r1.astype(bf)
    d3 = (r1 - d2.astype(f32)).astype(bf)
    cs = (jnp.dot(ones, d1, preferred_element_type=f32)
          + jnp.dot(ones, d2, preferred_element_type=f32)
          + jnp.dot(ones, d3, preferred_element_type=f32))
    cs_t = cs.T
    tot = cs[0:1, :] if reverse else cs[q - 1:q, :]
    exp_cs = jnp.exp(cs)
    d_end = jnp.exp(tot - cs)
    exp_tot = jnp.exp(tot)

    def pair_cols(v, p):
        return jnp.where(lane_lo, v[:, 2 * p:2 * p + 1], v[:, 2 * p + 1:2 * p + 2])

    dn_t = (((1,), (1,)), ((), ()))
    for g in range(SSD_GROUPS):
        b_g = xbc_ref[:, SSD_WIDTH + g * SSD_STATE:SSD_WIDTH + (g + 1) * SSD_STATE].astype(bf)
        c_off = SSD_WIDTH + SSD_GROUPS * SSD_STATE
        c_g = xbc_ref[:, c_off + g * SSD_STATE:c_off + (g + 1) * SSD_STATE].astype(bf)
        cb = lax.dot_general(c_g, b_g, dn_t, preferred_element_type=f32)
        pairs_per_group = SSD_HEADS // SSD_GROUPS // 2
        for p in range(g * pairs_per_group, (g + 1) * pairs_per_group):
            lo, hi = p * 2 * SSD_HEAD_DIM, (p + 1) * 2 * SSD_HEAD_DIM
            xs = xbc_ref[:, lo:hi]
            xdt = xs * pair_cols(dt, p)
            y = jnp.zeros((q, 2 * SSD_HEAD_DIM), f32)
            for h, keep in ((2 * p, lane_lo), (2 * p + 1, ~lane_lo)):
                seg = cs[:, h:h + 1] - cs_t[h:h + 1, :]
                w = jnp.where(mask, jnp.exp(jnp.minimum(seg, 0.0)), 0.0) * cb
                y = y + jnp.dot(w.astype(bf), jnp.where(keep, xdt, 0.0).astype(bf),
                                preferred_element_type=f32)
            s_pair = s_scr[lo:hi, :]
            y_off = lax.dot_general(c_g, s_pair.astype(bf), dn_t, preferred_element_type=f32)
            y = y + y_off * pair_cols(exp_cs, p)
            st = jnp.dot((xdt * pair_cols(d_end, p)).T.astype(bf), b_g,
                         preferred_element_type=f32)
            keep_s = jnp.where(row_lo, exp_tot[:, 2 * p:2 * p + 1], exp_tot[:, 2 * p + 1:2 * p + 2])
            s_scr[lo:hi, :] = s_pair * keep_s + st
            if reverse:
                y_scr[:, lo:hi] = y + yf_ref[:, lo:hi] + dskip_ref[:, lo:hi] * xs
            else:
                y_ref[:, lo:hi] = y

    if reverse:
        zz = z_ref[...]
        yy = y_scr[...] * (zz * jax.nn.sigmoid(zz))
        gw = SSD_WIDTH // SSD_GROUPS
        for g in range(SSD_GROUPS):
            y_g = yy[:, g * gw:(g + 1) * gw]
            y_g = y_g * lax.rsqrt(jnp.mean(y_g * y_g, axis=-1, keepdims=True) + NORM_EPS)
            y_ref[:, g * gw:(g + 1) * gw] = (y_g * gn_ref[:, g * gw:(g + 1) * gw]).astype(y_ref.dtype)

    @pl.when(ends_seq)
    def _():
        st_ref[...] = s_scr[...]


def _ssd(xbc, z_dt, dt_bias, a_log, h0, *, reverse, finish=None):
    q = SSD_CHUNK
    d = 1 if reverse else 0
    chunk = (lambda s: N_CHUNKS - 1 - s) if reverse else (lambda s: s)
    pad_row = lambda v: jnp.pad(v, (0, V7X_LANES - SSD_HEADS)).reshape(1, V7X_LANES)
    in_specs = [pl.BlockSpec((q, SSD_CONV_CH), lambda s: (chunk(s), 0)),
                pl.BlockSpec((q, V7X_LANES), lambda s: (chunk(s), d)),
                pl.BlockSpec((1, V7X_LANES), lambda s: (0, 0)),
                pl.BlockSpec((1, V7X_LANES), lambda s: (0, 0)),
                pl.BlockSpec((None, SSD_WIDTH, SSD_STATE),
                             lambda s: (jnp.maximum(_seq_of_chunk(chunk(s)) - BATCH, 0), 0, 0))]
    args = [xbc, z_dt, pad_row(dt_bias), pad_row(a_log), h0]
    scratch = [pltpu.VMEM((SSD_WIDTH, SSD_STATE), jnp.float32)]
    y_dtype = jnp.float32
    if reverse:
        y_fwd, z_ssd, dskip_row, gn_row = finish
        in_specs += [pl.BlockSpec((q, SSD_WIDTH), lambda s: (chunk(s), 0)),
                     pl.BlockSpec((q, SSD_WIDTH), lambda s: (chunk(s), 1)),
                     pl.BlockSpec((1, SSD_WIDTH), lambda s: (0, 0)),
                     pl.BlockSpec((1, SSD_WIDTH), lambda s: (0, 0))]
        args += [y_fwd, z_ssd, dskip_row, gn_row]
        scratch.append(pltpu.VMEM((q, SSD_WIDTH), jnp.float32))
        y_dtype = jnp.bfloat16
    return pl.pallas_call(
        functools.partial(_ssd_kernel, reverse=reverse),
        grid=(N_CHUNKS,),
        in_specs=in_specs,
        out_specs=[pl.BlockSpec((q, SSD_WIDTH), lambda s: (chunk(s), 0)),
                   pl.BlockSpec((None, SSD_WIDTH, SSD_STATE),
                                lambda s: (_seq_of_chunk(chunk(s)), 0, 0))],
        out_shape=[jax.ShapeDtypeStruct((T_ALL, SSD_WIDTH), y_dtype),
                   jax.ShapeDtypeStruct((N_SEQS, SSD_WIDTH, SSD_STATE), jnp.float32)],
        scratch_shapes=scratch,
        compiler_params=_cparams(1), name="ssd_bwd" if reverse else "ssd_fwd",
    )(*args)


ATT_TQ = 256
HEAD_W = 2 * DA_HEAD_DIM


def _rope_tables():
    t = jnp.arange(DEC_SEQ)
    pos = jnp.stack([t // GRID_W, t % GRID_W], axis=-1).astype(jnp.float32)
    inv = 1.0 / (ROPE_BASE ** (jnp.arange(ROPE_FREQS, dtype=jnp.float32) / ROPE_FREQS))
    ang = pos[:, :, None] * inv
    cos, sin = jnp.cos(ang), jnp.sin(ang)
    zero = jnp.zeros_like(sin)
    lanes = lambda first, second: jnp.tile(
        jnp.stack([first, second], axis=2).reshape(DEC_SEQ, DA_HEAD_DIM), (1, 2))
    return lanes(cos, cos), lanes(-sin, zero), lanes(zero, sin)


def _attn_kernel(lam_ref, *refs, has_ctx):
    if has_ctx:
        (q_ref, k_ref, v_ref, kc_ref, vc_ref, cq, saq, sbq, ck, sak, sbk, sw_ref, o_ref) = refs
    else:
        q_ref, k_ref, v_ref, sw_ref, o_ref = refs
    f32, bf = jnp.float32, jnp.bfloat16
    half = ROPE_FREQS

    def rope(x, c, sa, sb):
        return (x * c[...] + pltpu.roll(x, HEAD_W - half, 1) * sa[...]
                + pltpu.roll(x, half, 1) * sb[...])

    q, k = q_ref[...], k_ref[...]
    if has_ctx:
        q, k = rope(q, cq, saq, sbq), rope(k, ck, sak, sbk)
    lane = lax.broadcasted_iota(jnp.int32, q.shape, 1)
    kb, vb = k.astype(bf), v_ref[...].astype(bf)
    if has_ctx:
        kb = jnp.concatenate([kc_ref[...].astype(bf), kb], axis=0)
        vb = jnp.concatenate([vc_ref[...].astype(bf), vb], axis=0)
    scale = DA_HEAD_DIM ** -0.5

    def probs(qm):
        s = lax.dot_general(qm.astype(bf), kb, (((1,), (1,)), ((), ())),
                            preferred_element_type=f32) * scale
        e = jnp.exp(s - jnp.max(s, axis=-1, keepdims=True))
        return e / jnp.sum(e, axis=-1, keepdims=True)

    w = probs(jnp.where(lane < DA_HEAD_DIM, q, 0.0)) - lam_ref[0] * probs(
        jnp.where(lane >= DA_HEAD_DIM, q, 0.0))
    o = jnp.dot(w.astype(bf), vb, preferred_element_type=f32)
    o = o * lax.rsqrt(jnp.mean(o * o, axis=-1, keepdims=True) + NORM_EPS) * sw_ref[...]
    o_ref[...] = (o * (1.0 - DA_LAMBDA_INIT)).astype(o_ref.dtype)


def _diff_attention(z_qkv, cache_k, cache_v, lam, subln_w):
    sw = subln_w.reshape(1, HEAD_W)
    o_ctx = pl.pallas_call(
        functools.partial(_attn_kernel, has_ctx=False),
        grid_spec=pltpu.PrefetchScalarGridSpec(
            num_scalar_prefetch=1, grid=(BATCH, DA_HEADS),
            in_specs=[pl.BlockSpec((SEQ, HEAD_W), lambda b, h, lam: (b, h)),
                      pl.BlockSpec((SEQ, HEAD_W), lambda b, h, lam: (b, DA_HEADS + h)),
                      pl.BlockSpec((SEQ, HEAD_W), lambda b, h, lam: (b, 2 * DA_HEADS + h)),
                      pl.BlockSpec((1, HEAD_W), lambda b, h, lam: (0, 0))],
            out_specs=pl.BlockSpec((SEQ, HEAD_W), lambda b, h, lam: (b, h))),
        out_shape=jax.ShapeDtypeStruct((T_CTX, DA_WIDTH), jnp.bfloat16),
        compiler_params=_cparams(2), name="attn_ctx",
    )(lam, z_qkv, z_qkv, z_qkv, sw)

    tabs = _rope_tables()
    nq = DEC_SEQ // ATT_TQ
    q_blk0 = T_CTX // ATT_TQ
    kv_blk0 = T_CTX // DEC_SEQ
    kc = cache_k.reshape(DEC_BATCH * PAST_LEN, DA_WIDTH)
    vc = cache_v.reshape(DEC_BATCH * PAST_LEN, DA_WIDTH)
    q_row = lambda b, h, i, lam: (q_blk0 + b * nq + i, h)
    tab_q = pl.BlockSpec((ATT_TQ, HEAD_W), lambda b, h, i, lam: (i, 0))
    tab_k = pl.BlockSpec((DEC_SEQ, HEAD_W), lambda b, h, i, lam: (0, 0))
    o_dec = pl.pallas_call(
        functools.partial(_attn_kernel, has_ctx=True),
        grid_spec=pltpu.PrefetchScalarGridSpec(
            num_scalar_prefetch=1, grid=(DEC_BATCH, DA_HEADS, nq),
            in_specs=[pl.BlockSpec((ATT_TQ, HEAD_W), q_row),
                      pl.BlockSpec((DEC_SEQ, HEAD_W), lambda b, h, i, lam: (kv_blk0 + b, DA_HEADS + h)),
                      pl.BlockSpec((DEC_SEQ, HEAD_W), lambda b, h, i, lam: (kv_blk0 + b, 2 * DA_HEADS + h)),
                      pl.BlockSpec((PAST_LEN, HEAD_W), lambda b, h, i, lam: (b, h)),
                      pl.BlockSpec((PAST_LEN, HEAD_W), lambda b, h, i, lam: (b, h)),
                      tab_q, tab_q, tab_q, tab_k, tab_k, tab_k,
                      pl.BlockSpec((1, HEAD_W), lambda b, h, i, lam: (0, 0))],
            out_specs=pl.BlockSpec((ATT_TQ, HEAD_W), lambda b, h, i, lam: (b * nq + i, h))),
        out_shape=jax.ShapeDtypeStruct((T_DEC, DA_WIDTH), jnp.bfloat16),
        compiler_params=_cparams(3), name="attn_dec",
    )(lam, z_qkv, z_qkv, z_qkv, kc, vc, tabs[0], tabs[1], tabs[2], tabs[0], tabs[1], tabs[2], sw)
    return jnp.concatenate([o_ctx, o_dec], axis=0)


def _dft_angle(L, s_offset):
    k = jnp.arange(L, dtype=jnp.int32)[:, None]
    s = jnp.arange(L, dtype=jnp.int32)[None, :] + s_offset
    m = ((2 * k + 1) * s) % (4 * L)
    return m.astype(jnp.float32) * (math.pi / (2 * L))


def _hyena_tables(L):
    bf = jnp.bfloat16
    th = _dft_angle(L, 0)
    c, s = jnp.cos(th), jnp.sin(th)
    fwd = jnp.concatenate([c, -s], axis=0).astype(bf)
    inv = (jnp.concatenate([c.T, -s.T], axis=1) * (1.0 / L)).astype(bf)
    th_f = _dft_angle(L, -(L // 2))
    filt_fwd = jnp.concatenate([jnp.cos(th_f), -jnp.sin(th_f)], axis=0).astype(bf)
    return fwd, inv, filt_fwd


def _hyena_filter_kernel(band_ref, w1t_ref, w1c_ref, w1s_ref, b1_ref, w2_ref, b2_ref, w3_ref,
                         b3_ref, fr_ref, w4_ref, dl_ref, o_ref, h_scr, *, L):
    t = lax.broadcasted_iota(jnp.int32, (L, 1), 0).astype(jnp.float32)

    @pl.when(pl.program_id(0) == 0)
    def _():
        ang = ((2.0 * math.pi / L) * t) * band_ref[...]
        fr = fr_ref[...]
        z = ((t / L) * w1t_ref[...] + _dot3(jnp.cos(ang), w1c_ref[...])
             + _dot3(jnp.sin(ang), w1s_ref[...]))
        h = jnp.sin(fr * (z + b1_ref[...]))
        h = jnp.sin(fr * (_dot3(h, w2_ref[...]) + b2_ref[...]))
        h_scr[...] = jnp.sin(fr * (_dot3(h, w3_ref[...]) + b3_ref[...]))

    dist = jnp.abs(t - (L // 2)) / L
    o_ref[...] = _dot3(h_scr[...], w4_ref[...]) * jnp.exp(-dist * dl_ref[...])


def _hyena_filter(L, w1, b1, w2, b2, w3, b3, w4, freq):
    hid = w2.shape[0]
    tn = 512
    bands = jnp.linspace(1e-4, HY_BANDS - 1, HY_BANDS, dtype=jnp.float32)
    lane_pad = V7X_LANES - HY_BANDS
    band_row = jnp.pad(bands, (0, lane_pad)).reshape(1, V7X_LANES)
    pad_rows = lambda w: jnp.pad(w, ((0, lane_pad), (0, 0)))
    deltas = jnp.abs(jnp.linspace(math.log(HY_TARGET) / HY_SLOW_DECAY,
                                  math.log(HY_TARGET) / HY_FAST_DECAY, HY_WIDTH,
                                  dtype=jnp.float32)).reshape(1, HY_WIDTH)
    full = lambda shape: pl.BlockSpec(shape, lambda j: (0,) * len(shape))
    return pl.pallas_call(
        functools.partial(_hyena_filter_kernel, L=L),
        grid=(HY_WIDTH // tn,),
        in_specs=[full((1, V7X_LANES)), full((1, hid)), full((V7X_LANES, hid)),
                  full((V7X_LANES, hid)), full((1, hid)), full((hid, hid)), full((1, hid)),
                  full((hid, hid)), full((1, hid)), full((1, hid)),
                  pl.BlockSpec((hid, tn), lambda j: (0, j)),
                  pl.BlockSpec((1, tn), lambda j: (0, j))],
        out_specs=pl.BlockSpec((L, tn), lambda j: (0, j)),
        out_shape=jax.ShapeDtypeStruct((L, HY_WIDTH), jnp.float32),
        scratch_shapes=[pltpu.VMEM((L, hid), jnp.float32)],
        compiler_params=_cparams(1), name=f"hyena_filter_{L}",
    )(band_row, w1[0:1], pad_rows(w1[1:1 + HY_BANDS]), pad_rows(w1[1 + HY_BANDS:]),
      b1.reshape(1, hid), w2, b2.reshape(1, hid), w3, b3.reshape(1, hid), freq.reshape(1, hid),
      w4, deltas)


def _hyena_conv_kernel(*refs):
    (x0_ref, x1_ref, v_ref, fr_ref, fi_ref, hr_ref, hi_ref, ir_ref, ii_ref, fb_ref) = refs[:10]
    o_ref, vbf, acc = refs[-3:]
    f32, bf = jnp.float32, jnp.bfloat16
    kb = pl.program_id(2)

    @pl.when(kb == 0)
    def _():
        vbf[...] = (v_ref[...] * x1_ref[...]).astype(bf)
        acc[...] = jnp.zeros_like(acc)

    vr = jnp.dot(fr_ref[...], vbf[...], preferred_element_type=f32)
    vi = jnp.dot(fi_ref[...], vbf[...], preferred_element_type=f32)
    hr, hi = hr_ref[...], hi_ref[...]
    yr = (vr * hr - vi * hi).astype(bf)
    yi = (vr * hi + vi * hr).astype(bf)
    acc[...] += (jnp.dot(ir_ref[...], yr, preferred_element_type=f32)
                 + jnp.dot(ii_ref[...], yi, preferred_element_type=f32))

    @pl.when(kb == pl.num_programs(2) - 1)
    def _():
        vv = v_ref[...] * x1_ref[...]
        o_ref[...] = ((acc[...] + vv * fb_ref[...]) * x0_ref[...]).astype(o_ref.dtype)


def _hyena_conv(u, filt_spec, tables, f_bias, *, L, row_blk0, n_seq, tn, kblk):
    fwd, inv, _ = tables
    nj, nk = HY_WIDTH // tn, L // kblk
    u_spec = lambda part: pl.BlockSpec((L, tn), lambda b, j, k: (row_blk0 + b, part * nj + j))
    in_specs = [u_spec(0), u_spec(1), u_spec(2),
                pl.BlockSpec((kblk, L), lambda b, j, k: (k, 0)),
                pl.BlockSpec((kblk, L), lambda b, j, k: (nk + k, 0)),
                pl.BlockSpec((kblk, tn), lambda b, j, k: (k, j)),
                pl.BlockSpec((kblk, tn), lambda b, j, k: (nk + k, j)),
                pl.BlockSpec((L, kblk), lambda b, j, k: (0, k)),
                pl.BlockSpec((L, kblk), lambda b, j, k: (0, nk + k)),
                pl.BlockSpec((1, tn), lambda b, j, k: (0, j))]
    args = [u, u, u, fwd, fwd, filt_spec, filt_spec, inv, inv, f_bias.reshape(1, HY_WIDTH)]
    return pl.pallas_call(
        _hyena_conv_kernel,
        grid=(n_seq, nj, nk),
        in_specs=in_specs,
        out_specs=pl.BlockSpec((L, tn), lambda b, j, k: (b, j)),
        out_shape=jax.ShapeDtypeStruct((n_seq * L, HY_WIDTH), jnp.bfloat16),
        scratch_shapes=[pltpu.VMEM((L, tn), jnp.bfloat16), pltpu.VMEM((L, tn), jnp.float32)],
        compiler_params=_cparams(3), name=f"hyena_conv_{L}",
    )(*args)


def _hyena_mix(u, filt_p, f_bias):
    outs = []
    for L, row_blk0, n_seq, tn, kblk in ((SEQ, 0, BATCH, 1024, SEQ),
                                        (DEC_SEQ, T_CTX // DEC_SEQ, DEC_BATCH, 256, 512)):
        tables = _hyena_tables(L)
        filt = _hyena_filter(L, *filt_p)
        spec = _matmul(tables[2], filt, HY_WIDTH, tm=min(1024, 2 * L), name=f"hyena_spec_{L}")
        outs.append(_hyena_conv(u, spec, tables, f_bias, L=L, row_blk0=row_blk0, n_seq=n_seq,
                                tn=tn, kblk=kblk))
    return jnp.concatenate(outs, axis=0)


def kernel(x_prompt, x_sample, c, state_l0_ssm_fwd, state_l0_ssm_bwd, cache_l0_k, cache_l0_v, c_ctx, l0_norm_mix, l0_norm_ffn, l0_ada_w, l0_ada_b, l0_w_in, l0_conv_w, l0_conv_b, l0_a_log_fwd, l0_a_log_bwd, l0_dt_bias_fwd, l0_dt_bias_bwd, l0_d_skip, l0_gnorm_w, l0_lambda_q1, l0_lambda_k1, l0_lambda_q2, l0_lambda_k2, l0_subln_w, l0_w_out, l0_router_w, l0_router_b, l0_w_gate_up, l0_b_gate_up, l0_w_down, l0_b_down, l1_norm_mix, l1_norm_ffn, l1_ada_w, l1_ada_b, l1_w_in, l1_b_in, l1_short_w, l1_short_b, l1_filt_w1, l1_filt_b1, l1_filt_w2, l1_filt_b2, l1_filt_w3, l1_filt_b3, l1_filt_w4, l1_filt_freq, l1_filt_bias, l1_w_out, l1_b_out, l1_router_w, l1_router_b, l1_w_gate_up, l1_b_gate_up, l1_w_down, l1_b_down, final_norm):
    h = jnp.concatenate([x_prompt.reshape(T_CTX, D_MODEL), x_sample.reshape(T_DEC, D_MODEL)], axis=0)
    cond8 = jnp.concatenate([c_ctx[None, :], c,
                             jnp.zeros((MOD_ROWS - N_COND, D_MODEL), jnp.float32)], axis=0)

    def router_args(rw, rb):
        pad = V7X_LANES - N_EXPERTS
        return (jnp.pad(rw, ((0, 0), (0, pad))),
                jnp.pad(rb, (0, pad), constant_values=-1e30).reshape(1, V7X_LANES))

    def ffn(h, mod, norm_g, rw, rb, w_gu, b_gu, w_dn, b_dn):
        hn, logits = _normmod(h, norm_g, mod, 3, router=router_args(rw, rb))
        return _moe(h, hn, logits, w_gu, b_gu, w_dn, b_dn, mod)

    mod0 = _adaln(cond8, l0_ada_w, l0_ada_b)
    hn = _normmod(h, l0_norm_mix, mod0, 0)
    c_dt = sum(L0_SPLITS[:4])
    c_q = c_dt + 2 * SSD_HEADS
    z_ssd = _matmul(hn, l0_w_in, c_dt, tn=512, name="l0_in_ssd")
    lane_pad = jnp.zeros((D_MODEL, V7X_LANES - SSD_HEADS), jnp.float32)
    w_dt = jnp.concatenate([l0_w_in[:, c_dt:c_dt + SSD_HEADS], lane_pad,
                            l0_w_in[:, c_dt + SSD_HEADS:c_q], lane_pad], axis=1)
    z_dt = _matmul(hn, w_dt, 2 * V7X_LANES, tn=2 * V7X_LANES, name="l0_in_dt")
    z_qkv = _matmul(hn, l0_w_in[:, c_q:], 3 * DA_WIDTH, name="l0_in_qkv")
    xbc = _dwconv(z_ssd, l0_conv_w, l0_conv_b, in_col_blocks=(0, 1, 4), tc=512, silu=True,
                  name="ssd_conv")
    h0_shape = (DEC_BATCH, SSD_WIDTH, SSD_STATE)
    y_fwd, st_fwd = _ssd(xbc, z_dt, l0_dt_bias_fwd, l0_a_log_fwd,
                         state_l0_ssm_fwd.reshape(h0_shape), reverse=False)
    y_ssd, st_bwd = _ssd(xbc, z_dt, l0_dt_bias_bwd, l0_a_log_bwd,
                         state_l0_ssm_bwd.reshape(h0_shape), reverse=True,
                         finish=(y_fwd, z_ssd, jnp.repeat(l0_d_skip, SSD_HEAD_DIM).reshape(1, -1),
                                 l0_gnorm_w.reshape(1, -1)))
    lam = (jnp.exp(jnp.sum(l0_lambda_q1 * l0_lambda_k1))
           - jnp.exp(jnp.sum(l0_lambda_q2 * l0_lambda_k2)) + DA_LAMBDA_INIT).reshape(1)
    o_att = _diff_attention(z_qkv, cache_l0_k, cache_l0_v, lam, l0_subln_w)
    h = _matmul(y_ssd, l0_w_out, D_MODEL, a2=o_att, gated=(h, mod0, 2), name="l0_out")
    h = ffn(h, mod0, l0_norm_ffn, l0_router_w, l0_router_b, l0_w_gate_up, l0_b_gate_up,
            l0_w_down, l0_b_down)
    st_shape = (BATCH, SSD_HEADS, SSD_HEAD_DIM, SSD_STATE)
    kv_shape = (BATCH, SEQ, DA_HEADS, HEAD_W)
    produced = (st_fwd[:BATCH].reshape(st_shape), st_bwd[:BATCH].reshape(st_shape),
                z_qkv[:T_CTX, DA_WIDTH:2 * DA_WIDTH].reshape(kv_shape),
                z_qkv[:T_CTX, 2 * DA_WIDTH:].reshape(kv_shape))

    mod1 = _adaln(cond8, l1_ada_w, l1_ada_b)
    hn = _normmod(h, l1_norm_mix, mod1, 0)
    u = _matmul(hn, l1_w_in, 3 * HY_WIDTH, bias=l1_b_in, name="l1_in")
    u = _dwconv(u, l1_short_w, l1_short_b, in_col_blocks=(0, 1, 2), tc=HY_WIDTH, silu=False,
                name="hyena_conv")
    filt_p = (l1_filt_w1, l1_filt_b1, l1_filt_w2, l1_filt_b2, l1_filt_w3, l1_filt_b3, l1_filt_w4,
              l1_filt_freq)
    y = _hyena_mix(u, filt_p, l1_filt_bias)
    h = _matmul(y, l1_w_out, D_MODEL, bias=l1_b_out, gated=(h, mod1, 2), name="l1_out")
    h = ffn(h, mod1, l1_norm_ffn, l1_router_w, l1_router_b, l1_w_gate_up, l1_b_gate_up,
            l1_w_down, l1_b_down)

    y_prompt = _final_norm(h, final_norm, 0, T_CTX).reshape(BATCH, SEQ, D_MODEL)
    y_sample = _final_norm(h, final_norm, T_CTX, T_DEC).reshape(DEC_BATCH, DEC_SEQ, D_MODEL)
    return (y_prompt, y_sample) + produced
```

```python
import functools
import math

import numpy as np
import jax
import jax.numpy as jnp
from jax import lax
from jax.experimental import pallas as pl
from jax.experimental.pallas import tpu as pltpu

D_MODEL = 2048
BATCH = 32
SEQ = 256
DEC_BATCH = 2
DEC_SEQ = 2048
PAST_LEN = 512
GRID_W = 64
NORM_EPS = 1e-6
SSD_WIDTH = D_MODEL // 2
SSD_HEAD_DIM = 64
SSD_HEADS = SSD_WIDTH // SSD_HEAD_DIM
SSD_GROUPS = 2
SSD_STATE = 128
SSD_CHUNK = 128
SSD_CONV_CH = SSD_WIDTH + 2 * SSD_GROUPS * SSD_STATE
DA_WIDTH = D_MODEL // 2
DA_HEAD_DIM = 64
DA_HEADS = DA_WIDTH // (2 * DA_HEAD_DIM)
DA_LAMBDA_INIT = 0.8 - 0.6 * math.exp(-0.3 * 0)
ROPE_BASE = 10000.0
ROPE_FREQS = DA_HEAD_DIM // 4
L0_SPLITS = (SSD_WIDTH, SSD_WIDTH, SSD_GROUPS * SSD_STATE, SSD_GROUPS * SSD_STATE,
             SSD_HEADS, SSD_HEADS, DA_WIDTH, DA_WIDTH, DA_WIDTH)
HY_WIDTH = D_MODEL
HY_BANDS = 16
HY_FAST_DECAY = 0.3
HY_SLOW_DECAY = 1.5
HY_TARGET = 1e-2
N_EXPERTS = 32
TOP_K = 4
SWIGLU_LIMIT = 7.0
SWIGLU_ALPHA = 1.702

T_CTX = BATCH * SEQ
T_DEC = DEC_BATCH * DEC_SEQ
T_ALL = T_CTX + T_DEC
N_COND = 1 + DEC_BATCH
MOD_ROWS = 8

V7X_LANES = 128
V7X_MXU_DIM = 256
V7X_VMEM_BYTES = 64 * 1024 * 1024
VMEM_LIMIT = V7X_VMEM_BYTES - 8 * 1024 * 1024

MOE_TM = 512
MOE_TN = 2048
MOE_ROWS = T_ALL * TOP_K + N_EXPERTS * MOE_TM
MOE_BLOCKS = MOE_ROWS // MOE_TM


def _cparams(n_grid):
    return pltpu.CompilerParams(dimension_semantics=("arbitrary",) * n_grid,
                                vmem_limit_bytes=VMEM_LIMIT)


def _cond_of_row(row):
    return jnp.where(row < T_CTX, 0, (row - T_CTX) // DEC_SEQ + 1)


def _split_bf16(x):
    hi = x.astype(jnp.bfloat16)
    lo = (x - hi.astype(jnp.float32)).astype(jnp.bfloat16)
    return hi, lo


def _dot3(a, w):
    a_hi, a_lo = _split_bf16(a)
    w_hi, w_lo = _split_bf16(w)
    f32 = jnp.float32
    return (jnp.dot(a_hi, w_hi, preferred_element_type=f32)
            + jnp.dot(a_lo, w_hi, preferred_element_type=f32)
            + jnp.dot(a_hi, w_lo, preferred_element_type=f32))


def _adaln_kernel(c_ref, w_ref, b_ref, o_ref):
    c = c_ref[...]
    a = c * jax.nn.sigmoid(c)
    o_ref[...] = _dot3(a, w_ref[...]) + b_ref[...]


def _adaln(cond8, w, b):
    n = w.shape[1]
    tn = 512
    out = pl.pallas_call(
        _adaln_kernel,
        grid=(n // tn,),
        in_specs=[pl.BlockSpec((MOD_ROWS, D_MODEL), lambda j: (0, 0)),
                  pl.BlockSpec((D_MODEL, tn), lambda j: (0, j)),
                  pl.BlockSpec((1, tn), lambda j: (0, j))],
        out_specs=pl.BlockSpec((MOD_ROWS, tn), lambda j: (0, j)),
        out_shape=jax.ShapeDtypeStruct((MOD_ROWS, n), jnp.float32),
        compiler_params=_cparams(1),
        name="adaln",
    )(cond8, w, b.reshape(1, n))
    m = out[:N_COND].reshape(N_COND, 6, D_MODEL)
    return jnp.pad(m, ((0, 0), (0, MOD_ROWS - 6), (0, 0)))


HALF_D = D_MODEL // 2


def _pack_rows(x):
    as_bits = lambda v: lax.bitcast_convert_type(v.astype(jnp.bfloat16).astype(jnp.float32),
                                                 jnp.uint32)
    lo, hi = as_bits(x[:, :HALF_D]), as_bits(x[:, HALF_D:])
    return (lo >> 16) | (hi & jnp.uint32(0xFFFF0000))


def _unpack_rows(p):
    lo = lax.bitcast_convert_type(p << 16, jnp.float32)
    hi = lax.bitcast_convert_type(p & jnp.uint32(0xFFFF0000), jnp.float32)
    return lo, hi


def _norm_rows(x, g):
    var = jnp.mean(x * x, axis=-1, keepdims=True)
    return x * lax.rsqrt(var + NORM_EPS) * g


def _normmod_kernel(h_ref, g_ref, mod_ref, o_ref, *, sh_row):
    y = _norm_rows(h_ref[...], g_ref[...])
    y = y * (1.0 + mod_ref[sh_row + 1:sh_row + 2, :]) + mod_ref[sh_row:sh_row + 1, :]
    o_ref[...] = y.astype(o_ref.dtype)


def _normmod_router_kernel(h_ref, g_ref, mod_ref, rw_ref, rb_ref, o_ref, lg_ref, *, sh_row):
    y = _norm_rows(h_ref[...], g_ref[...])
    y = y * (1.0 + mod_ref[sh_row + 1:sh_row + 2, :]) + mod_ref[sh_row:sh_row + 1, :]
    o_ref[...] = _pack_rows(y)
    lg_ref[...] = _dot3(y, rw_ref[...]) + rb_ref[...]


def _plain_norm_kernel(h_ref, g_ref, o_ref):
    o_ref[...] = _norm_rows(h_ref[...], g_ref[...])


def _normmod(h, g, mod, sh_row, router=None):
    tm = 512
    grid = (T_ALL // tm,)
    h_spec = pl.BlockSpec((tm, D_MODEL), lambda i: (i, 0))
    g_spec = pl.BlockSpec((1, D_MODEL), lambda i: (0, 0))
    mod_spec = pl.BlockSpec((None, MOD_ROWS, D_MODEL), lambda i: (_cond_of_row(i * tm), 0, 0))
    o_spec = pl.BlockSpec((tm, D_MODEL), lambda i: (i, 0))
    o_shape = jax.ShapeDtypeStruct((T_ALL, D_MODEL), jnp.bfloat16)
    if router is None:
        return pl.pallas_call(
            functools.partial(_normmod_kernel, sh_row=sh_row),
            grid=grid, in_specs=[h_spec, g_spec, mod_spec], out_specs=o_spec,
            out_shape=o_shape, compiler_params=_cparams(1), name="normmod",
        )(h, g.reshape(1, D_MODEL), mod)
    rw, rb = router
    return pl.pallas_call(
        functools.partial(_normmod_router_kernel, sh_row=sh_row),
        grid=grid,
        in_specs=[h_spec, g_spec, mod_spec,
                  pl.BlockSpec((D_MODEL, V7X_LANES), lambda i: (0, 0)),
                  pl.BlockSpec((1, V7X_LANES), lambda i: (0, 0))],
        out_specs=[pl.BlockSpec((tm, HALF_D), lambda i: (i, 0)),
                   pl.BlockSpec((tm, V7X_LANES), lambda i: (i, 0))],
        out_shape=[jax.ShapeDtypeStruct((T_ALL, HALF_D), jnp.uint32),
                   jax.ShapeDtypeStruct((T_ALL, V7X_LANES), jnp.float32)],
        compiler_params=_cparams(1), name="normmod_router",
    )(h, g.reshape(1, D_MODEL), mod, rw, rb)


def _final_norm(h, g, row0, n_rows):
    tm = 512
    blk0 = row0 // tm
    return pl.pallas_call(
        _plain_norm_kernel,
        grid=(n_rows // tm,),
        in_specs=[pl.BlockSpec((tm, D_MODEL), lambda i: (i + blk0, 0)),
                  pl.BlockSpec((1, D_MODEL), lambda i: (0, 0))],
        out_specs=pl.BlockSpec((tm, D_MODEL), lambda i: (i, 0)),
        out_shape=jax.ShapeDtypeStruct((n_rows, D_MODEL), jnp.float32),
        compiler_params=_cparams(1), name="final_norm",
    )(h, g.reshape(1, D_MODEL))


def _mm_kernel(*refs, two_a, has_bias, gated, g_row):
    it = iter(refs)
    a_ref = next(it)
    a2_ref = next(it) if two_a else None
    w_ref = next(it)
    b_ref = next(it) if has_bias else None
    res_ref, mod_ref = (next(it), next(it)) if gated else (None, None)
    o_ref, wbf_ref = next(it), next(it)

    @pl.when(pl.program_id(1) == 0)
    def _():
        wbf_ref[...] = w_ref[...].astype(jnp.bfloat16)

    k1 = a_ref.shape[1]
    acc = jnp.dot(a_ref[...], wbf_ref[:k1, :], preferred_element_type=jnp.float32)
    if two_a:
        acc = acc + jnp.dot(a2_ref[...], wbf_ref[k1:, :], preferred_element_type=jnp.float32)
    if has_bias:
        acc = acc + b_ref[...]
    if gated:
        acc = res_ref[...] + mod_ref[g_row:g_row + 1, :] * acc
    o_ref[...] = acc.astype(o_ref.dtype)


def _matmul(a, w, n_cols, *, a2=None, col_blk0=0, bias=None, gated=None,
            out_dtype=jnp.float32, tm=1024, tn=1024, name="matmul"):
    m, k1 = a.shape
    k = w.shape[0]
    assert m % tm == 0 and n_cols % tn == 0
    in_specs = [pl.BlockSpec((tm, k1), lambda j, i: (i, 0))]
    args = [a]
    if a2 is not None:
        assert k1 + a2.shape[1] == k
        in_specs.append(pl.BlockSpec((tm, k - k1), lambda j, i: (i, 0)))
        args.append(a2)
    in_specs.append(pl.BlockSpec((k, tn), lambda j, i: (0, j + col_blk0)))
    args.append(w)
    if bias is not None:
        in_specs.append(pl.BlockSpec((1, tn), lambda j, i: (0, j + col_blk0)))
        args.append(bias.reshape(1, -1))
    g_row = 0
    if gated is not None:
        res, mod, g_row = gated
        in_specs.append(pl.BlockSpec((tm, tn), lambda j, i: (i, j)))
        in_specs.append(pl.BlockSpec((None, MOD_ROWS, tn),
                                     lambda j, i: (_cond_of_row(i * tm), 0, j)))
        args += [res, mod]
    return pl.pallas_call(
        functools.partial(_mm_kernel, two_a=a2 is not None, has_bias=bias is not None,
                          gated=gated is not None, g_row=g_row),
        grid=(n_cols // tn, m // tm),
        in_specs=in_specs,
        out_specs=pl.BlockSpec((tm, tn), lambda j, i: (i, j)),
        out_shape=jax.ShapeDtypeStruct((m, n_cols), out_dtype),
        scratch_shapes=[pltpu.VMEM((k, tn), jnp.bfloat16)],
        compiler_params=_cparams(2), name=name,
    )(*args)


def _gmm_kernel(be_ref, half_ref, nv_ref, x_ref, w_ref, b_ref, *rest, swiglu):
    if swiglu:
        sel_ref, o_ref, wbf_ref = rest
    else:
        o_ref, wbf_ref = rest
    i = pl.program_id(1)
    prev = be_ref[jnp.maximum(i - 1, 0)]

    @pl.when((i == 0) | (be_ref[i] != prev))
    def _():
        wbf_ref[...] = w_ref[...].astype(jnp.bfloat16)

    def compute(n_rows):
        f32, bf = jnp.float32, jnp.bfloat16
        if n_rows < MOE_TM:
            o_ref[n_rows:, :] = jnp.zeros((MOE_TM - n_rows, o_ref.shape[1]), o_ref.dtype)
        if not swiglu:
            acc = jnp.dot(x_ref[0:n_rows, :], wbf_ref[...], preferred_element_type=f32) + b_ref[...]
            o_ref[0:n_rows, :] = acc.astype(o_ref.dtype)
            return
        x_lo, x_hi = _unpack_rows(x_ref[0:n_rows, :])
        acc = (jnp.dot(x_lo.astype(bf), wbf_ref[:HALF_D, :], preferred_element_type=f32)
               + jnp.dot(x_hi.astype(bf), wbf_ref[HALF_D:, :], preferred_element_type=f32)
               + b_ref[...])
        tn = acc.shape[1]
        wide, half = V7X_MXU_DIM, V7X_MXU_DIM // 2
        gate = jnp.minimum(acc, SWIGLU_LIMIT)
        up = pltpu.roll(jnp.clip(acc, -SWIGLU_LIMIT, SWIGLU_LIMIT), tn - 1, 1)
        act = (up + 1.0) * (gate * jax.nn.sigmoid(SWIGLU_ALPHA * gate))
        lane = lax.broadcasted_iota(jnp.int32, act.shape, 1)
        act = jnp.where(lane % 2 == 0, act, 0.0).astype(jnp.bfloat16)
        for g in range(tn // wide):
            o_ref[0:n_rows, g * half:(g + 1) * half] = jnp.dot(
                act[:, g * wide:(g + 1) * wide], sel_ref[...],
                preferred_element_type=f32).astype(o_ref.dtype)

    valid = i < nv_ref[0]

    @pl.when(valid & (half_ref[i] == 0))
    def _():
        compute(MOE_TM)

    @pl.when(valid & (half_ref[i] != 0))
    def _():
        compute(MOE_TM // 2)

    @pl.when(i >= nv_ref[0])
    def _():
        o_ref[...] = jnp.zeros_like(o_ref)


def _grouped_matmul(x, w, b, block_e, block_half, n_valid, *, swiglu, tn, out_dtype, name):
    rows = x.shape[0]
    k, n = w.shape[1:]
    if swiglu:
        tn_out, n_out = tn // 2, n // 2
    else:
        tn_out, n_out = tn, n
    in_specs = [pl.BlockSpec((MOE_TM, x.shape[1]), lambda j, i, be, hf, nv: (i, 0)),
                pl.BlockSpec((None, k, tn), lambda j, i, be, hf, nv: (be[i], 0, j)),
                pl.BlockSpec((None, 1, tn), lambda j, i, be, hf, nv: (be[i], 0, j))]
    args = [x, w, b.reshape(N_EXPERTS, 1, n)]
    if swiglu:
        sel = np.zeros((V7X_MXU_DIM, V7X_MXU_DIM // 2), np.float32)
        sel[2 * np.arange(V7X_MXU_DIM // 2), np.arange(V7X_MXU_DIM // 2)] = 1.0
        in_specs.append(pl.BlockSpec(sel.shape, lambda j, i, be, hf, nv: (0, 0)))
        args.append(jnp.asarray(sel, jnp.bfloat16))
    return pl.pallas_call(
        functools.partial(_gmm_kernel, swiglu=swiglu),
        grid_spec=pltpu.PrefetchScalarGridSpec(
            num_scalar_prefetch=3,
            grid=(n // tn, rows // MOE_TM),
            in_specs=in_specs,
            out_specs=pl.BlockSpec((MOE_TM, tn_out), lambda j, i, be, hf, nv: (i, j)),
            scratch_shapes=[pltpu.VMEM((k, tn), jnp.bfloat16)]),
        out_shape=jax.ShapeDtypeStruct((rows, n_out), out_dtype),
        compiler_params=_cparams(2), name=name,
    )(block_e, block_half, n_valid, *args)


DISPATCH_TM = 256


def _dispatch_kernel(pos_ref, hp_ref, x_init, x_hbm, sem):
    del x_init

    def body(r, carry):
        for k in range(TOP_K):
            pltpu.make_async_copy(hp_ref.at[pl.ds(r, 1)],
                                  x_hbm.at[pl.ds(pos_ref[0, r * TOP_K + k], 1)], sem).start()
        return carry
    lax.fori_loop(0, DISPATCH_TM, body, 0, unroll=4)
    for _ in range(TOP_K):
        pltpu.make_async_copy(hp_ref, x_hbm.at[pl.ds(0, DISPATCH_TM)], sem).wait()


def _moe_dispatch(hp, pos):
    t = hp.shape[0]
    nblk = t // DISPATCH_TM
    pos3 = pos.reshape(nblk, 1, DISPATCH_TM * TOP_K)
    return pl.pallas_call(
        _dispatch_kernel,
        grid=(nblk,),
        in_specs=[pl.BlockSpec((None, 1, DISPATCH_TM * TOP_K), lambda i: (i, 0, 0),
                               memory_space=pltpu.SMEM),
                  pl.BlockSpec((DISPATCH_TM, HALF_D), lambda i: (i, 0)),
                  pl.BlockSpec(memory_space=pl.ANY)],
        out_specs=pl.BlockSpec(memory_space=pl.ANY),
        out_shape=jax.ShapeDtypeStruct((MOE_ROWS, HALF_D), jnp.uint32),
        scratch_shapes=[pltpu.SemaphoreType.DMA(())],
        input_output_aliases={2: 0},
        compiler_params=_cparams(1), name="moe_dispatch",
    )(pos3, hp, jnp.zeros((MOE_ROWS, HALF_D), jnp.uint32))


COMBINE_TM = 128


def _combine_kernel(pos_ref, pos_next_ref, h_ref, gates_ref, mod_ref, y_hbm, o_ref, buf, sem,
                    *, g_row):
    i = pl.program_id(0)
    slot = i % 2
    n_rows = TOP_K * COMBINE_TM

    def issue(idx_ref, s):
        def body(r, carry):
            pltpu.make_async_copy(y_hbm.at[pl.ds(idx_ref[0, r], 1)],
                                  buf.at[s, pl.ds(r, 1)], sem.at[s]).start()
            return carry
        lax.fori_loop(0, n_rows, body, 0, unroll=8)

    @pl.when(i == 0)
    def _():
        issue(pos_ref, 0)

    @pl.when(i + 1 < pl.num_programs(0))
    def _():
        issue(pos_next_ref, 1 - slot)

    pltpu.make_async_copy(y_hbm.at[pl.ds(0, n_rows)], buf.at[slot], sem.at[slot]).wait()
    acc = gates_ref[:, 0:1] * buf[slot, 0:COMBINE_TM, :]
    for k in range(1, TOP_K):
        acc = acc + gates_ref[:, k:k + 1] * buf[slot, k * COMBINE_TM:(k + 1) * COMBINE_TM, :]
    o_ref[...] = h_ref[...] + mod_ref[g_row:g_row + 1, :] * acc


def _moe_combine(h, y_rows, pos, gates_pad, mod, g_row):
    t = h.shape[0]
    nblk = t // COMBINE_TM
    pos3 = pos.reshape(nblk, COMBINE_TM, TOP_K).transpose(0, 2, 1).reshape(nblk, 1, TOP_K * COMBINE_TM)
    smem_blk = lambda f: pl.BlockSpec((None, 1, TOP_K * COMBINE_TM), f, memory_space=pltpu.SMEM)
    return pl.pallas_call(
        functools.partial(_combine_kernel, g_row=g_row),
        grid=(nblk,),
        in_specs=[smem_blk(lambda i: (i, 0, 0)),
                  smem_blk(lambda i: (jnp.minimum(i + 1, nblk - 1), 0, 0)),
                  pl.BlockSpec((COMBINE_TM, D_MODEL), lambda i: (i, 0)),
                  pl.BlockSpec((COMBINE_TM, V7X_LANES), lambda i: (i, 0)),
                  pl.BlockSpec((None, MOD_ROWS, D_MODEL),
                               lambda i: (_cond_of_row(i * COMBINE_TM), 0, 0)),
                  pl.BlockSpec(memory_space=pl.ANY)],
        out_specs=pl.BlockSpec((COMBINE_TM, D_MODEL), lambda i: (i, 0)),
        out_shape=jax.ShapeDtypeStruct((t, D_MODEL), jnp.float32),
        scratch_shapes=[pltpu.VMEM((2, TOP_K * COMBINE_TM, D_MODEL), jnp.float32),
                        pltpu.SemaphoreType.DMA((2,))],
        compiler_params=_cparams(1), name="moe_combine",
    )(pos3, pos3, h, gates_pad, mod, y_rows)


ROUTE_TM = 256


def _route_kernel(lg_ref, idx_ref, gate_ref, rank_ref, cnt_ref, carry):
    f32 = jnp.float32
    i = pl.program_id(0)

    @pl.when(i == 0)
    def _():
        carry[...] = jnp.zeros_like(carry)

    lg = lg_ref[...]
    lane = lax.broadcasted_iota(jnp.int32, lg.shape, 1)
    lane_f = lane.astype(f32)
    vals, ids, hots = [], [], []
    for _ in range(TOP_K):
        m = jnp.max(lg, axis=1, keepdims=True)
        first = jnp.min(jnp.where(lg == m, lane_f, float(V7X_LANES)), axis=1, keepdims=True)
        hot = lane_f == first
        vals.append(m)
        ids.append(first)
        hots.append(hot)
        lg = jnp.where(hot, -jnp.inf, lg)
    exps = [jnp.exp(v - vals[0]) for v in vals]
    denom = exps[0]
    for e in exps[1:]:
        denom = denom + e
    chosen = jnp.zeros(lg.shape, f32)
    for hot in hots:
        chosen = chosen + jnp.where(hot, 1.0, 0.0)
    ri = lax.broadcasted_iota(jnp.int32, (ROUTE_TM, ROUTE_TM), 0)
    ci = lax.broadcasted_iota(jnp.int32, (ROUTE_TM, ROUTE_TM), 1)
    tri = jnp.where(ci <= ri, 1.0, 0.0).astype(jnp.bfloat16)
    upto = jnp.dot(tri, chosen.astype(jnp.bfloat16), preferred_element_type=f32)
    before = carry[...] + upto - chosen
    idx_out = jnp.zeros(lg.shape, f32)
    gate_out = jnp.zeros(lg.shape, f32)
    rank_out = jnp.zeros(lg.shape, f32)
    for k in range(TOP_K):
        rank_k = jnp.sum(jnp.where(hots[k], before, 0.0), axis=1, keepdims=True)
        idx_out = jnp.where(lane == k, ids[k], idx_out)
        gate_out = jnp.where(lane == k, exps[k] / denom, gate_out)
        rank_out = jnp.where(lane == k, rank_k, rank_out)
    idx_ref[...] = idx_out.astype(jnp.int32)
    gate_ref[...] = gate_out
    rank_ref[...] = rank_out.astype(jnp.int32)
    carry[...] = carry[...] + upto[ROUTE_TM - 1:ROUTE_TM, :]
    cnt_ref[...] = carry[...]


def _route(logits):
    t = logits.shape[0]
    blk = pl.BlockSpec((ROUTE_TM, V7X_LANES), lambda i: (i, 0))
    row = pl.BlockSpec((1, V7X_LANES), lambda i: (0, 0))
    sds = lambda dt: jax.ShapeDtypeStruct((t, V7X_LANES), dt)
    return pl.pallas_call(
        _route_kernel,
        grid=(t // ROUTE_TM,),
        in_specs=[blk],
        out_specs=[blk, blk, blk, row],
        out_shape=[sds(jnp.int32), sds(jnp.float32), sds(jnp.int32),
                   jax.ShapeDtypeStruct((1, V7X_LANES), jnp.float32)],
        scratch_shapes=[pltpu.VMEM((1, V7X_LANES), jnp.float32)],
        compiler_params=_cparams(1), name="moe_route",
    )(logits)


def _moe(h, hn, logits, w_gu, b_gu, w_dn, b_dn, mod):
    t = hn.shape[0]
    idx, gates_pad, rank, counts = _route(logits)
    top_idx = idx[:, :TOP_K]
    counts = counts[0, :N_EXPERTS].astype(jnp.int32)
    padded = (counts + MOE_TM - 1) // MOE_TM * MOE_TM
    padded_end = jnp.cumsum(padded)
    padded_start = padded_end - padded
    pos = padded_start[top_idx] + rank[:, :TOP_K]
    block_e = jnp.minimum(
        jnp.searchsorted(padded_end, jnp.arange(MOE_BLOCKS) * MOE_TM, side='right'),
        N_EXPERTS - 1).astype(jnp.int32)
    n_valid = (padded_end[-1] // MOE_TM).astype(jnp.int32).reshape(1)
    rows_used = (padded_start + counts)[block_e] - jnp.arange(MOE_BLOCKS) * MOE_TM
    block_half = (rows_used <= MOE_TM // 2).astype(jnp.int32)

    x_rows = _moe_dispatch(hn, pos)
    act = _grouped_matmul(x_rows, w_gu, b_gu, block_e, block_half, n_valid, swiglu=True, tn=MOE_TN,
                          out_dtype=jnp.bfloat16, name="moe_gate_up")
    y_rows = _grouped_matmul(act, w_dn, b_dn, block_e, block_half, n_valid, swiglu=False, tn=MOE_TN,
                             out_dtype=jnp.float32, name="moe_down")
    return _moe_combine(h, y_rows, pos, gates_pad, mod, 5)


CONV_TM = 256
N_CHUNKS = T_ALL // SSD_CHUNK
CTX_CHUNKS = T_CTX // SSD_CHUNK
N_SEQS = BATCH + DEC_BATCH


def _seq_len_of_row(row):
    return jnp.where(row < T_CTX, SEQ, DEC_SEQ)


def _seq_pos_of_row(row):
    return jnp.where(row < T_CTX, row % SEQ, (row - T_CTX) % DEC_SEQ)


def _seq_of_chunk(n):
    return jnp.where(n < CTX_CHUNKS, n // (SEQ // SSD_CHUNK),
                     BATCH + (n - CTX_CHUNKS) // (DEC_SEQ // SSD_CHUNK))


def _dwconv_kernel(x_ref, prev_ref, next_ref, w_ref, b_ref, o_ref, *, width, silu):
    row0 = pl.program_id(1) * CONV_TM
    pos0 = _seq_pos_of_row(row0)
    has_prev = pos0 != 0
    has_next = pos0 + CONV_TM != _seq_len_of_row(row0)
    x = x_ref[...]
    rid = lax.broadcasted_iota(jnp.int32, x.shape, 0)
    prev = jnp.where(has_prev, prev_ref[7:8, :], 0.0)
    nxt0 = jnp.where(has_next, next_ref[0:1, :], 0.0)
    acc = w_ref[1:2, :] * x + b_ref[...]
    xm1 = jnp.where(rid == 0, prev, pltpu.roll(x, 1, 0))
    acc = acc + w_ref[0:1, :] * xm1
    xp1 = jnp.where(rid == CONV_TM - 1, nxt0, pltpu.roll(x, CONV_TM - 1, 0))
    acc = acc + w_ref[2:3, :] * xp1
    if width == 4:
        nxt1 = jnp.where(has_next, next_ref[1:2, :], 0.0)
        xp2 = pltpu.roll(x, CONV_TM - 2, 0)
        xp2 = jnp.where(rid == CONV_TM - 2, nxt0, jnp.where(rid == CONV_TM - 1, nxt1, xp2))
        acc = acc + w_ref[3:4, :] * xp2
    if silu:
        acc = acc * jax.nn.sigmoid(acc)
    o_ref[...] = acc


def _dwconv(x, w, b, *, in_col_blocks, tc, silu, name):
    width = w.shape[0]
    nblk = len(in_col_blocks)
    assert w.shape[1] == nblk * tc and width in (3, 4)
    cols = np.asarray(in_col_blocks, np.int32)
    if np.array_equal(cols, np.arange(nblk)):
        col = lambda j: j
    else:
        assert nblk == 3 and cols[0] == 0 and cols[1] == 1
        col = lambda j: jnp.where(j < 2, j, int(cols[2]))
    w8 = jnp.pad(w, ((0, 8 - width), (0, 0)))
    sub = CONV_TM // 8
    last8 = T_ALL // 8 - 1
    return pl.pallas_call(
        functools.partial(_dwconv_kernel, width=width, silu=silu),
        grid=(nblk, T_ALL // CONV_TM),
        in_specs=[pl.BlockSpec((CONV_TM, tc), lambda j, i: (i, col(j))),
                  pl.BlockSpec((8, tc), lambda j, i: (jnp.maximum(i * sub - 1, 0), col(j))),
                  pl.BlockSpec((8, tc), lambda j, i: (jnp.minimum((i + 1) * sub, last8), col(j))),
                  pl.BlockSpec((8, tc), lambda j, i: (0, j)),
                  pl.BlockSpec((1, tc), lambda j, i: (0, j))],
        out_specs=pl.BlockSpec((CONV_TM, tc), lambda j, i: (i, j)),
        out_shape=jax.ShapeDtypeStruct((T_ALL, nblk * tc), jnp.float32),
        compiler_params=_cparams(2), name=name,
    )(x, x, x, w8, b.reshape(1, -1))


def _softplus(x):
    return jnp.maximum(x, 0.0) + jnp.log(1.0 + jnp.exp(-jnp.abs(x)))


def _ssd_kernel(*refs, reverse):
    if reverse:
        (xbc_ref, zdt_ref, bias_ref, alog_ref, h0_ref, yf_ref, z_ref, dskip_ref, gn_ref,
         y_ref, st_ref, s_scr, y_scr) = refs
    else:
        xbc_ref, zdt_ref, bias_ref, alog_ref, h0_ref, y_ref, st_ref, s_scr = refs
    f32, bf = jnp.float32, jnp.bfloat16
    q = SSD_CHUNK
    step = pl.program_id(0)
    n = (N_CHUNKS - 1 - step) if reverse else step
    row0 = n * q
    pos0 = _seq_pos_of_row(row0)
    seq_len = _seq_len_of_row(row0)
    starts_seq = (pos0 + q == seq_len) if reverse else (pos0 == 0)
    ends_seq = (pos0 == 0) if reverse else (pos0 + q == seq_len)

    @pl.when(starts_seq)
    def _():
        s_scr[...] = jnp.where(row0 < T_CTX, 0.0, h0_ref[...])

    ri = lax.broadcasted_iota(jnp.int32, (q, q), 0)
    ci = lax.broadcasted_iota(jnp.int32, (q, q), 1)
    mask = (ci >= ri) if reverse else (ci <= ri)
    lane_lo = ci < SSD_HEAD_DIM
    row_lo = ri < SSD_HEAD_DIM

    dt = _softplus(zdt_ref[...] + bias_ref[...])
    da = dt * (-jnp.exp(alog_ref[...]))
    ones = jnp.where(mask, 1.0, 0.0).astype(bf)
    d1 = da.astype(bf)
    r1 = da - d1.astype(f32)
    d2 = r1.astype(bf)
    d3 = (r1 - d2.astype(f32)).astype(bf)
    cs = (jnp.dot(ones, d1, preferred_element_type=f32)
          + jnp.dot(ones, d2, preferred_element_type=f32)
          + jnp.dot(ones, d3, preferred_element_type=f32))
    cs_t = cs.T
    tot = cs[0:1, :] if reverse else cs[q - 1:q, :]
    exp_cs = jnp.exp(cs)
    d_end = jnp.exp(tot - cs)
    exp_tot = jnp.exp(tot)

    def pair_cols(v, p):
        return jnp.where(lane_lo, v[:, 2 * p:2 * p + 1], v[:, 2 * p + 1:2 * p + 2])

    dn_t = (((1,), (1,)), ((), ()))
    for g in range(SSD_GROUPS):
        b_g = xbc_ref[:, SSD_WIDTH + g * SSD_STATE:SSD_WIDTH + (g + 1) * SSD_STATE].astype(bf)
        c_off = SSD_WIDTH + SSD_GROUPS * SSD_STATE
        c_g = xbc_ref[:, c_off + g * SSD_STATE:c_off + (g + 1) * SSD_STATE].astype(bf)
        cb = lax.dot_general(c_g, b_g, dn_t, preferred_element_type=f32)
        pairs_per_group = SSD_HEADS // SSD_GROUPS // 2
        for p in range(g * pairs_per_group, (g + 1) * pairs_per_group):
            lo, hi = p * 2 * SSD_HEAD_DIM, (p + 1) * 2 * SSD_HEAD_DIM
            xs = xbc_ref[:, lo:hi]
            xdt = xs * pair_cols(dt, p)
            y = jnp.zeros((q, 2 * SSD_HEAD_DIM), f32)
            for h, keep in ((2 * p, lane_lo), (2 * p + 1, ~lane_lo)):
                seg = cs[:, h:h + 1] - cs_t[h:h + 1, :]
                w = jnp.where(mask, jnp.exp(jnp.minimum(seg, 0.0)), 0.0) * cb
                y = y + jnp.dot(w.astype(bf), jnp.where(keep, xdt, 0.0).astype(bf),
                                preferred_element_type=f32)
            s_pair = s_scr[lo:hi, :]
            y_off = lax.dot_general(c_g, s_pair.astype(bf), dn_t, preferred_element_type=f32)
            y = y + y_off * pair_cols(exp_cs, p)
            st = jnp.dot((xdt * pair_cols(d_end, p)).T.astype(bf), b_g,
                         preferred_element_type=f32)
            keep_s = jnp.where(row_lo, exp_tot[:, 2 * p:2 * p + 1], exp_tot[:, 2 * p + 1:2 * p + 2])
            s_scr[lo:hi, :] = s_pair * keep_s + st
            if reverse:
                y_scr[:, lo:hi] = y + yf_ref[:, lo:hi] + dskip_ref[:, lo:hi] * xs
            else:
                y_ref[:, lo:hi] = y

    if reverse:
        zz = z_ref[...]
        yy = y_scr[...] * (zz * jax.nn.sigmoid(zz))
        gw = SSD_WIDTH // SSD_GROUPS
        for g in range(SSD_GROUPS):
            y_g = yy[:, g * gw:(g + 1) * gw]
            y_g = y_g * lax.rsqrt(jnp.mean(y_g * y_g, axis=-1, keepdims=True) + NORM_EPS)
            y_ref[:, g * gw:(g + 1) * gw] = (y_g * gn_ref[:, g * gw:(g + 1) * gw]).astype(y_ref.dtype)

    @pl.when(ends_seq)
    def _():
        st_ref[...] = s_scr[...]


def _ssd(xbc, z_dt, dt_bias, a_log, h0, *, reverse, finish=None):
    q = SSD_CHUNK
    d = 1 if reverse else 0
    chunk = (lambda s: N_CHUNKS - 1 - s) if reverse else (lambda s: s)
    pad_row = lambda v: jnp.pad(v, (0, V7X_LANES - SSD_HEADS)).reshape(1, V7X_LANES)
    in_specs = [pl.BlockSpec((q, SSD_CONV_CH), lambda s: (chunk(s), 0)),
                pl.BlockSpec((q, V7X_LANES), lambda s: (chunk(s), d)),
                pl.BlockSpec((1, V7X_LANES), lambda s: (0, 0)),
                pl.BlockSpec((1, V7X_LANES), lambda s: (0, 0)),
                pl.BlockSpec((None, SSD_WIDTH, SSD_STATE),
                             lambda s: (jnp.maximum(_seq_of_chunk(chunk(s)) - BATCH, 0), 0, 0))]
    args = [xbc, z_dt, pad_row(dt_bias), pad_row(a_log), h0]
    scratch = [pltpu.VMEM((SSD_WIDTH, SSD_STATE), jnp.float32)]
    y_dtype = jnp.float32
    if reverse:
        y_fwd, z_ssd, dskip_row, gn_row = finish
        in_specs += [pl.BlockSpec((q, SSD_WIDTH), lambda s: (chunk(s), 0)),
                     pl.BlockSpec((q, SSD_WIDTH), lambda s: (chunk(s), 1)),
                     pl.BlockSpec((1, SSD_WIDTH), lambda s: (0, 0)),
                     pl.BlockSpec((1, SSD_WIDTH), lambda s: (0, 0))]
        args += [y_fwd, z_ssd, dskip_row, gn_row]
        scratch.append(pltpu.VMEM((q, SSD_WIDTH), jnp.float32))
        y_dtype = jnp.bfloat16
    return pl.pallas_call(
        functools.partial(_ssd_kernel, reverse=reverse),
        grid=(N_CHUNKS,),
        in_specs=in_specs,
        out_specs=[pl.BlockSpec((q, SSD_WIDTH), lambda s: (chunk(s), 0)),
                   pl.BlockSpec((None, SSD_WIDTH, SSD_STATE),
                                lambda s: (_seq_of_chunk(chunk(s)), 0, 0))],
        out_shape=[jax.ShapeDtypeStruct((T_ALL, SSD_WIDTH), y_dtype),
                   jax.ShapeDtypeStruct((N_SEQS, SSD_WIDTH, SSD_STATE), jnp.float32)],
        scratch_shapes=scratch,
        compiler_params=_cparams(1), name="ssd_bwd" if reverse else "ssd_fwd",
    )(*args)


ATT_TQ = 256
HEAD_W = 2 * DA_HEAD_DIM


def _rope_tables():
    t = jnp.arange(DEC_SEQ)
    pos = jnp.stack([t // GRID_W, t % GRID_W], axis=-1).astype(jnp.float32)
    inv = 1.0 / (ROPE_BASE ** (jnp.arange(ROPE_FREQS, dtype=jnp.float32) / ROPE_FREQS))
    ang = pos[:, :, None] * inv
    cos, sin = jnp.cos(ang), jnp.sin(ang)
    zero = jnp.zeros_like(sin)
    lanes = lambda first, second: jnp.tile(
        jnp.stack([first, second], axis=2).reshape(DEC_SEQ, DA_HEAD_DIM), (1, 2))
    return lanes(cos, cos), lanes(-sin, zero), lanes(zero, sin)


def _attn_kernel(lam_ref, *refs, has_ctx):
    if has_ctx:
        (q_ref, k_ref, v_ref, kc_ref, vc_ref, cq, saq, sbq, ck, sak, sbk, sw_ref, o_ref) = refs
    else:
        q_ref, k_ref, v_ref, sw_ref, o_ref = refs
    f32, bf = jnp.float32, jnp.bfloat16
    half = ROPE_FREQS

    def rope(x, c, sa, sb):
        return (x * c[...] + pltpu.roll(x, HEAD_W - half, 1) * sa[...]
                + pltpu.roll(x, half, 1) * sb[...])

    q, k = q_ref[...], k_ref[...]
    if has_ctx:
        q, k = rope(q, cq, saq, sbq), rope(k, ck, sak, sbk)
    lane = lax.broadcasted_iota(jnp.int32, q.shape, 1)
    kb, vb = k.astype(bf), v_ref[...].astype(bf)
    if has_ctx:
        kb = jnp.concatenate([kc_ref[...].astype(bf), kb], axis=0)
        vb = jnp.concatenate([vc_ref[...].astype(bf), vb], axis=0)
    scale = DA_HEAD_DIM ** -0.5

    def probs(qm):
        s = lax.dot_general(qm.astype(bf), kb, (((1,), (1,)), ((), ())),
                            preferred_element_type=f32) * scale
        e = jnp.exp(s - jnp.max(s, axis=-1, keepdims=True))
        return e / jnp.sum(e, axis=-1, keepdims=True)

    w = probs(jnp.where(lane < DA_HEAD_DIM, q, 0.0)) - lam_ref[0] * probs(
        jnp.where(lane >= DA_HEAD_DIM, q, 0.0))
    o = jnp.dot(w.astype(bf), vb, preferred_element_type=f32)
    o = o * lax.rsqrt(jnp.mean(o * o, axis=-1, keepdims=True) + NORM_EPS) * sw_ref[...]
    o_ref[...] = (o * (1.0 - DA_LAMBDA_INIT)).astype(o_ref.dtype)


def _diff_attention(z_qkv, cache_k, cache_v, lam, subln_w):
    sw = subln_w.reshape(1, HEAD_W)
    o_ctx = pl.pallas_call(
        functools.partial(_attn_kernel, has_ctx=False),
        grid_spec=pltpu.PrefetchScalarGridSpec(
            num_scalar_prefetch=1, grid=(BATCH, DA_HEADS),
            in_specs=[pl.BlockSpec((SEQ, HEAD_W), lambda b, h, lam: (b, h)),
                      pl.BlockSpec((SEQ, HEAD_W), lambda b, h, lam: (b, DA_HEADS + h)),
                      pl.BlockSpec((SEQ, HEAD_W), lambda b, h, lam: (b, 2 * DA_HEADS + h)),
                      pl.BlockSpec((1, HEAD_W), lambda b, h, lam: (0, 0))],
            out_specs=pl.BlockSpec((SEQ, HEAD_W), lambda b, h, lam: (b, h))),
        out_shape=jax.ShapeDtypeStruct((T_CTX, DA_WIDTH), jnp.bfloat16),
        compiler_params=_cparams(2), name="attn_ctx",
    )(lam, z_qkv, z_qkv, z_qkv, sw)

    tabs = _rope_tables()
    nq = DEC_SEQ // ATT_TQ
    q_blk0 = T_CTX // ATT_TQ
    kv_blk0 = T_CTX // DEC_SEQ
    kc = cache_k.reshape(DEC_BATCH * PAST_LEN, DA_WIDTH)
    vc = cache_v.reshape(DEC_BATCH * PAST_LEN, DA_WIDTH)
    q_row = lambda b, h, i, lam: (q_blk0 + b * nq + i, h)
    tab_q = pl.BlockSpec((ATT_TQ, HEAD_W), lambda b, h, i, lam: (i, 0))
    tab_k = pl.BlockSpec((DEC_SEQ, HEAD_W), lambda b, h, i, lam: (0, 0))
    o_dec = pl.pallas_call(
        functools.partial(_attn_kernel, has_ctx=True),
        grid_spec=pltpu.PrefetchScalarGridSpec(
            num_scalar_prefetch=1, grid=(DEC_BATCH, DA_HEADS, nq),
            in_specs=[pl.BlockSpec((ATT_TQ, HEAD_W), q_row),
                      pl.BlockSpec((DEC_SEQ, HEAD_W), lambda b, h, i, lam: (kv_blk0 + b, DA_HEADS + h)),
                      pl.BlockSpec((DEC_SEQ, HEAD_W), lambda b, h, i, lam: (kv_blk0 + b, 2 * DA_HEADS + h)),
                      pl.BlockSpec((PAST_LEN, HEAD_W), lambda b, h, i, lam: (b, h)),
                      pl.BlockSpec((PAST_LEN, HEAD_W), lambda b, h, i, lam: (b, h)),
                      tab_q, tab_q, tab_q, tab_k, tab_k, tab_k,
                      pl.BlockSpec((1, HEAD_W), lambda b, h, i, lam: (0, 0))],
            out_specs=pl.BlockSpec((ATT_TQ, HEAD_W), lambda b, h, i, lam: (b * nq + i, h))),
        out_shape=jax.ShapeDtypeStruct((T_DEC, DA_WIDTH), jnp.bfloat16),
        compiler_params=_cparams(3), name="attn_dec",
    )(lam, z_qkv, z_qkv, z_qkv, kc, vc, tabs[0], tabs[1], tabs[2], tabs[0], tabs[1], tabs[2], sw)
    return jnp.concatenate([o_ctx, o_dec], axis=0)


def _dft_angle(L, s_offset):
    k = jnp.arange(L, dtype=jnp.int32)[:, None]
    s = jnp.arange(L, dtype=jnp.int32)[None, :] + s_offset
    m = ((2 * k + 1) * s) % (4 * L)
    return m.astype(jnp.float32) * (math.pi / (2 * L))


def _hyena_tables(L):
    bf = jnp.bfloat16
    th = _dft_angle(L, 0)
    c, s = jnp.cos(th), jnp.sin(th)
    fwd = jnp.concatenate([c, -s], axis=0).astype(bf)
    inv = (jnp.concatenate([c.T, -s.T], axis=1) * (1.0 / L)).astype(bf)
    th_f = _dft_angle(L, -(L // 2))
    filt_fwd = jnp.concatenate([jnp.cos(th_f), -jnp.sin(th_f)], axis=0).astype(bf)
    return fwd, inv, filt_fwd


def _hyena_filter_kernel(band_ref, w1t_ref, w1c_ref, w1s_ref, b1_ref, w2_ref, b2_ref, w3_ref,
                         b3_ref, fr_ref, w4_ref, dl_ref, o_ref, h_scr, *, L):
    t = lax.broadcasted_iota(jnp.int32, (L, 1), 0).astype(jnp.float32)

    @pl.when(pl.program_id(0) == 0)
    def _():
        ang = ((2.0 * math.pi / L) * t) * band_ref[...]
        fr = fr_ref[...]
        z = ((t / L) * w1t_ref[...] + _dot3(jnp.cos(ang), w1c_ref[...])
             + _dot3(jnp.sin(ang), w1s_ref[...]))
        h = jnp.sin(fr * (z + b1_ref[...]))
        h = jnp.sin(fr * (_dot3(h, w2_ref[...]) + b2_ref[...]))
        h_scr[...] = jnp.sin(fr * (_dot3(h, w3_ref[...]) + b3_ref[...]))

    dist = jnp.abs(t - (L // 2)) / L
    o_ref[...] = _dot3(h_scr[...], w4_ref[...]) * jnp.exp(-dist * dl_ref[...])


def _hyena_filter(L, w1, b1, w2, b2, w3, b3, w4, freq):
    hid = w2.shape[0]
    tn = 512
    bands = jnp.linspace(1e-4, HY_BANDS - 1, HY_BANDS, dtype=jnp.float32)
    lane_pad = V7X_LANES - HY_BANDS
    band_row = jnp.pad(bands, (0, lane_pad)).reshape(1, V7X_LANES)
    pad_rows = lambda w: jnp.pad(w, ((0, lane_pad), (0, 0)))
    deltas = jnp.abs(jnp.linspace(math.log(HY_TARGET) / HY_SLOW_DECAY,
                                  math.log(HY_TARGET) / HY_FAST_DECAY, HY_WIDTH,
                                  dtype=jnp.float32)).reshape(1, HY_WIDTH)
    full = lambda shape: pl.BlockSpec(shape, lambda j: (0,) * len(shape))
    return pl.pallas_call(
        functools.partial(_hyena_filter_kernel, L=L),
        grid=(HY_WIDTH // tn,),
        in_specs=[full((1, V7X_LANES)), full((1, hid)), full((V7X_LANES, hid)),
                  full((V7X_LANES, hid)), full((1, hid)), full((hid, hid)), full((1, hid)),
                  full((hid, hid)), full((1, hid)), full((1, hid)),
                  pl.BlockSpec((hid, tn), lambda j: (0, j)),
                  pl.BlockSpec((1, tn), lambda j: (0, j))],
        out_specs=pl.BlockSpec((L, tn), lambda j: (0, j)),
        out_shape=jax.ShapeDtypeStruct((L, HY_WIDTH), jnp.float32),
        scratch_shapes=[pltpu.VMEM((L, hid), jnp.float32)],
        compiler_params=_cparams(1), name=f"hyena_filter_{L}",
    )(band_row, w1[0:1], pad_rows(w1[1:1 + HY_BANDS]), pad_rows(w1[1 + HY_BANDS:]),
      b1.reshape(1, hid), w2, b2.reshape(1, hid), w3, b3.reshape(1, hid), freq.reshape(1, hid),
      w4, deltas)


def _hyena_conv_kernel(*refs):
    (x0_ref, x1_ref, v_ref, fr_ref, fi_ref, hr_ref, hi_ref, ir_ref, ii_ref, fb_ref) = refs[:10]
    o_ref, vbf, acc = refs[-3:]
    f32, bf = jnp.float32, jnp.bfloat16
    kb = pl.program_id(2)

    @pl.when(kb == 0)
    def _():
        vbf[...] = (v_ref[...] * x1_ref[...]).astype(bf)
        acc[...] = jnp.zeros_like(acc)

    vr = jnp.dot(fr_ref[...], vbf[...], preferred_element_type=f32)
    vi = jnp.dot(fi_ref[...], vbf[...], preferred_element_type=f32)
    hr, hi = hr_ref[...], hi_ref[...]
    yr = (vr * hr - vi * hi).astype(bf)
    yi = (vr * hi + vi * hr).astype(bf)
    acc[...] += (jnp.dot(ir_ref[...], yr, preferred_element_type=f32)
                 + jnp.dot(ii_ref[...], yi, preferred_element_type=f32))

    @pl.when(kb == pl.num_programs(2) - 1)
    def _():
        vv = v_ref[...] * x1_ref[...]
        o_ref[...] = ((acc[...] + vv * fb_ref[...]) * x0_ref[...]).astype(o_ref.dtype)


def _hyena_conv(u, filt_spec, tables, f_bias, *, L, row_blk0, n_seq, tn, kblk):
    fwd, inv, _ = tables
    nj, nk = HY_WIDTH // tn, L // kblk
    u_spec = lambda part: pl.BlockSpec((L, tn), lambda b, j, k: (row_blk0 + b, part * nj + j))
    in_specs = [u_spec(0), u_spec(1), u_spec(2),
                pl.BlockSpec((kblk, L), lambda b, j, k: (k, 0)),
                pl.BlockSpec((kblk, L), lambda b, j, k: (nk + k, 0)),
                pl.BlockSpec((kblk, tn), lambda b, j, k: (k, j)),
                pl.BlockSpec((kblk, tn), lambda b, j, k: (nk + k, j)),
                pl.BlockSpec((L, kblk), lambda b, j, k: (0, k)),
                pl.BlockSpec((L, kblk), lambda b, j, k: (0, nk + k)),
                pl.BlockSpec((1, tn), lambda b, j, k: (0, j))]
    args = [u, u, u, fwd, fwd, filt_spec, filt_spec, inv, inv, f_bias.reshape(1, HY_WIDTH)]
    return pl.pallas_call(
        _hyena_conv_kernel,
        grid=(n_seq, nj, nk),
        in_specs=in_specs,
        out_specs=pl.BlockSpec((L, tn), lambda b, j, k: (b, j)),
        out_shape=jax.ShapeDtypeStruct((n_seq * L, HY_WIDTH), jnp.bfloat16),
        scratch_shapes=[pltpu.VMEM((L, tn), jnp.bfloat16), pltpu.VMEM((L, tn), jnp.float32)],
        compiler_params=_cparams(3), name=f"hyena_conv_{L}",
    )(*args)


def _hyena_mix(u, filt_p, f_bias):
    outs = []
    for L, row_blk0, n_seq, tn, kblk in ((SEQ, 0, BATCH, 1024, SEQ),
                                        (DEC_SEQ, T_CTX // DEC_SEQ, DEC_BATCH, 256, 512)):
        tables = _hyena_tables(L)
        filt = _hyena_filter(L, *filt_p)
        spec = _matmul(tables[2], filt, HY_WIDTH, tm=min(1024, 2 * L), name=f"hyena_spec_{L}")
        outs.append(_hyena_conv(u, spec, tables, f_bias, L=L, row_blk0=row_blk0, n_seq=n_seq,
                                tn=tn, kblk=kblk))
    return jnp.concatenate(outs, axis=0)


def kernel(x_prompt, x_sample, c, state_l0_ssm_fwd, state_l0_ssm_bwd, cache_l0_k, cache_l0_v, c_ctx, l0_norm_mix, l0_norm_ffn, l0_ada_w, l0_ada_b, l0_w_in, l0_conv_w, l0_conv_b, l0_a_log_fwd, l0_a_log_bwd, l0_dt_bias_fwd, l0_dt_bias_bwd, l0_d_skip, l0_gnorm_w, l0_lambda_q1, l0_lambda_k1, l0_lambda_q2, l0_lambda_k2, l0_subln_w, l0_w_out, l0_router_w, l0_router_b, l0_w_gate_up, l0_b_gate_up, l0_w_down, l0_b_down, l1_norm_mix, l1_norm_ffn, l1_ada_w, l1_ada_b, l1_w_in, l1_b_in, l1_short_w, l1_short_b, l1_filt_w1, l1_filt_b1, l1_filt_w2, l1_filt_b2, l1_filt_w3, l1_filt_b3, l1_filt_w4, l1_filt_freq, l1_filt_bias, l1_w_out, l1_b_out, l1_router_w, l1_router_b, l1_w_gate_up, l1_b_gate_up, l1_w_down, l1_b_down, final_norm):
    h = jnp.concatenate([x_prompt.reshape(T_CTX, D_MODEL), x_sample.reshape(T_DEC, D_MODEL)], axis=0)
    cond8 = jnp.concatenate([c_ctx[None, :], c,
                             jnp.zeros((MOD_ROWS - N_COND, D_MODEL), jnp.float32)], axis=0)

    def router_args(rw, rb):
        pad = V7X_LANES - N_EXPERTS
        return (jnp.pad(rw, ((0, 0), (0, pad))),
                jnp.pad(rb, (0, pad), constant_values=-1e30).reshape(1, V7X_LANES))

    def ffn(h, mod, norm_g, rw, rb, w_gu, b_gu, w_dn, b_dn):
        hn, logits = _normmod(h, norm_g, mod, 3, router=router_args(rw, rb))
        return _moe(h, hn, logits, w_gu, b_gu, w_dn, b_dn, mod)

    mod0 = _adaln(cond8, l0_ada_w, l0_ada_b)
    hn = _normmod(h, l0_norm_mix, mod0, 0)
    c_dt = sum(L0_SPLITS[:4])
    c_q = c_dt + 2 * SSD_HEADS
    z_ssd = _matmul(hn, l0_w_in, c_dt, tn=512, name="l0_in_ssd")
    lane_pad = jnp.zeros((D_MODEL, V7X_LANES - SSD_HEADS), jnp.float32)
    w_dt = jnp.concatenate([l0_w_in[:, c_dt:c_dt + SSD_HEADS], lane_pad,
                            l0_w_in[:, c_dt + SSD_HEADS:c_q], lane_pad], axis=1)
    z_dt = _matmul(hn, w_dt, 2 * V7X_LANES, tn=2 * V7X_LANES, name="l0_in_dt")
    z_qkv = _matmul(hn, l0_w_in[:, c_q:], 3 * DA_WIDTH, name="l0_in_qkv")
    xbc = _dwconv(z_ssd, l0_conv_w, l0_conv_b, in_col_blocks=(0, 1, 4), tc=512, silu=True,
                  name="ssd_conv")
    h0_shape = (DEC_BATCH, SSD_WIDTH, SSD_STATE)
    y_fwd, st_fwd = _ssd(xbc, z_dt, l0_dt_bias_fwd, l0_a_log_fwd,
                         state_l0_ssm_fwd.reshape(h0_shape), reverse=False)
    y_ssd, st_bwd = _ssd(xbc, z_dt, l0_dt_bias_bwd, l0_a_log_bwd,
                         state_l0_ssm_bwd.reshape(h0_shape), reverse=True,
                         finish=(y_fwd, z_ssd, jnp.repeat(l0_d_skip, SSD_HEAD_DIM).reshape(1, -1),
                                 l0_gnorm_w.reshape(1, -1)))
    lam = (jnp.exp(jnp.sum(l0_lambda_q1 * l0_lambda_k1))
           - jnp.exp(jnp.sum(l0_lambda_q2 * l0_lambda_k2)) + DA_LAMBDA_INIT).reshape(1)
    o_att = _diff_attention(z_qkv, cache_l0_k, cache_l0_v, lam, l0_subln_w)
    h = _matmul(y_ssd, l0_w_out, D_MODEL, a2=o_att, gated=(h, mod0, 2), name="l0_out")
    h = ffn(h, mod0, l0_norm_ffn, l0_router_w, l0_router_b, l0_w_gate_up, l0_b_gate_up,
            l0_w_down, l0_b_down)
    st_shape = (BATCH, SSD_HEADS, SSD_HEAD_DIM, SSD_STATE)
    kv_shape = (BATCH, SEQ, DA_HEADS, HEAD_W)
    produced = (st_fwd[:BATCH].reshape(st_shape), st_bwd[:BATCH].reshape(st_shape),
                z_qkv[:T_CTX, DA_WIDTH:2 * DA_WIDTH].reshape(kv_shape),
                z_qkv[:T_CTX, 2 * DA_WIDTH:].reshape(kv_shape))

    mod1 = _adaln(cond8, l1_ada_w, l1_ada_b)
    hn = _normmod(h, l1_norm_mix, mod1, 0)
    u = _matmul(hn, l1_w_in, 3 * HY_WIDTH, bias=l1_b_in, name="l1_in")
    u = _dwconv(u, l1_short_w, l1_short_b, in_col_blocks=(0, 1, 2), tc=HY_WIDTH, silu=False,
                name="hyena_conv")
    filt_p = (l1_filt_w1, l1_filt_b1, l1_filt_w2, l1_filt_b2, l1_filt_w3, l1_filt_b3, l1_filt_w4,
              l1_filt_freq)
    y = _hyena_mix(u, filt_p, l1_filt_bias)
    h = _matmul(y, l1_w_out, D_MODEL, bias=l1_b_out, gated=(h, mod1, 2), name="l1_out")
    h = ffn(h, mod1, l1_norm_ffn, l1_router_w, l1_router_b, l1_w_gate_up, l1_b_gate_up,
            l1_w_down, l1_b_down)

    y_prompt = _final_norm(h, final_norm, 0, T_CTX).reshape(BATCH, SEQ, D_MODEL)
    y_sample = _final_norm(h, final_norm, T_CTX, T_DEC).reshape(DEC_BATCH, DEC_SEQ, D_MODEL)
    return (y_prompt, y_sample) + produced
```

```python
import functools
import math

import numpy as np
import jax
import jax.numpy as jnp
from jax import lax
from jax.experimental import pallas as pl
from jax.experimental.pallas import tpu as pltpu

D_MODEL = 2048
BATCH = 32
SEQ = 256
DEC_BATCH = 2
DEC_SEQ = 2048
PAST_LEN = 512
GRID_W = 64
NORM_EPS = 1e-6
SSD_WIDTH = D_MODEL // 2
SSD_HEAD_DIM = 64
SSD_HEADS = SSD_WIDTH // SSD_HEAD_DIM
SSD_GROUPS = 2
SSD_STATE = 128
SSD_CHUNK = 128
SSD_CONV_CH = SSD_WIDTH + 2 * SSD_GROUPS * SSD_STATE
DA_WIDTH = D_MODEL // 2
DA_HEAD_DIM = 64
DA_HEADS = DA_WIDTH // (2 * DA_HEAD_DIM)
DA_LAMBDA_INIT = 0.8 - 0.6 * math.exp(-0.3 * 0)
ROPE_BASE = 10000.0
ROPE_FREQS = DA_HEAD_DIM // 4
L0_SPLITS = (SSD_WIDTH, SSD_WIDTH, SSD_GROUPS * SSD_STATE, SSD_GROUPS * SSD_STATE,
             SSD_HEADS, SSD_HEADS, DA_WIDTH, DA_WIDTH, DA_WIDTH)
HY_WIDTH = D_MODEL
HY_BANDS = 16
HY_FAST_DECAY = 0.3
HY_SLOW_DECAY = 1.5
HY_TARGET = 1e-2
N_EXPERTS = 32
TOP_K = 4
SWIGLU_LIMIT = 7.0
SWIGLU_ALPHA = 1.702

T_CTX = BATCH * SEQ
T_DEC = DEC_BATCH * DEC_SEQ
T_ALL = T_CTX + T_DEC
N_COND = 1 + DEC_BATCH
MOD_ROWS = 8

V7X_LANES = 128
V7X_MXU_DIM = 256
V7X_VMEM_BYTES = 64 * 1024 * 1024
VMEM_LIMIT = V7X_VMEM_BYTES - 8 * 1024 * 1024

MOE_TM = 512
MOE_TN = 2048
MOE_ROWS = T_ALL * TOP_K + N_EXPERTS * MOE_TM
MOE_BLOCKS = MOE_ROWS // MOE_TM


def _cparams(n_grid):
    return pltpu.CompilerParams(dimension_semantics=("arbitrary",) * n_grid,
                                vmem_limit_bytes=VMEM_LIMIT)


def _cond_of_row(row):
    return jnp.where(row < T_CTX, 0, (row - T_CTX) // DEC_SEQ + 1)


def _split_bf16(x):
    hi = x.astype(jnp.bfloat16)
    lo = (x - hi.astype(jnp.float32)).astype(jnp.bfloat16)
    return hi, lo


def _dot3(a, w):
    a_hi, a_lo = _split_bf16(a)
    w_hi, w_lo = _split_bf16(w)
    f32 = jnp.float32
    return (jnp.dot(a_hi, w_hi, preferred_element_type=f32)
            + jnp.dot(a_lo, w_hi, preferred_element_type=f32)
            + jnp.dot(a_hi, w_lo, preferred_element_type=f32))


def _adaln_kernel(c_ref, w_ref, b_ref, o_ref):
    c = c_ref[...]
    a = c * jax.nn.sigmoid(c)
    o_ref[...] = _dot3(a, w_ref[...]) + b_ref[...]


def _adaln(cond8, w, b):
    n = w.shape[1]
    tn = 512
    out = pl.pallas_call(
        _adaln_kernel,
        grid=(n // tn,),
        in_specs=[pl.BlockSpec((MOD_ROWS, D_MODEL), lambda j: (0, 0)),
                  pl.BlockSpec((D_MODEL, tn), lambda j: (0, j)),
                  pl.BlockSpec((1, tn), lambda j: (0, j))],
        out_specs=pl.BlockSpec((MOD_ROWS, tn), lambda j: (0, j)),
        out_shape=jax.ShapeDtypeStruct((MOD_ROWS, n), jnp.float32),
        compiler_params=_cparams(1),
        name="adaln",
    )(cond8, w, b.reshape(1, n))
    m = out[:N_COND].reshape(N_COND, 6, D_MODEL)
    return jnp.pad(m, ((0, 0), (0, MOD_ROWS - 6), (0, 0)))


HALF_D = D_MODEL // 2


def _pack_rows(x):
    as_bits = lambda v: lax.bitcast_convert_type(v.astype(jnp.bfloat16).astype(jnp.float32),
                                                 jnp.uint32)
    lo, hi = as_bits(x[:, :HALF_D]), as_bits(x[:, HALF_D:])
    return (lo >> 16) | (hi & jnp.uint32(0xFFFF0000))


def _unpack_rows(p):
    lo = lax.bitcast_convert_type(p << 16, jnp.float32)
    hi = lax.bitcast_convert_type(p & jnp.uint32(0xFFFF0000), jnp.float32)
    return lo, hi


def _norm_rows(x, g):
    var = jnp.mean(x * x, axis=-1, keepdims=True)
    return x * lax.rsqrt(var + NORM_EPS) * g


def _normmod_kernel(h_ref, g_ref, mod_ref, o_ref, *, sh_row):
    y = _norm_rows(h_ref[...], g_ref[...])
    y = y * (1.0 + mod_ref[sh_row + 1:sh_row + 2, :]) + mod_ref[sh_row:sh_row + 1, :]
    o_ref[...] = y.astype(o_ref.dtype)


def _normmod_router_kernel(h_ref, g_ref, mod_ref, rw_ref, rb_ref, o_ref, lg_ref, *, sh_row):
    y = _norm_rows(h_ref[...], g_ref[...])
    y = y * (1.0 + mod_ref[sh_row + 1:sh_row + 2, :]) + mod_ref[sh_row:sh_row + 1, :]
    o_ref[...] = _pack_rows(y)
    lg_ref[...] = _dot3(y, rw_ref[...]) + rb_ref[...]


def _plain_norm_kernel(h_ref, g_ref, o_ref):
    o_ref[...] = _norm_rows(h_ref[...], g_ref[...])


def _normmod(h, g, mod, sh_row, router=None):
    tm = 512
    grid = (T_ALL // tm,)
    h_spec = pl.BlockSpec((tm, D_MODEL), lambda i: (i, 0))
    g_spec = pl.BlockSpec((1, D_MODEL), lambda i: (0, 0))
    mod_spec = pl.BlockSpec((None, MOD_ROWS, D_MODEL), lambda i: (_cond_of_row(i * tm), 0, 0))
    o_spec = pl.BlockSpec((tm, D_MODEL), lambda i: (i, 0))
    o_shape = jax.ShapeDtypeStruct((T_ALL, D_MODEL), jnp.bfloat16)
    if router is None:
        return pl.pallas_call(
            functools.partial(_normmod_kernel, sh_row=sh_row),
            grid=grid, in_specs=[h_spec, g_spec, mod_spec], out_specs=o_spec,
            out_shape=o_shape, compiler_params=_cparams(1), name="normmod",
        )(h, g.reshape(1, D_MODEL), mod)
    rw, rb = router
    return pl.pallas_call(
        functools.partial(_normmod_router_kernel, sh_row=sh_row),
        grid=grid,
        in_specs=[h_spec, g_spec, mod_spec,
                  pl.BlockSpec((D_MODEL, V7X_LANES), lambda i: (0, 0)),
                  pl.BlockSpec((1, V7X_LANES), lambda i: (0, 0))],
        out_specs=[pl.BlockSpec((tm, HALF_D), lambda i: (i, 0)),
                   pl.BlockSpec((tm, V7X_LANES), lambda i: (i, 0))],
        out_shape=[jax.ShapeDtypeStruct((T_ALL, HALF_D), jnp.uint32),
                   jax.ShapeDtypeStruct((T_ALL, V7X_LANES), jnp.float32)],
        compiler_params=_cparams(1), name="normmod_router",
    )(h, g.reshape(1, D_MODEL), mod, rw, rb)


def _final_norm(h, g, row0, n_rows):
    tm = 512
    blk0 = row0 // tm
    return pl.pallas_call(
        _plain_norm_kernel,
        grid=(n_rows // tm,),
        in_specs=[pl.BlockSpec((tm, D_MODEL), lambda i: (i + blk0, 0)),
                  pl.BlockSpec((1, D_MODEL), lambda i: (0, 0))],
        out_specs=pl.BlockSpec((tm, D_MODEL), lambda i: (i, 0)),
        out_shape=jax.ShapeDtypeStruct((n_rows, D_MODEL), jnp.float32),
        compiler_params=_cparams(1), name="final_norm",
    )(h, g.reshape(1, D_MODEL))


def _mm_kernel(*refs, two_a, has_bias, gated, g_row):
    it = iter(refs)
    a_ref = next(it)
    a2_ref = next(it) if two_a else None
    w_ref = next(it)
    b_ref = next(it) if has_bias else None
    res_ref, mod_ref = (next(it), next(it)) if gated else (None, None)
    o_ref, wbf_ref = next(it), next(it)

    @pl.when(pl.program_id(1) == 0)
    def _():
        wbf_ref[...] = w_ref[...].astype(jnp.bfloat16)

    k1 = a_ref.shape[1]
    acc = jnp.dot(a_ref[...], wbf_ref[:k1, :], preferred_element_type=jnp.float32)
    if two_a:
        acc = acc + jnp.dot(a2_ref[...], wbf_ref[k1:, :], preferred_element_type=jnp.float32)
    if has_bias:
        acc = acc + b_ref[...]
    if gated:
        acc = res_ref[...] + mod_ref[g_row:g_row + 1, :] * acc
    o_ref[...] = acc.astype(o_ref.dtype)


def _matmul(a, w, n_cols, *, a2=None, col_blk0=0, bias=None, gated=None,
            out_dtype=jnp.float32, tm=1024, tn=1024, name="matmul"):
    m, k1 = a.shape
    k = w.shape[0]
    assert m % tm == 0 and n_cols % tn == 0
    in_specs = [pl.BlockSpec((tm, k1), lambda j, i: (i, 0))]
    args = [a]
    if a2 is not None:
        assert k1 + a2.shape[1] == k
        in_specs.append(pl.BlockSpec((tm, k - k1), lambda j, i: (i, 0)))
        args.append(a2)
    in_specs.append(pl.BlockSpec((k, tn), lambda j, i: (0, j + col_blk0)))
    args.append(w)
    if bias is not None:
        in_specs.append(pl.BlockSpec((1, tn), lambda j, i: (0, j + col_blk0)))
        args.append(bias.reshape(1, -1))
    g_row = 0
    if gated is not None:
        res, mod, g_row = gated
        in_specs.append(pl.BlockSpec((tm, tn), lambda j, i: (i, j)))
        in_specs.append(pl.BlockSpec((None, MOD_ROWS, tn),
                                     lambda j, i: (_cond_of_row(i * tm), 0, j)))
        args += [res, mod]
    return pl.pallas_call(
        functools.partial(_mm_kernel, two_a=a2 is not None, has_bias=bias is not None,
                          gated=gated is not None, g_row=g_row),
        grid=(n_cols // tn, m // tm),
        in_specs=in_specs,
        out_specs=pl.BlockSpec((tm, tn), lambda j, i: (i, j)),
        out_shape=jax.ShapeDtypeStruct((m, n_cols), out_dtype),
        scratch_shapes=[pltpu.VMEM((k, tn), jnp.bfloat16)],
        compiler_params=_cparams(2), name=name,
    )(*args)


def _gmm_kernel(be_ref, half_ref, nv_ref, x_ref, w_ref, b_ref, *rest, swiglu):
    if swiglu:
        sel_ref, o_ref, wbf_ref = rest
    else:
        o_ref, wbf_ref = rest
    i = pl.program_id(1)
    prev = be_ref[jnp.maximum(i - 1, 0)]

    @pl.when((i == 0) | (be_ref[i] != prev))
    def _():
        wbf_ref[...] = w_ref[...].astype(jnp.bfloat16)

    def compute(n_rows):
        f32, bf = jnp.float32, jnp.bfloat16
        if n_rows < MOE_TM:
            o_ref[n_rows:, :] = jnp.zeros((MOE_TM - n_rows, o_ref.shape[1]), o_ref.dtype)
        if not swiglu:
            acc = jnp.dot(x_ref[0:n_rows, :], wbf_ref[...], preferred_element_type=f32) + b_ref[...]
            o_ref[0:n_rows, :] = acc.astype(o_ref.dtype)
            return
        x_lo, x_hi = _unpack_rows(x_ref[0:n_rows, :])
        acc = (jnp.dot(x_lo.astype(bf), wbf_ref[:HALF_D, :], preferred_element_type=f32)
               + jnp.dot(x_hi.astype(bf), wbf_ref[HALF_D:, :], preferred_element_type=f32)
               + b_ref[...])
        tn = acc.shape[1]
        wide, half = V7X_MXU_DIM, V7X_MXU_DIM // 2
        gate = jnp.minimum(acc, SWIGLU_LIMIT)
        up = pltpu.roll(jnp.clip(acc, -SWIGLU_LIMIT, SWIGLU_LIMIT), tn - 1, 1)
        act = (up + 1.0) * (gate * jax.nn.sigmoid(SWIGLU_ALPHA * gate))
        lane = lax.broadcasted_iota(jnp.int32, act.shape, 1)
        act = jnp.where(lane % 2 == 0, act, 0.0).astype(jnp.bfloat16)
        for g in range(tn // wide):
            o_ref[0:n_rows, g * half:(g + 1) * half] = jnp.dot(
                act[:, g * wide:(g + 1) * wide], sel_ref[...],
                preferred_element_type=f32).astype(o_ref.dtype)

    valid = i < nv_ref[0]

    @pl.when(valid & (half_ref[i] == 0))
    def _():
        compute(MOE_TM)

    @pl.when(valid & (half_ref[i] != 0))
    def _():
        compute(MOE_TM // 2)

    @pl.when(i >= nv_ref[0])
    def _():
        o_ref[...] = jnp.zeros_like(o_ref)


def _grouped_matmul(x, w, b, block_e, block_half, n_valid, *, swiglu, tn, out_dtype, name):
    rows = x.shape[0]
    k, n = w.shape[1:]
    if swiglu:
        tn_out, n_out = tn // 2, n // 2
    else:
        tn_out, n_out = tn, n
    in_specs = [pl.BlockSpec((MOE_TM, x.shape[1]), lambda j, i, be, hf, nv: (i, 0)),
                pl.BlockSpec((None, k, tn), lambda j, i, be, hf, nv: (be[i], 0, j)),
                pl.BlockSpec((None, 1, tn), lambda j, i, be, hf, nv: (be[i], 0, j))]
    args = [x, w, b.reshape(N_EXPERTS, 1, n)]
    if swiglu:
        sel = np.zeros((V7X_MXU_DIM, V7X_MXU_DIM // 2), np.float32)
        sel[2 * np.arange(V7X_MXU_DIM // 2), np.arange(V7X_MXU_DIM // 2)] = 1.0
        in_specs.append(pl.BlockSpec(sel.shape, lambda j, i, be, hf, nv: (0, 0)))
        args.append(jnp.asarray(sel, jnp.bfloat16))
    return pl.pallas_call(
        functools.partial(_gmm_kernel, swiglu=swiglu),
        grid_spec=pltpu.PrefetchScalarGridSpec(
            num_scalar_prefetch=3,
            grid=(n // tn, rows // MOE_TM),
            in_specs=in_specs,
            out_specs=pl.BlockSpec((MOE_TM, tn_out), lambda j, i, be, hf, nv: (i, j)),
            scratch_shapes=[pltpu.VMEM((k, tn), jnp.bfloat16)]),
        out_shape=jax.ShapeDtypeStruct((rows, n_out), out_dtype),
        compiler_params=_cparams(2), name=name,
    )(block_e, block_half, n_valid, *args)


DISPATCH_TM = 256


def _dispatch_kernel(pos_ref, hp_ref, x_init, x_hbm, sem):
    del x_init

    def body(r, carry):
        for k in range(TOP_K):
            pltpu.make_async_copy(hp_ref.at[pl.ds(r, 1)],
                                  x_hbm.at[pl.ds(pos_ref[0, r * TOP_K + k], 1)], sem).start()
        return carry
    lax.fori_loop(0, DISPATCH_TM, body, 0, unroll=4)
    for _ in range(TOP_K):
        pltpu.make_async_copy(hp_ref, x_hbm.at[pl.ds(0, DISPATCH_TM)], sem).wait()


def _moe_dispatch(hp, pos):
    t = hp.shape[0]
    nblk = t // DISPATCH_TM
    pos3 = pos.reshape(nblk, 1, DISPATCH_TM * TOP_K)
    return pl.pallas_call(
        _dispatch_kernel,
        grid=(nblk,),
        in_specs=[pl.BlockSpec((None, 1, DISPATCH_TM * TOP_K), lambda i: (i, 0, 0),
                               memory_space=pltpu.SMEM),
                  pl.BlockSpec((DISPATCH_TM, HALF_D), lambda i: (i, 0)),
                  pl.BlockSpec(memory_space=pl.ANY)],
        out_specs=pl.BlockSpec(memory_space=pl.ANY),
        out_shape=jax.ShapeDtypeStruct((MOE_ROWS, HALF_D), jnp.uint32),
        scratch_shapes=[pltpu.SemaphoreType.DMA(())],
        input_output_aliases={2: 0},
        compiler_params=_cparams(1), name="moe_dispatch",
    )(pos3, hp, jnp.zeros((MOE_ROWS, HALF_D), jnp.uint32))


COMBINE_TM = 128


def _combine_kernel(pos_ref, pos_next_ref, h_ref, gates_ref, mod_ref, y_hbm, o_ref, buf, sem,
                    *, g_row):
    i = pl.program_id(0)
    slot = i % 2
    n_rows = TOP_K * COMBINE_TM

    def issue(idx_ref, s):
        def body(r, carry):
            pltpu.make_async_copy(y_hbm.at[pl.ds(idx_ref[0, r], 1)],
                                  buf.at[s, pl.ds(r, 1)], sem.at[s]).start()
            return carry
        lax.fori_loop(0, n_rows, body, 0, unroll=8)

    @pl.when(i == 0)
    def _():
        issue(pos_ref, 0)

    @pl.when(i + 1 < pl.num_programs(0))
    def _():
        issue(pos_next_ref, 1 - slot)

    pltpu.make_async_copy(y_hbm.at[pl.ds(0, n_rows)], buf.at[slot], sem.at[slot]).wait()
    acc = gates_ref[:, 0:1] * buf[slot, 0:COMBINE_TM, :]
    for k in range(1, TOP_K):
        acc = acc + gates_ref[:, k:k + 1] * buf[slot, k * COMBINE_TM:(k + 1) * COMBINE_TM, :]
    o_ref[...] = h_ref[...] + mod_ref[g_row:g_row + 1, :] * acc


def _moe_combine(h, y_rows, pos, gates_pad, mod, g_row):
    t = h.shape[0]
    nblk = t // COMBINE_TM
    pos3 = pos.reshape(nblk, COMBINE_TM, TOP_K).transpose(0, 2, 1).reshape(nblk, 1, TOP_K * COMBINE_TM)
    smem_blk = lambda f: pl.BlockSpec((None, 1, TOP_K * COMBINE_TM), f, memory_space=pltpu.SMEM)
    return pl.pallas_call(
        functools.partial(_combine_kernel, g_row=g_row),
        grid=(nblk,),
        in_specs=[smem_blk(lambda i: (i, 0, 0)),
                  smem_blk(lambda i: (jnp.minimum(i + 1, nblk - 1), 0, 0)),
                  pl.BlockSpec((COMBINE_TM, D_MODEL), lambda i: (i, 0)),
                  pl.BlockSpec((COMBINE_TM, V7X_LANES), lambda i: (i, 0)),
                  pl.BlockSpec((None, MOD_ROWS, D_MODEL),
                               lambda i: (_cond_of_row(i * COMBINE_TM), 0, 0)),
                  pl.BlockSpec(memory_space=pl.ANY)],
        out_specs=pl.BlockSpec((COMBINE_TM, D_MODEL), lambda i: (i, 0)),
        out_shape=jax.ShapeDtypeStruct((t, D_MODEL), jnp.float32),
        scratch_shapes=[pltpu.VMEM((2, TOP_K * COMBINE_TM, D_MODEL), jnp.float32),
                        pltpu.SemaphoreType.DMA((2,))],
        compiler_params=_cparams(1), name="moe_combine",
    )(pos3, pos3, h, gates_pad, mod, y_rows)


ROUTE_TM = 256


def _route_kernel(lg_ref, idx_ref, gate_ref, rank_ref, cnt_ref, carry):
    f32 = jnp.float32
    i = pl.program_id(0)

    @pl.when(i == 0)
    def _():
        carry[...] = jnp.zeros_like(carry)

    lg = lg_ref[...]
    lane = lax.broadcasted_iota(jnp.int32, lg.shape, 1)
    lane_f = lane.astype(f32)
    vals, ids, hots = [], [], []
    for _ in range(TOP_K):
        m = jnp.max(lg, axis=1, keepdims=True)
        first = jnp.min(jnp.where(lg == m, lane_f, float(V7X_LANES)), axis=1, keepdims=True)
        hot = lane_f == first
        vals.append(m)
        ids.append(first)
        hots.append(hot)
        lg = jnp.where(hot, -jnp.inf, lg)
    exps = [jnp.exp(v - vals[0]) for v in vals]
    denom = exps[0]
    for e in exps[1:]:
        denom = denom + e
    chosen = jnp.zeros(lg.shape, f32)
    for hot in hots:
        chosen = chosen + jnp.where(hot, 1.0, 0.0)
    ri = lax.broadcasted_iota(jnp.int32, (ROUTE_TM, ROUTE_TM), 0)
    ci = lax.broadcasted_iota(jnp.int32, (ROUTE_TM, ROUTE_TM), 1)
    tri = jnp.where(ci <= ri, 1.0, 0.0).astype(jnp.bfloat16)
    upto = jnp.dot(tri, chosen.astype(jnp.bfloat16), preferred_element_type=f32)
    before = carry[...] + upto - chosen
    idx_out = jnp.zeros(lg.shape, f32)
    gate_out = jnp.zeros(lg.shape, f32)
    rank_out = jnp.zeros(lg.shape, f32)
    for k in range(TOP_K):
        rank_k = jnp.sum(jnp.where(hots[k], before, 0.0), axis=1, keepdims=True)
        idx_out = jnp.where(lane == k, ids[k], idx_out)
        gate_out = jnp.where(lane == k, exps[k] / denom, gate_out)
        rank_out = jnp.where(lane == k, rank_k, rank_out)
    idx_ref[...] = idx_out.astype(jnp.int32)
    gate_ref[...] = gate_out
    rank_ref[...] = rank_out.astype(jnp.int32)
    carry[...] = carry[...] + upto[ROUTE_TM - 1:ROUTE_TM, :]
    cnt_ref[...] = carry[...]


def _route(logits):
    t = logits.shape[0]
    blk = pl.BlockSpec((ROUTE_TM, V7X_LANES), lambda i: (i, 0))
    row = pl.BlockSpec((1, V7X_LANES), lambda i: (0, 0))
    sds = lambda dt: jax.ShapeDtypeStruct((t, V7X_LANES), dt)
    return pl.pallas_call(
        _route_kernel,
        grid=(t // ROUTE_TM,),
        in_specs=[blk],
        out_specs=[blk, blk, blk, row],
        out_shape=[sds(jnp.int32), sds(jnp.float32), sds(jnp.int32),
                   jax.ShapeDtypeStruct((1, V7X_LANES), jnp.float32)],
        scratch_shapes=[pltpu.VMEM((1, V7X_LANES), jnp.float32)],
        compiler_params=_cparams(1), name="moe_route",
    )(logits)


def _moe(h, hn, logits, w_gu, b_gu, w_dn, b_dn, mod):
    t = hn.shape[0]
    idx, gates_pad, rank, counts = _route(logits)
    top_idx = idx[:, :TOP_K]
    counts = counts[0, :N_EXPERTS].astype(jnp.int32)
    padded = (counts + MOE_TM - 1) // MOE_TM * MOE_TM
    padded_end = jnp.cumsum(padded)
    padded_start = padded_end - padded
    pos = padded_start[top_idx] + rank[:, :TOP_K]
    block_e = jnp.minimum(
        jnp.searchsorted(padded_end, jnp.arange(MOE_BLOCKS) * MOE_TM, side='right'),
        N_EXPERTS - 1).astype(jnp.int32)
    n_valid = (padded_end[-1] // MOE_TM).astype(jnp.int32).reshape(1)
    rows_used = (padded_start + counts)[block_e] - jnp.arange(MOE_BLOCKS) * MOE_TM
    block_half = (rows_used <= MOE_TM // 2).astype(jnp.int32)

    x_rows = _moe_dispatch(hn, pos)
    act = _grouped_matmul(x_rows, w_gu, b_gu, block_e, block_half, n_valid, swiglu=True, tn=MOE_TN,
                          out_dtype=jnp.bfloat16, name="moe_gate_up")
    y_rows = _grouped_matmul(act, w_dn, b_dn, block_e, block_half, n_valid, swiglu=False, tn=MOE_TN,
                             out_dtype=jnp.float32, name="moe_down")
    return _moe_combine(h, y_rows, pos, gates_pad, mod, 5)


CONV_TM = 256
N_CHUNKS = T_ALL // SSD_CHUNK
CTX_CHUNKS = T_CTX // SSD_CHUNK
N_SEQS = BATCH + DEC_BATCH


def _seq_len_of_row(row):
    return jnp.where(row < T_CTX, SEQ, DEC_SEQ)


def _seq_pos_of_row(row):
    return jnp.where(row < T_CTX, row % SEQ, (row - T_CTX) % DEC_SEQ)


def _seq_of_chunk(n):
    return jnp.where(n < CTX_CHUNKS, n // (SEQ // SSD_CHUNK),
                     BATCH + (n - CTX_CHUNKS) // (DEC_SEQ // SSD_CHUNK))


def _dwconv_kernel(x_ref, prev_ref, next_ref, w_ref, b_ref, o_ref, *, width, silu):
    row0 = pl.program_id(1) * CONV_TM
    pos0 = _seq_pos_of_row(row0)
    has_prev = pos0 != 0
    has_next = pos0 + CONV_TM != _seq_len_of_row(row0)
    x = x_ref[...]
    rid = lax.broadcasted_iota(jnp.int32, x.shape, 0)
    prev = jnp.where(has_prev, prev_ref[7:8, :], 0.0)
    nxt0 = jnp.where(has_next, next_ref[0:1, :], 0.0)
    acc = w_ref[1:2, :] * x + b_ref[...]
    xm1 = jnp.where(rid == 0, prev, pltpu.roll(x, 1, 0))
    acc = acc + w_ref[0:1, :] * xm1
    xp1 = jnp.where(rid == CONV_TM - 1, nxt0, pltpu.roll(x, CONV_TM - 1, 0))
    acc = acc + w_ref[2:3, :] * xp1
    if width == 4:
        nxt1 = jnp.where(has_next, next_ref[1:2, :], 0.0)
        xp2 = pltpu.roll(x, CONV_TM - 2, 0)
        xp2 = jnp.where(rid == CONV_TM - 2, nxt0, jnp.where(rid == CONV_TM - 1, nxt1, xp2))
        acc = acc + w_ref[3:4, :] * xp2
    if silu:
        acc = acc * jax.nn.sigmoid(acc)
    o_ref[...] = acc


def _dwconv(x, w, b, *, in_col_blocks, tc, silu, name):
    width = w.shape[0]
    nblk = len(in_col_blocks)
    assert w.shape[1] == nblk * tc and width in (3, 4)
    cols = np.asarray(in_col_blocks, np.int32)
    if np.array_equal(cols, np.arange(nblk)):
        col = lambda j: j
    else:
        assert nblk == 3 and cols[0] == 0 and cols[1] == 1
        col = lambda j: jnp.where(j < 2, j, int(cols[2]))
    w8 = jnp.pad(w, ((0, 8 - width), (0, 0)))
    sub = CONV_TM // 8
    last8 = T_ALL // 8 - 1
    return pl.pallas_call(
        functools.partial(_dwconv_kernel, width=width, silu=silu),
        grid=(nblk, T_ALL // CONV_TM),
        in_specs=[pl.BlockSpec((CONV_TM, tc), lambda j, i: (i, col(j))),
                  pl.BlockSpec((8, tc), lambda j, i: (jnp.maximum(i * sub - 1, 0), col(j))),
                  pl.BlockSpec((8, tc), lambda j, i: (jnp.minimum((i + 1) * sub, last8), col(j))),
                  pl.BlockSpec((8, tc), lambda j, i: (0, j)),
                  pl.BlockSpec((1, tc), lambda j, i: (0, j))],
        out_specs=pl.BlockSpec((CONV_TM, tc), lambda j, i: (i, j)),
        out_shape=jax.ShapeDtypeStruct((T_ALL, nblk * tc), jnp.float32),
        compiler_params=_cparams(2), name=name,
    )(x, x, x, w8, b.reshape(1, -1))


def _softplus(x):
    return jnp.maximum(x, 0.0) + jnp.log(1.0 + jnp.exp(-jnp.abs(x)))


def _ssd_kernel(*refs, reverse):
    if reverse:
        (xbc_ref, zdt_ref, bias_ref, alog_ref, h0_ref, yf_ref, z_ref, dskip_ref, gn_ref,
         y_ref, st_ref, s_scr, y_scr) = refs
    else:
        xbc_ref, zdt_ref, bias_ref, alog_ref, h0_ref, y_ref, st_ref, s_scr = refs
    f32, bf = jnp.float32, jnp.bfloat16
    q = SSD_CHUNK
    step = pl.program_id(0)
    n = (N_CHUNKS - 1 - step) if reverse else step
    row0 = n * q
    pos0 = _seq_pos_of_row(row0)
    seq_len = _seq_len_of_row(row0)
    starts_seq = (pos0 + q == seq_len) if reverse else (pos0 == 0)
    ends_seq = (pos0 == 0) if reverse else (pos0 + q == seq_len)

    @pl.when(starts_seq)
    def _():
        s_scr[...] = jnp.where(row0 < T_CTX, 0.0, h0_ref[...])

    ri = lax.broadcasted_iota(jnp.int32, (q, q), 0)
    ci = lax.broadcasted_iota(jnp.int32, (q, q), 1)
    mask = (ci >= ri) if reverse else (ci <= ri)
    lane_lo = ci < SSD_HEAD_DIM
    row_lo = ri < SSD_HEAD_DIM

    dt = _softplus(zdt_ref[...] + bias_ref[...])
    da = dt * (-jnp.exp(alog_ref[...]))
    ones = jnp.where(mask, 1.0, 0.0).astype(bf)
    d1 = da.astype(bf)
    r1 = da - d1.astype(f32)
    d2 = r1.astype(bf)
    d3 = (r1 - d2.astype(f32)).astype(bf)
    cs = (jnp.dot(ones, d1, preferred_element_type=f32)
          + jnp.dot(ones, d2, preferred_element_type=f32)
          + jnp.dot(ones, d3, preferred_element_type=f32))
    cs_t = cs.T
    tot = cs[0:1, :] if reverse else cs[q - 1:q, :]
    exp_cs = jnp.exp(cs)
    d_end = jnp.exp(tot - cs)
    exp_tot = jnp.exp(tot)

    def pair_cols(v, p):
        return jnp.where(lane_lo, v[:, 2 * p:2 * p + 1], v[:, 2 * p + 1:2 * p + 2])

    dn_t = (((1,), (1,)), ((), ()))
    for g in range(SSD_GROUPS):
        b_g = xbc_ref[:, SSD_WIDTH + g * SSD_STATE:SSD_WIDTH + (g + 1) * SSD_STATE].astype(bf)
        c_off = SSD_WIDTH + SSD_GROUPS * SSD_STATE
        c_g = xbc_ref[:, c_off + g * SSD_STATE:c_off + (g + 1) * SSD_STATE].astype(bf)
        cb = lax.dot_general(c_g, b_g, dn_t, preferred_element_type=f32)
        pairs_per_group = SSD_HEADS // SSD_GROUPS // 2
        for p in range(g * pairs_per_group, (g + 1) * pairs_per_group):
            lo, hi = p * 2 * SSD_HEAD_DIM, (p + 1) * 2 * SSD_HEAD_DIM
            xs = xbc_ref[:, lo:hi]
            xdt = xs * pair_cols(dt, p)
            y = jnp.zeros((q, 2 * SSD_HEAD_DIM), f32)
            for h, keep in ((2 * p, lane_lo), (2 * p + 1, ~lane_lo)):
                seg = cs[:, h:h + 1] - cs_t[h:h + 1, :]
                w = jnp.where(mask, jnp.exp(jnp.minimum(seg, 0.0)), 0.0) * cb
                y = y + jnp.dot(w.astype(bf), jnp.where(keep, xdt, 0.0).astype(bf),
                                preferred_element_type=f32)
            s_pair = s_scr[lo:hi, :]
            y_off = lax.dot_general(c_g, s_pair.astype(bf), dn_t, preferred_element_type=f32)
            y = y + y_off * pair_cols(exp_cs, p)
            st = jnp.dot((xdt * pair_cols(d_end, p)).T.astype(bf), b_g,
                         preferred_element_type=f32)
            keep_s = jnp.where(row_lo, exp_tot[:, 2 * p:2 * p + 1], exp_tot[:, 2 * p + 1:2 * p + 2])
            s_scr[lo:hi, :] = s_pair * keep_s + st
            if reverse:
                y_scr[:, lo:hi] = y + yf_ref[:, lo:hi] + dskip_ref[:, lo:hi] * xs
            else:
                y_ref[:, lo:hi] = y

    if reverse:
        zz = z_ref[...]
        yy = y_scr[...] * (zz * jax.nn.sigmoid(zz))
        gw = SSD_WIDTH // SSD_GROUPS
        for g in range(SSD_GROUPS):
            y_g = yy[:, g * gw:(g + 1) * gw]
            y_g = y_g * lax.rsqrt(jnp.mean(y_g * y_g, axis=-1, keepdims=True) + NORM_EPS)
            y_ref[:, g * gw:(g + 1) * gw] = (y_g * gn_ref[:, g * gw:(g + 1) * gw]).astype(y_ref.dtype)

    @pl.when(ends_seq)
    def _():
        st_ref[...] = s_scr[...]


def _ssd(xbc, z_dt, dt_bias, a_log, h0, *, reverse, finish=None):
    q = SSD_CHUNK
    d = 1 if reverse else 0
    chunk = (lambda s: N_CHUNKS - 1 - s) if reverse else (lambda s: s)
    pad_row = lambda v: jnp.pad(v, (0, V7X_LANES - SSD_HEADS)).reshape(1, V7X_LANES)
    in_specs = [pl.BlockSpec((q, SSD_CONV_CH), lambda s: (chunk(s), 0)),
                pl.BlockSpec((q, V7X_LANES), lambda s: (chunk(s), d)),
                pl.BlockSpec((1, V7X_LANES), lambda s: (0, 0)),
                pl.BlockSpec((1, V7X_LANES), lambda s: (0, 0)),
                pl.BlockSpec((None, SSD_WIDTH, SSD_STATE),
                             lambda s: (jnp.maximum(_seq_of_chunk(chunk(s)) - BATCH, 0), 0, 0))]
    args = [xbc, z_dt, pad_row(dt_bias), pad_row(a_log), h0]
    scratch = [pltpu.VMEM((SSD_WIDTH, SSD_STATE), jnp.float32)]
    y_dtype = jnp.float32
    if reverse:
        y_fwd, z_ssd, dskip_row, gn_row = finish
        in_specs += [pl.BlockSpec((q, SSD_WIDTH), lambda s: (chunk(s), 0)),
                     pl.BlockSpec((q, SSD_WIDTH), lambda s: (chunk(s), 1)),
                     pl.BlockSpec((1, SSD_WIDTH), lambda s: (0, 0)),
                     pl.BlockSpec((1, SSD_WIDTH), lambda s: (0, 0))]
        args += [y_fwd, z_ssd, dskip_row, gn_row]
        scratch.append(pltpu.VMEM((q, SSD_WIDTH), jnp.float32))
        y_dtype = jnp.bfloat16
    return pl.pallas_call(
        functools.partial(_ssd_kernel, reverse=reverse),
        grid=(N_CHUNKS,),
        in_specs=in_specs,
        out_specs=[pl.BlockSpec((q, SSD_WIDTH), lambda s: (chunk(s), 0)),
                   pl.BlockSpec((None, SSD_WIDTH, SSD_STATE),
                                lambda s: (_seq_of_chunk(chunk(s)), 0, 0))],
        out_shape=[jax.ShapeDtypeStruct((T_ALL, SSD_WIDTH), y_dtype),
                   jax.ShapeDtypeStruct((N_SEQS, SSD_WIDTH, SSD_STATE), jnp.float32)],
        scratch_shapes=scratch,
        compiler_params=_cparams(1), name="ssd_bwd" if reverse else "ssd_fwd",
    )(*args)


ATT_TQ = 256
HEAD_W = 2 * DA_HEAD_DIM


def _rope_tables():
    t = jnp.arange(DEC_SEQ)
    pos = jnp.stack([t // GRID_W, t % GRID_W], axis=-1).astype(jnp.float32)
    inv = 1.0 / (ROPE_BASE ** (jnp.arange(ROPE_FREQS, dtype=jnp.float32) / ROPE_FREQS))
    ang = pos[:, :, None] * inv
    cos, sin = jnp.cos(ang), jnp.sin(ang)
    zero = jnp.zeros_like(sin)
    lanes = lambda first, second: jnp.tile(
        jnp.stack([first, second], axis=2).reshape(DEC_SEQ, DA_HEAD_DIM), (1, 2))
    return lanes(cos, cos), lanes(-sin, zero), lanes(zero, sin)


def _attn_kernel(lam_ref, *refs, has_ctx):
    if has_ctx:
        (q_ref, k_ref, v_ref, kc_ref, vc_ref, cq, saq, sbq, ck, sak, sbk, sw_ref, o_ref) = refs
    else:
        q_ref, k_ref, v_ref, sw_ref, o_ref = refs
    f32, bf = jnp.float32, jnp.bfloat16
    half = ROPE_FREQS
    scale = DA_HEAD_DIM ** -0.5

    def rope(x, c, sa, sb):
        return (x * c[...] + pltpu.roll(x, HEAD_W - half, 1) * sa[...]
                + pltpu.roll(x, half, 1) * sb[...])

    for hd in range(q_ref.shape[1] // HEAD_W):
        cols = slice(hd * HEAD_W, (hd + 1) * HEAD_W)
        q, k = q_ref[:, cols], k_ref[:, cols]
        if has_ctx:
            q, k = rope(q, cq, saq, sbq), rope(k, ck, sak, sbk)
        lane = lax.broadcasted_iota(jnp.int32, q.shape, 1)
        kb, vb = k.astype(bf), v_ref[:, cols].astype(bf)
        if has_ctx:
            kb = jnp.concatenate([kc_ref[:, cols].astype(bf), kb], axis=0)
            vb = jnp.concatenate([vc_ref[:, cols].astype(bf), vb], axis=0)

        def exps(qm):
            s = lax.dot_general(qm.astype(bf), kb, (((1,), (1,)), ((), ())),
                                preferred_element_type=f32) * scale
            e = jnp.exp(s - jnp.max(s, axis=-1, keepdims=True))
            return e, 1.0 / jnp.sum(e, axis=-1, keepdims=True)

        e0, r0 = exps(jnp.where(lane < DA_HEAD_DIM, q, 0.0))
        e1, r1 = exps(jnp.where(lane >= DA_HEAD_DIM, q, 0.0))
        w = e0 * r0 - e1 * (lam_ref[0] * r1)
        o = jnp.dot(w.astype(bf), vb, preferred_element_type=f32)
        o = o * lax.rsqrt(jnp.mean(o * o, axis=-1, keepdims=True) + NORM_EPS) * sw_ref[...]
        o_ref[:, cols] = (o * (1.0 - DA_LAMBDA_INIT)).astype(o_ref.dtype)


def _diff_attention(z_qkv, cache_k, cache_v, lam, subln_w):
    sw = subln_w.reshape(1, HEAD_W)
    ctx_w = 2 * HEAD_W
    ctx_blk = lambda part: pl.BlockSpec(
        (SEQ, ctx_w), lambda b, h, lam: (b, part * (DA_WIDTH // ctx_w) + h))
    o_ctx = pl.pallas_call(
        functools.partial(_attn_kernel, has_ctx=False),
        grid_spec=pltpu.PrefetchScalarGridSpec(
            num_scalar_prefetch=1, grid=(BATCH, DA_WIDTH // ctx_w),
            in_specs=[ctx_blk(0), ctx_blk(1), ctx_blk(2),
                      pl.BlockSpec((1, HEAD_W), lambda b, h, lam: (0, 0))],
            out_specs=ctx_blk(0)),
        out_shape=jax.ShapeDtypeStruct((T_CTX, DA_WIDTH), jnp.bfloat16),
        compiler_params=_cparams(2), name="attn_ctx",
    )(lam, z_qkv, z_qkv, z_qkv, sw)

    tabs = _rope_tables()
    nq = DEC_SEQ // ATT_TQ
    q_blk0 = T_CTX // ATT_TQ
    kv_blk0 = T_CTX // DEC_SEQ
    kc = cache_k.reshape(DEC_BATCH * PAST_LEN, DA_WIDTH)
    vc = cache_v.reshape(DEC_BATCH * PAST_LEN, DA_WIDTH)
    q_row = lambda b, h, i, lam: (q_blk0 + b * nq + i, h)
    tab_q = pl.BlockSpec((ATT_TQ, HEAD_W), lambda b, h, i, lam: (i, 0))
    tab_k = pl.BlockSpec((DEC_SEQ, HEAD_W), lambda b, h, i, lam: (0, 0))
    o_dec = pl.pallas_call(
        functools.partial(_attn_kernel, has_ctx=True),
        grid_spec=pltpu.PrefetchScalarGridSpec(
            num_scalar_prefetch=1, grid=(DEC_BATCH, DA_HEADS, nq),
            in_specs=[pl.BlockSpec((ATT_TQ, HEAD_W), q_row),
                      pl.BlockSpec((DEC_SEQ, HEAD_W), lambda b, h, i, lam: (kv_blk0 + b, DA_HEADS + h)),
                      pl.BlockSpec((DEC_SEQ, HEAD_W), lambda b, h, i, lam: (kv_blk0 + b, 2 * DA_HEADS + h)),
                      pl.BlockSpec((PAST_LEN, HEAD_W), lambda b, h, i, lam: (b, h)),
                      pl.BlockSpec((PAST_LEN, HEAD_W), lambda b, h, i, lam: (b, h)),
                      tab_q, tab_q, tab_q, tab_k, tab_k, tab_k,
                      pl.BlockSpec((1, HEAD_W), lambda b, h, i, lam: (0, 0))],
            out_specs=pl.BlockSpec((ATT_TQ, HEAD_W), lambda b, h, i, lam: (b * nq + i, h))),
        out_shape=jax.ShapeDtypeStruct((T_DEC, DA_WIDTH), jnp.bfloat16),
        compiler_params=_cparams(3), name="attn_dec",
    )(lam, z_qkv, z_qkv, z_qkv, kc, vc, tabs[0], tabs[1], tabs[2], tabs[0], tabs[1], tabs[2], sw)
    return jnp.concatenate([o_ctx, o_dec], axis=0)


def _dft_cos_sin(L, s_offset):
    k0n = 64
    sp = jnp.arange(L, dtype=jnp.int32) + s_offset
    phase = lambda mult: ((mult[:, None] * sp[None, :]) % (4 * L)).astype(jnp.float32) * (
        math.pi / (2 * L))
    x = phase(2 * k0n * jnp.arange(L // k0n, dtype=jnp.int32))[:, None, :]
    y = phase(2 * jnp.arange(k0n, dtype=jnp.int32) + 1)[None, :, :]
    cx, sx, cy, sy = jnp.cos(x), jnp.sin(x), jnp.cos(y), jnp.sin(y)
    return (cx * cy - sx * sy).reshape(L, L), (sx * cy + cx * sy).reshape(L, L)


def _hyena_tables(L):
    bf = jnp.bfloat16
    c, s = _dft_cos_sin(L, 0)
    fwd = jnp.concatenate([c, -s], axis=0).astype(bf)
    inv = (jnp.concatenate([c.T, -s.T], axis=1) * (1.0 / L)).astype(bf)
    cf, sf = _dft_cos_sin(L, -(L // 2))
    filt_fwd = jnp.concatenate([cf, -sf], axis=0).astype(bf)
    return fwd, inv, filt_fwd


def _hyena_filter_kernel(band_ref, w1t_ref, w1c_ref, w1s_ref, b1_ref, w2_ref, b2_ref, w3_ref,
                         b3_ref, fr_ref, w4_ref, dl_ref, o_ref, h_scr, *, L):
    t = lax.broadcasted_iota(jnp.int32, (L, 1), 0).astype(jnp.float32)

    @pl.when(pl.program_id(0) == 0)
    def _():
        ang = ((2.0 * math.pi / L) * t) * band_ref[...]
        fr = fr_ref[...]
        z = ((t / L) * w1t_ref[...] + _dot3(jnp.cos(ang), w1c_ref[...])
             + _dot3(jnp.sin(ang), w1s_ref[...]))
        h = jnp.sin(fr * (z + b1_ref[...]))
        h = jnp.sin(fr * (_dot3(h, w2_ref[...]) + b2_ref[...]))
        h_scr[...] = jnp.sin(fr * (_dot3(h, w3_ref[...]) + b3_ref[...]))

    dist = jnp.abs(t - (L // 2)) / L
    o_ref[...] = _dot3(h_scr[...], w4_ref[...]) * jnp.exp(-dist * dl_ref[...])


def _hyena_filter(L, w1, b1, w2, b2, w3, b3, w4, freq):
    hid = w2.shape[0]
    tn = 512
    bands = jnp.linspace(1e-4, HY_BANDS - 1, HY_BANDS, dtype=jnp.float32)
    lane_pad = V7X_LANES - HY_BANDS
    band_row = jnp.pad(bands, (0, lane_pad)).reshape(1, V7X_LANES)
    pad_rows = lambda w: jnp.pad(w, ((0, lane_pad), (0, 0)))
    deltas = jnp.abs(jnp.linspace(math.log(HY_TARGET) / HY_SLOW_DECAY,
                                  math.log(HY_TARGET) / HY_FAST_DECAY, HY_WIDTH,
                                  dtype=jnp.float32)).reshape(1, HY_WIDTH)
    full = lambda shape: pl.BlockSpec(shape, lambda j: (0,) * len(shape))
    return pl.pallas_call(
        functools.partial(_hyena_filter_kernel, L=L),
        grid=(HY_WIDTH // tn,),
        in_specs=[full((1, V7X_LANES)), full((1, hid)), full((V7X_LANES, hid)),
                  full((V7X_LANES, hid)), full((1, hid)), full((hid, hid)), full((1, hid)),
                  full((hid, hid)), full((1, hid)), full((1, hid)),
                  pl.BlockSpec((hid, tn), lambda j: (0, j)),
                  pl.BlockSpec((1, tn), lambda j: (0, j))],
        out_specs=pl.BlockSpec((L, tn), lambda j: (0, j)),
        out_shape=jax.ShapeDtypeStruct((L, HY_WIDTH), jnp.float32),
        scratch_shapes=[pltpu.VMEM((L, hid), jnp.float32)],
        compiler_params=_cparams(1), name=f"hyena_filter_{L}",
    )(band_row, w1[0:1], pad_rows(w1[1:1 + HY_BANDS]), pad_rows(w1[1 + HY_BANDS:]),
      b1.reshape(1, hid), w2, b2.reshape(1, hid), w3, b3.reshape(1, hid), freq.reshape(1, hid),
      w4, deltas)


def _hyena_conv_kernel(*refs):
    (x0_ref, x1_ref, v_ref, fr_ref, fi_ref, hr_ref, hi_ref, ir_ref, ii_ref, fb_ref) = refs[:10]
    o_ref, vbf, acc = refs[-3:]
    f32, bf = jnp.float32, jnp.bfloat16
    kb = pl.program_id(2)

    @pl.when(kb == 0)
    def _():
        vbf[...] = (v_ref[...] * x1_ref[...]).astype(bf)
        acc[...] = jnp.zeros_like(acc)

    vr = jnp.dot(fr_ref[...], vbf[...], preferred_element_type=f32)
    vi = jnp.dot(fi_ref[...], vbf[...], preferred_element_type=f32)
    hr, hi = hr_ref[...], hi_ref[...]
    yr = (vr * hr - vi * hi).astype(bf)
    yi = (vr * hi + vi * hr).astype(bf)
    acc[...] += (jnp.dot(ir_ref[...], yr, preferred_element_type=f32)
                 + jnp.dot(ii_ref[...], yi, preferred_element_type=f32))

    @pl.when(kb == pl.num_programs(2) - 1)
    def _():
        vv = v_ref[...] * x1_ref[...]
        o_ref[...] = ((acc[...] + vv * fb_ref[...]) * x0_ref[...]).astype(o_ref.dtype)


def _hyena_conv(u, filt_spec, tables, f_bias, *, L, row_blk0, n_seq, tn, kblk):
    fwd, inv, _ = tables
    nj, nk = HY_WIDTH // tn, L // kblk
    u_spec = lambda part: pl.BlockSpec((L, tn), lambda b, j, k: (row_blk0 + b, part * nj + j))
    in_specs = [u_spec(0), u_spec(1), u_spec(2),
                pl.BlockSpec((kblk, L), lambda b, j, k: (k, 0)),
                pl.BlockSpec((kblk, L), lambda b, j, k: (nk + k, 0)),
                pl.BlockSpec((kblk, tn), lambda b, j, k: (k, j)),
                pl.BlockSpec((kblk, tn), lambda b, j, k: (nk + k, j)),
                pl.BlockSpec((L, kblk), lambda b, j, k: (0, k)),
                pl.BlockSpec((L, kblk), lambda b, j, k: (0, nk + k)),
                pl.BlockSpec((1, tn), lambda b, j, k: (0, j))]
    args = [u, u, u, fwd, fwd, filt_spec, filt_spec, inv, inv, f_bias.reshape(1, HY_WIDTH)]
    return pl.pallas_call(
        _hyena_conv_kernel,
        grid=(n_seq, nj, nk),
        in_specs=in_specs,
        out_specs=pl.BlockSpec((L, tn), lambda b, j, k: (b, j)),
        out_shape=jax.ShapeDtypeStruct((n_seq * L, HY_WIDTH), jnp.bfloat16),
        scratch_shapes=[pltpu.VMEM((L, tn), jnp.bfloat16), pltpu.VMEM((L, tn), jnp.float32)],
        compiler_params=_cparams(3), name=f"hyena_conv_{L}",
    )(*args)


def _hyena_mix(u, filt_p, f_bias):
    outs = []
    for L, row_blk0, n_seq, tn, kblk in ((SEQ, 0, BATCH, 1024, SEQ),
                                        (DEC_SEQ, T_CTX // DEC_SEQ, DEC_BATCH, 256, 512)):
        tables = _hyena_tables(L)
        filt = _hyena_filter(L, *filt_p)
        spec = _matmul(tables[2], filt, HY_WIDTH, tm=min(1024, 2 * L), name=f"hyena_spec_{L}")
        outs.append(_hyena_conv(u, spec, tables, f_bias, L=L, row_blk0=row_blk0, n_seq=n_seq,
                                tn=tn, kblk=kblk))
    return jnp.concatenate(outs, axis=0)


def kernel(x_prompt, x_sample, c, state_l0_ssm_fwd, state_l0_ssm_bwd, cache_l0_k, cache_l0_v, c_ctx, l0_norm_mix, l0_norm_ffn, l0_ada_w, l0_ada_b, l0_w_in, l0_conv_w, l0_conv_b, l0_a_log_fwd, l0_a_log_bwd, l0_dt_bias_fwd, l0_dt_bias_bwd, l0_d_skip, l0_gnorm_w, l0_lambda_q1, l0_lambda_k1, l0_lambda_q2, l0_lambda_k2, l0_subln_w, l0_w_out, l0_router_w, l0_router_b, l0_w_gate_up, l0_b_gate_up, l0_w_down, l0_b_down, l1_norm_mix, l1_norm_ffn, l1_ada_w, l1_ada_b, l1_w_in, l1_b_in, l1_short_w, l1_short_b, l1_filt_w1, l1_filt_b1, l1_filt_w2, l1_filt_b2, l1_filt_w3, l1_filt_b3, l1_filt_w4, l1_filt_freq, l1_filt_bias, l1_w_out, l1_b_out, l1_router_w, l1_router_b, l1_w_gate_up, l1_b_gate_up, l1_w_down, l1_b_down, final_norm):
    h = jnp.concatenate([x_prompt.reshape(T_CTX, D_MODEL), x_sample.reshape(T_DEC, D_MODEL)], axis=0)
    cond8 = jnp.concatenate([c_ctx[None, :], c,
                             jnp.zeros((MOD_ROWS - N_COND, D_MODEL), jnp.float32)], axis=0)

    def router_args(rw, rb):
        pad = V7X_LANES - N_EXPERTS
        return (jnp.pad(rw, ((0, 0), (0, pad))),
                jnp.pad(rb, (0, pad), constant_values=-1e30).reshape(1, V7X_LANES))

    def ffn(h, mod, norm_g, rw, rb, w_gu, b_gu, w_dn, b_dn):
        hn, logits = _normmod(h, norm_g, mod, 3, router=router_args(rw, rb))
        return _moe(h, hn, logits, w_gu, b_gu, w_dn, b_dn, mod)

    mod0 = _adaln(cond8, l0_ada_w, l0_ada_b)
    hn = _normmod(h, l0_norm_mix, mod0, 0)
    c_dt = sum(L0_SPLITS[:4])
    c_q = c_dt + 2 * SSD_HEADS
    z_ssd = _matmul(hn, l0_w_in, c_dt, tn=512, name="l0_in_ssd")
    lane_pad = jnp.zeros((D_MODEL, V7X_LANES - SSD_HEADS), jnp.float32)
    w_dt = jnp.concatenate([l0_w_in[:, c_dt:c_dt + SSD_HEADS], lane_pad,
                            l0_w_in[:, c_dt + SSD_HEADS:c_q], lane_pad], axis=1)
    z_dt = _matmul(hn, w_dt, 2 * V7X_LANES, tn=2 * V7X_LANES, name="l0_in_dt")
    z_qkv = _matmul(hn, l0_w_in[:, c_q:], 3 * DA_WIDTH, name="l0_in_qkv")
    xbc = _dwconv(z_ssd, l0_conv_w, l0_conv_b, in_col_blocks=(0, 1, 4), tc=512, silu=True,
                  name="ssd_conv")
    h0_shape = (DEC_BATCH, SSD_WIDTH, SSD_STATE)
    y_fwd, st_fwd = _ssd(xbc, z_dt, l0_dt_bias_fwd, l0_a_log_fwd,
                         state_l0_ssm_fwd.reshape(h0_shape), reverse=False)
    y_ssd, st_bwd = _ssd(xbc, z_dt, l0_dt_bias_bwd, l0_a_log_bwd,
                         state_l0_ssm_bwd.reshape(h0_shape), reverse=True,
                         finish=(y_fwd, z_ssd, jnp.repeat(l0_d_skip, SSD_HEAD_DIM).reshape(1, -1),
                                 l0_gnorm_w.reshape(1, -1)))
    lam = (jnp.exp(jnp.sum(l0_lambda_q1 * l0_lambda_k1))
           - jnp.exp(jnp.sum(l0_lambda_q2 * l0_lambda_k2)) + DA_LAMBDA_INIT).reshape(1)
    o_att = _diff_attention(z_qkv, cache_l0_k, cache_l0_v, lam, l0_subln_w)
    h = _matmul(y_ssd, l0_w_out, D_MODEL, a2=o_att, gated=(h, mod0, 2), name="l0_out")
    h = ffn(h, mod0, l0_norm_ffn, l0_router_w, l0_router_b, l0_w_gate_up, l0_b_gate_up,
            l0_w_down, l0_b_down)
    st_shape = (BATCH, SSD_HEADS, SSD_HEAD_DIM, SSD_STATE)
    kv_shape = (BATCH, SEQ, DA_HEADS, HEAD_W)
    produced = (st_fwd[:BATCH].reshape(st_shape), st_bwd[:BATCH].reshape(st_shape),
                z_qkv[:T_CTX, DA_WIDTH:2 * DA_WIDTH].reshape(kv_shape),
                z_qkv[:T_CTX, 2 * DA_WIDTH:].reshape(kv_shape))

    mod1 = _adaln(cond8, l1_ada_w, l1_ada_b)
    hn = _normmod(h, l1_norm_mix, mod1, 0)
    u = _matmul(hn, l1_w_in, 3 * HY_WIDTH, bias=l1_b_in, name="l1_in")
    u = _dwconv(u, l1_short_w, l1_short_b, in_col_blocks=(0, 1, 2), tc=HY_WIDTH, silu=False,
                name="hyena_conv")
    filt_p = (l1_filt_w1, l1_filt_b1, l1_filt_w2, l1_filt_b2, l1_filt_w3, l1_filt_b3, l1_filt_w4,
              l1_filt_freq)
    y = _hyena_mix(u, filt_p, l1_filt_bias)
    h = _matmul(y, l1_w_out, D_MODEL, bias=l1_b_out, gated=(h, mod1, 2), name="l1_out")
    h = ffn(h, mod1, l1_norm_ffn, l1_router_w, l1_router_b, l1_w_gate_up, l1_b_gate_up,
            l1_w_down, l1_b_down)

    y_prompt = _final_norm(h, final_norm, 0, T_CTX).reshape(BATCH, SEQ, D_MODEL)
    y_sample = _final_norm(h, final_norm, T_CTX, T_DEC).reshape(DEC_BATCH, DEC_SEQ, D_MODEL)
    return (y_prompt, y_sample) + produced
```

```python
import functools
import math

import numpy as np
import jax
import jax.numpy as jnp
from jax import lax
from jax.experimental import pallas as pl
from jax.experimental.pallas import tpu as pltpu

D_MODEL = 2048
BATCH = 32
SEQ = 256
DEC_BATCH = 2
DEC_SEQ = 2048
PAST_LEN = 512
GRID_W = 64
NORM_EPS = 1e-6
SSD_WIDTH = D_MODEL // 2
SSD_HEAD_DIM = 64
SSD_HEADS = SSD_WIDTH // SSD_HEAD_DIM
SSD_GROUPS = 2
SSD_STATE = 128
SSD_CHUNK = 128
SSD_CONV_CH = SSD_WIDTH + 2 * SSD_GROUPS * SSD_STATE
DA_WIDTH = D_MODEL // 2
DA_HEAD_DIM = 64
DA_HEADS = DA_WIDTH // (2 * DA_HEAD_DIM)
DA_LAMBDA_INIT = 0.8 - 0.6 * math.exp(-0.3 * 0)
ROPE_BASE = 10000.0
ROPE_FREQS = DA_HEAD_DIM // 4
L0_SPLITS = (SSD_WIDTH, SSD_WIDTH, SSD_GROUPS * SSD_STATE, SSD_GROUPS * SSD_STATE,
             SSD_HEADS, SSD_HEADS, DA_WIDTH, DA_WIDTH, DA_WIDTH)
HY_WIDTH = D_MODEL
HY_BANDS = 16
HY_FAST_DECAY = 0.3
HY_SLOW_DECAY = 1.5
HY_TARGET = 1e-2
N_EXPERTS = 32
TOP_K = 4
SWIGLU_LIMIT = 7.0
SWIGLU_ALPHA = 1.702

T_CTX = BATCH * SEQ
T_DEC = DEC_BATCH * DEC_SEQ
T_ALL = T_CTX + T_DEC
N_COND = 1 + DEC_BATCH
MOD_ROWS = 8

V7X_LANES = 128
V7X_MXU_DIM = 256
V7X_VMEM_BYTES = 64 * 1024 * 1024
VMEM_LIMIT = V7X_VMEM_BYTES - 8 * 1024 * 1024

MOE_TM = 512
MOE_TN = 2048
MOE_ROWS = T_ALL * TOP_K + N_EXPERTS * MOE_TM
MOE_BLOCKS = MOE_ROWS // MOE_TM


def _cparams(n_grid):
    return pltpu.CompilerParams(dimension_semantics=("arbitrary",) * n_grid,
                                vmem_limit_bytes=VMEM_LIMIT)


def _cond_of_row(row):
    return jnp.where(row < T_CTX, 0, (row - T_CTX) // DEC_SEQ + 1)


def _split_bf16(x):
    hi = x.astype(jnp.bfloat16)
    lo = (x - hi.astype(jnp.float32)).astype(jnp.bfloat16)
    return hi, lo


def _dot3(a, w):
    a_hi, a_lo = _split_bf16(a)
    w_hi, w_lo = _split_bf16(w)
    f32 = jnp.float32
    return (jnp.dot(a_hi, w_hi, preferred_element_type=f32)
            + jnp.dot(a_lo, w_hi, preferred_element_type=f32)
            + jnp.dot(a_hi, w_lo, preferred_element_type=f32))


def _adaln_kernel(c_ref, w_ref, b_ref, o_ref):
    c = c_ref[...]
    a = c * jax.nn.sigmoid(c)
    o_ref[...] = _dot3(a, w_ref[...]) + b_ref[...]


def _adaln(cond8, w, b):
    n = w.shape[1]
    tn = 512
    out = pl.pallas_call(
        _adaln_kernel,
        grid=(n // tn,),
        in_specs=[pl.BlockSpec((MOD_ROWS, D_MODEL), lambda j: (0, 0)),
                  pl.BlockSpec((D_MODEL, tn), lambda j: (0, j)),
                  pl.BlockSpec((1, tn), lambda j: (0, j))],
        out_specs=pl.BlockSpec((MOD_ROWS, tn), lambda j: (0, j)),
        out_shape=jax.ShapeDtypeStruct((MOD_ROWS, n), jnp.float32),
        compiler_params=_cparams(1),
        name="adaln",
    )(cond8, w, b.reshape(1, n))
    m = out[:N_COND].reshape(N_COND, 6, D_MODEL)
    return jnp.pad(m, ((0, 0), (0, MOD_ROWS - 6), (0, 0)))


HALF_D = D_MODEL // 2


def _pack_rows(x):
    as_bits = lambda v: lax.bitcast_convert_type(v.astype(jnp.bfloat16).astype(jnp.float32),
                                                 jnp.uint32)
    lo, hi = as_bits(x[:, :HALF_D]), as_bits(x[:, HALF_D:])
    return (lo >> 16) | (hi & jnp.uint32(0xFFFF0000))


def _unpack_rows(p):
    lo = lax.bitcast_convert_type(p << 16, jnp.float32)
    hi = lax.bitcast_convert_type(p & jnp.uint32(0xFFFF0000), jnp.float32)
    return lo, hi


def _norm_rows(x, g):
    var = jnp.mean(x * x, axis=-1, keepdims=True)
    return x * lax.rsqrt(var + NORM_EPS) * g


def _normmod_kernel(h_ref, g_ref, mod_ref, o_ref, *, sh_row):
    y = _norm_rows(h_ref[...], g_ref[...])
    y = y * (1.0 + mod_ref[sh_row + 1:sh_row + 2, :]) + mod_ref[sh_row:sh_row + 1, :]
    o_ref[...] = y.astype(o_ref.dtype)


def _normmod_router_kernel(h_ref, g_ref, mod_ref, rw_ref, rb_ref, o_ref, lg_ref, *, sh_row):
    y = _norm_rows(h_ref[...], g_ref[...])
    y = y * (1.0 + mod_ref[sh_row + 1:sh_row + 2, :]) + mod_ref[sh_row:sh_row + 1, :]
    o_ref[...] = _pack_rows(y)
    lg_ref[...] = _dot3(y, rw_ref[...]) + rb_ref[...]


def _plain_norm_kernel(h_ref, g_ref, o_ref):
    o_ref[...] = _norm_rows(h_ref[...], g_ref[...])


def _normmod(h, g, mod, sh_row, router=None):
    tm = 512
    grid = (T_ALL // tm,)
    h_spec = pl.BlockSpec((tm, D_MODEL), lambda i: (i, 0))
    g_spec = pl.BlockSpec((1, D_MODEL), lambda i: (0, 0))
    mod_spec = pl.BlockSpec((None, MOD_ROWS, D_MODEL), lambda i: (_cond_of_row(i * tm), 0, 0))
    o_spec = pl.BlockSpec((tm, D_MODEL), lambda i: (i, 0))
    o_shape = jax.ShapeDtypeStruct((T_ALL, D_MODEL), jnp.bfloat16)
    if router is None:
        return pl.pallas_call(
            functools.partial(_normmod_kernel, sh_row=sh_row),
            grid=grid, in_specs=[h_spec, g_spec, mod_spec], out_specs=o_spec,
            out_shape=o_shape, compiler_params=_cparams(1), name="normmod",
        )(h, g.reshape(1, D_MODEL), mod)
    rw, rb = router
    return pl.pallas_call(
        functools.partial(_normmod_router_kernel, sh_row=sh_row),
        grid=grid,
        in_specs=[h_spec, g_spec, mod_spec,
                  pl.BlockSpec((D_MODEL, V7X_LANES), lambda i: (0, 0)),
                  pl.BlockSpec((1, V7X_LANES), lambda i: (0, 0))],
        out_specs=[pl.BlockSpec((tm, HALF_D), lambda i: (i, 0)),
                   pl.BlockSpec((tm, V7X_LANES), lambda i: (i, 0))],
        out_shape=[jax.ShapeDtypeStruct((T_ALL, HALF_D), jnp.uint32),
                   jax.ShapeDtypeStruct((T_ALL, V7X_LANES), jnp.float32)],
        compiler_params=_cparams(1), name="normmod_router",
    )(h, g.reshape(1, D_MODEL), mod, rw, rb)


def _final_norm(h, g, row0, n_rows):
    tm = 512
    blk0 = row0 // tm
    return pl.pallas_call(
        _plain_norm_kernel,
        grid=(n_rows // tm,),
        in_specs=[pl.BlockSpec((tm, D_MODEL), lambda i: (i + blk0, 0)),
                  pl.BlockSpec((1, D_MODEL), lambda i: (0, 0))],
        out_specs=pl.BlockSpec((tm, D_MODEL), lambda i: (i, 0)),
        out_shape=jax.ShapeDtypeStruct((n_rows, D_MODEL), jnp.float32),
        compiler_params=_cparams(1), name="final_norm",
    )(h, g.reshape(1, D_MODEL))


def _mm_kernel(*refs, two_a, has_bias, gated, g_row):
    it = iter(refs)
    a_ref = next(it)
    a2_ref = next(it) if two_a else None
    w_ref = next(it)
    b_ref = next(it) if has_bias else None
    res_ref, mod_ref = (next(it), next(it)) if gated else (None, None)
    o_ref, wbf_ref = next(it), next(it)

    @pl.when(pl.program_id(1) == 0)
    def _():
        wbf_ref[...] = w_ref[...].astype(jnp.bfloat16)

    k1 = a_ref.shape[1]
    acc = jnp.dot(a_ref[...], wbf_ref[:k1, :], preferred_element_type=jnp.float32)
    if two_a:
        acc = acc + jnp.dot(a2_ref[...], wbf_ref[k1:, :], preferred_element_type=jnp.float32)
    if has_bias:
        acc = acc + b_ref[...]
    if gated:
        acc = res_ref[...] + mod_ref[g_row:g_row + 1, :] * acc
    o_ref[...] = acc.astype(o_ref.dtype)


def _matmul(a, w, n_cols, *, a2=None, col_blk0=0, bias=None, gated=None,
            out_dtype=jnp.float32, tm=1024, tn=1024, name="matmul"):
    m, k1 = a.shape
    k = w.shape[0]
    assert m % tm == 0 and n_cols % tn == 0
    in_specs = [pl.BlockSpec((tm, k1), lambda j, i: (i, 0))]
    args = [a]
    if a2 is not None:
        assert k1 + a2.shape[1] == k
        in_specs.append(pl.BlockSpec((tm, k - k1), lambda j, i: (i, 0)))
        args.append(a2)
    in_specs.append(pl.BlockSpec((k, tn), lambda j, i: (0, j + col_blk0)))
    args.append(w)
    if bias is not None:
        in_specs.append(pl.BlockSpec((1, tn), lambda j, i: (0, j + col_blk0)))
        args.append(bias.reshape(1, -1))
    g_row = 0
    if gated is not None:
        res, mod, g_row = gated
        in_specs.append(pl.BlockSpec((tm, tn), lambda j, i: (i, j)))
        in_specs.append(pl.BlockSpec((None, MOD_ROWS, tn),
                                     lambda j, i: (_cond_of_row(i * tm), 0, j)))
        args += [res, mod]
    return pl.pallas_call(
        functools.partial(_mm_kernel, two_a=a2 is not None, has_bias=bias is not None,
                          gated=gated is not None, g_row=g_row),
        grid=(n_cols // tn, m // tm),
        in_specs=in_specs,
        out_specs=pl.BlockSpec((tm, tn), lambda j, i: (i, j)),
        out_shape=jax.ShapeDtypeStruct((m, n_cols), out_dtype),
        scratch_shapes=[pltpu.VMEM((k, tn), jnp.bfloat16)],
        compiler_params=_cparams(2), name=name,
    )(*args)


def _gmm_kernel(be_ref, half_ref, nv_ref, x_ref, w_ref, b_ref, *rest, swiglu):
    if swiglu:
        sel_ref, o_ref, wbf_ref = rest
    else:
        o_ref, wbf_ref = rest
    i = pl.program_id(1)
    prev = be_ref[jnp.maximum(i - 1, 0)]

    @pl.when((i == 0) | (be_ref[i] != prev))
    def _():
        wbf_ref[...] = w_ref[...].astype(jnp.bfloat16)

    def compute(n_rows):
        f32, bf = jnp.float32, jnp.bfloat16
        if n_rows < MOE_TM:
            o_ref[n_rows:, :] = jnp.zeros((MOE_TM - n_rows, o_ref.shape[1]), o_ref.dtype)
        if not swiglu:
            acc = jnp.dot(x_ref[0:n_rows, :], wbf_ref[...], preferred_element_type=f32) + b_ref[...]
            o_ref[0:n_rows, :] = acc.astype(o_ref.dtype)
            return
        x_lo, x_hi = _unpack_rows(x_ref[0:n_rows, :])
        acc = (jnp.dot(x_lo.astype(bf), wbf_ref[:HALF_D, :], preferred_element_type=f32)
               + jnp.dot(x_hi.astype(bf), wbf_ref[HALF_D:, :], preferred_element_type=f32)
               + b_ref[...])
        tn = acc.shape[1]
        wide, half = V7X_MXU_DIM, V7X_MXU_DIM // 2
        gate = jnp.minimum(acc, SWIGLU_LIMIT)
        up = pltpu.roll(jnp.clip(acc, -SWIGLU_LIMIT, SWIGLU_LIMIT), tn - 1, 1)
        act = (up + 1.0) * (gate * jax.nn.sigmoid(SWIGLU_ALPHA * gate))
        lane = lax.broadcasted_iota(jnp.int32, act.shape, 1)
        act = jnp.where(lane % 2 == 0, act, 0.0).astype(jnp.bfloat16)
        for g in range(tn // wide):
            o_ref[0:n_rows, g * half:(g + 1) * half] = jnp.dot(
                act[:, g * wide:(g + 1) * wide], sel_ref[...],
                preferred_element_type=f32).astype(o_ref.dtype)

    valid = i < nv_ref[0]

    @pl.when(valid & (half_ref[i] == 0))
    def _():
        compute(MOE_TM)

    @pl.when(valid & (half_ref[i] != 0))
    def _():
        compute(MOE_TM // 2)

    @pl.when(i >= nv_ref[0])
    def _():
        o_ref[...] = jnp.zeros_like(o_ref)


def _grouped_matmul(x, w, b, block_e, block_half, n_valid, *, swiglu, tn, out_dtype, name):
    rows = x.shape[0]
    k, n = w.shape[1:]
    if swiglu:
        tn_out, n_out = tn // 2, n // 2
    else:
        tn_out, n_out = tn, n
    in_specs = [pl.BlockSpec((MOE_TM, x.shape[1]), lambda j, i, be, hf, nv: (i, 0)),
                pl.BlockSpec((None, k, tn), lambda j, i, be, hf, nv: (be[i], 0, j)),
                pl.BlockSpec((None, 1, tn), lambda j, i, be, hf, nv: (be[i], 0, j))]
    args = [x, w, b.reshape(N_EXPERTS, 1, n)]
    if swiglu:
        sel = np.zeros((V7X_MXU_DIM, V7X_MXU_DIM // 2), np.float32)
        sel[2 * np.arange(V7X_MXU_DIM // 2), np.arange(V7X_MXU_DIM // 2)] = 1.0
        in_specs.append(pl.BlockSpec(sel.shape, lambda j, i, be, hf, nv: (0, 0)))
        args.append(jnp.asarray(sel, jnp.bfloat16))
    return pl.pallas_call(
        functools.partial(_gmm_kernel, swiglu=swiglu),
        grid_spec=pltpu.PrefetchScalarGridSpec(
            num_scalar_prefetch=3,
            grid=(n // tn, rows // MOE_TM),
            in_specs=in_specs,
            out_specs=pl.BlockSpec((MOE_TM, tn_out), lambda j, i, be, hf, nv: (i, j)),
            scratch_shapes=[pltpu.VMEM((k, tn), jnp.bfloat16)]),
        out_shape=jax.ShapeDtypeStruct((rows, n_out), out_dtype),
        compiler_params=_cparams(2), name=name,
    )(block_e, block_half, n_valid, *args)


DISPATCH_TM = 512


def _dispatch_kernel(pos_ref, hp_ref, x_init, x_hbm, sem):
    del x_init

    def body(r, carry):
        for k in range(TOP_K):
            pltpu.make_async_copy(hp_ref.at[pl.ds(r, 1)],
                                  x_hbm.at[pl.ds(pos_ref[0, r * TOP_K + k], 1)], sem).start()
        return carry
    lax.fori_loop(0, DISPATCH_TM, body, 0, unroll=4)
    for _ in range(TOP_K):
        pltpu.make_async_copy(hp_ref, x_hbm.at[pl.ds(0, DISPATCH_TM)], sem).wait()


def _moe_dispatch(hp, pos):
    t = hp.shape[0]
    nblk = t // DISPATCH_TM
    pos3 = pos.reshape(nblk, 1, DISPATCH_TM * TOP_K)
    return pl.pallas_call(
        _dispatch_kernel,
        grid=(nblk,),
        in_specs=[pl.BlockSpec((None, 1, DISPATCH_TM * TOP_K), lambda i: (i, 0, 0),
                               memory_space=pltpu.SMEM),
                  pl.BlockSpec((DISPATCH_TM, HALF_D), lambda i: (i, 0)),
                  pl.BlockSpec(memory_space=pl.ANY)],
        out_specs=pl.BlockSpec(memory_space=pl.ANY),
        out_shape=jax.ShapeDtypeStruct((MOE_ROWS, HALF_D), jnp.uint32),
        scratch_shapes=[pltpu.SemaphoreType.DMA(())],
        input_output_aliases={2: 0},
        compiler_params=_cparams(1), name="moe_dispatch",
    )(pos3, hp, jnp.zeros((MOE_ROWS, HALF_D), jnp.uint32))


COMBINE_TM = 256


def _combine_kernel(pos_ref, pos_next_ref, h_ref, gates_ref, mod_ref, y_hbm, o_ref, buf, sem,
                    *, g_row):
    i = pl.program_id(0)
    slot = i % 2
    n_rows = TOP_K * COMBINE_TM

    def issue(idx_ref, s):
        def body(r, carry):
            pltpu.make_async_copy(y_hbm.at[pl.ds(idx_ref[0, r], 1)],
                                  buf.at[s, pl.ds(r, 1)], sem.at[s]).start()
            return carry
        lax.fori_loop(0, n_rows, body, 0, unroll=8)

    @pl.when(i == 0)
    def _():
        issue(pos_ref, 0)

    @pl.when(i + 1 < pl.num_programs(0))
    def _():
        issue(pos_next_ref, 1 - slot)

    pltpu.make_async_copy(y_hbm.at[pl.ds(0, n_rows)], buf.at[slot], sem.at[slot]).wait()
    acc = gates_ref[:, 0:1] * buf[slot, 0:COMBINE_TM, :]
    for k in range(1, TOP_K):
        acc = acc + gates_ref[:, k:k + 1] * buf[slot, k * COMBINE_TM:(k + 1) * COMBINE_TM, :]
    o_ref[...] = h_ref[...] + mod_ref[g_row:g_row + 1, :] * acc


def _moe_combine(h, y_rows, pos, gates_pad, mod, g_row):
    t = h.shape[0]
    nblk = t // COMBINE_TM
    pos3 = pos.reshape(nblk, COMBINE_TM, TOP_K).transpose(0, 2, 1).reshape(nblk, 1, TOP_K * COMBINE_TM)
    smem_blk = lambda f: pl.BlockSpec((None, 1, TOP_K * COMBINE_TM), f, memory_space=pltpu.SMEM)
    return pl.pallas_call(
        functools.partial(_combine_kernel, g_row=g_row),
        grid=(nblk,),
        in_specs=[smem_blk(lambda i: (i, 0, 0)),
                  smem_blk(lambda i: (jnp.minimum(i + 1, nblk - 1), 0, 0)),
                  pl.BlockSpec((COMBINE_TM, D_MODEL), lambda i: (i, 0)),
                  pl.BlockSpec((COMBINE_TM, V7X_LANES), lambda i: (i, 0)),
                  pl.BlockSpec((None, MOD_ROWS, D_MODEL),
                               lambda i: (_cond_of_row(i * COMBINE_TM), 0, 0)),
                  pl.BlockSpec(memory_space=pl.ANY)],
        out_specs=pl.BlockSpec((COMBINE_TM, D_MODEL), lambda i: (i, 0)),
        out_shape=jax.ShapeDtypeStruct((t, D_MODEL), jnp.float32),
        scratch_shapes=[pltpu.VMEM((2, TOP_K * COMBINE_TM, D_MODEL), jnp.float32),
                        pltpu.SemaphoreType.DMA((2,))],
        compiler_params=_cparams(1), name="moe_combine",
    )(pos3, pos3, h, gates_pad, mod, y_rows)


ROUTE_TM = 512


def _route_kernel(lg_ref, idx_ref, gate_ref, rank_ref, cnt_ref, carry):
    f32 = jnp.float32
    i = pl.program_id(0)

    @pl.when(i == 0)
    def _():
        carry[...] = jnp.zeros_like(carry)

    lg = lg_ref[...]
    lane = lax.broadcasted_iota(jnp.int32, lg.shape, 1)
    lane_f = lane.astype(f32)
    vals, ids, hots = [], [], []
    for _ in range(TOP_K):
        m = jnp.max(lg, axis=1, keepdims=True)
        first = jnp.min(jnp.where(lg == m, lane_f, float(V7X_LANES)), axis=1, keepdims=True)
        hot = lane_f == first
        vals.append(m)
        ids.append(first)
        hots.append(hot)
        lg = jnp.where(hot, -jnp.inf, lg)
    exps = [jnp.exp(v - vals[0]) for v in vals]
    denom = exps[0]
    for e in exps[1:]:
        denom = denom + e
    chosen = jnp.zeros(lg.shape, f32)
    for hot in hots:
        chosen = chosen + jnp.where(hot, 1.0, 0.0)
    ri = lax.broadcasted_iota(jnp.int32, (ROUTE_TM, ROUTE_TM), 0)
    ci = lax.broadcasted_iota(jnp.int32, (ROUTE_TM, ROUTE_TM), 1)
    tri = jnp.where(ci <= ri, 1.0, 0.0).astype(jnp.bfloat16)
    upto = jnp.dot(tri, chosen.astype(jnp.bfloat16), preferred_element_type=f32)
    before = carry[...] + upto - chosen
    idx_out = jnp.zeros(lg.shape, f32)
    gate_out = jnp.zeros(lg.shape, f32)
    rank_out = jnp.zeros(lg.shape, f32)
    for k in range(TOP_K):
        rank_k = jnp.sum(jnp.where(hots[k], before, 0.0), axis=1, keepdims=True)
        idx_out = jnp.where(lane == k, ids[k], idx_out)
        gate_out = jnp.where(lane == k, exps[k] / denom, gate_out)
        rank_out = jnp.where(lane == k, rank_k, rank_out)
    idx_ref[...] = idx_out.astype(jnp.int32)
    gate_ref[...] = gate_out
    rank_ref[...] = rank_out.astype(jnp.int32)
    carry[...] = carry[...] + upto[ROUTE_TM - 1:ROUTE_TM, :]
    cnt_ref[...] = carry[...]


def _route(logits):
    t = logits.shape[0]
    blk = pl.BlockSpec((ROUTE_TM, V7X_LANES), lambda i: (i, 0))
    row = pl.BlockSpec((1, V7X_LANES), lambda i: (0, 0))
    sds = lambda dt: jax.ShapeDtypeStruct((t, V7X_LANES), dt)
    return pl.pallas_call(
        _route_kernel,
        grid=(t // ROUTE_TM,),
        in_specs=[blk],
        out_specs=[blk, blk, blk, row],
        out_shape=[sds(jnp.int32), sds(jnp.float32), sds(jnp.int32),
                   jax.ShapeDtypeStruct((1, V7X_LANES), jnp.float32)],
        scratch_shapes=[pltpu.VMEM((1, V7X_LANES), jnp.float32)],
        compiler_params=_cparams(1), name="moe_route",
    )(logits)


def _moe(h, hn, logits, w_gu, b_gu, w_dn, b_dn, mod):
    t = hn.shape[0]
    idx, gates_pad, rank, counts = _route(logits)
    top_idx = idx[:, :TOP_K]
    counts = counts[0, :N_EXPERTS].astype(jnp.int32)
    padded = (counts + MOE_TM - 1) // MOE_TM * MOE_TM
    padded_end = jnp.cumsum(padded)
    padded_start = padded_end - padded
    pos = padded_start[top_idx] + rank[:, :TOP_K]
    block_e = jnp.minimum(
        jnp.searchsorted(padded_end, jnp.arange(MOE_BLOCKS) * MOE_TM, side='right'),
        N_EXPERTS - 1).astype(jnp.int32)
    n_valid = (padded_end[-1] // MOE_TM).astype(jnp.int32).reshape(1)
    rows_used = (padded_start + counts)[block_e] - jnp.arange(MOE_BLOCKS) * MOE_TM
    block_half = (rows_used <= MOE_TM // 2).astype(jnp.int32)

    x_rows = _moe_dispatch(hn, pos)
    act = _grouped_matmul(x_rows, w_gu, b_gu, block_e, block_half, n_valid, swiglu=True, tn=MOE_TN,
                          out_dtype=jnp.bfloat16, name="moe_gate_up")
    y_rows = _grouped_matmul(act, w_dn, b_dn, block_e, block_half, n_valid, swiglu=False, tn=MOE_TN,
                             out_dtype=jnp.float32, name="moe_down")
    return _moe_combine(h, y_rows, pos, gates_pad, mod, 5)


CONV_TM = 256
N_CHUNKS = T_ALL // SSD_CHUNK
CTX_CHUNKS = T_CTX // SSD_CHUNK
N_SEQS = BATCH + DEC_BATCH


def _seq_len_of_row(row):
    return jnp.where(row < T_CTX, SEQ, DEC_SEQ)


def _seq_pos_of_row(row):
    return jnp.where(row < T_CTX, row % SEQ, (row - T_CTX) % DEC_SEQ)


def _seq_of_chunk(n):
    return jnp.where(n < CTX_CHUNKS, n // (SEQ // SSD_CHUNK),
                     BATCH + (n - CTX_CHUNKS) // (DEC_SEQ // SSD_CHUNK))


def _dwconv_kernel(x_ref, prev_ref, next_ref, w_ref, b_ref, o_ref, *, width, silu):
    row0 = pl.program_id(1) * CONV_TM
    pos0 = _seq_pos_of_row(row0)
    has_prev = pos0 != 0
    has_next = pos0 + CONV_TM != _seq_len_of_row(row0)
    x = x_ref[...]
    rid = lax.broadcasted_iota(jnp.int32, x.shape, 0)
    prev = jnp.where(has_prev, prev_ref[7:8, :], 0.0)
    nxt0 = jnp.where(has_next, next_ref[0:1, :], 0.0)
    acc = w_ref[1:2, :] * x + b_ref[...]
    xm1 = jnp.where(rid == 0, prev, pltpu.roll(x, 1, 0))
    acc = acc + w_ref[0:1, :] * xm1
    xp1 = jnp.where(rid == CONV_TM - 1, nxt0, pltpu.roll(x, CONV_TM - 1, 0))
    acc = acc + w_ref[2:3, :] * xp1
    if width == 4:
        nxt1 = jnp.where(has_next, next_ref[1:2, :], 0.0)
        xp2 = pltpu.roll(x, CONV_TM - 2, 0)
        xp2 = jnp.where(rid == CONV_TM - 2, nxt0, jnp.where(rid == CONV_TM - 1, nxt1, xp2))
        acc = acc + w_ref[3:4, :] * xp2
    if silu:
        acc = acc * jax.nn.sigmoid(acc)
    o_ref[...] = acc


def _dwconv(x, w, b, *, in_col_blocks, tc, silu, name):
    width = w.shape[0]
    nblk = len(in_col_blocks)
    assert w.shape[1] == nblk * tc and width in (3, 4)
    cols = np.asarray(in_col_blocks, np.int32)
    if np.array_equal(cols, np.arange(nblk)):
        col = lambda j: j
    else:
        assert nblk == 3 and cols[0] == 0 and cols[1] == 1
        col = lambda j: jnp.where(j < 2, j, int(cols[2]))
    w8 = jnp.pad(w, ((0, 8 - width), (0, 0)))
    sub = CONV_TM // 8
    last8 = T_ALL // 8 - 1
    return pl.pallas_call(
        functools.partial(_dwconv_kernel, width=width, silu=silu),
        grid=(nblk, T_ALL // CONV_TM),
        in_specs=[pl.BlockSpec((CONV_TM, tc), lambda j, i: (i, col(j))),
                  pl.BlockSpec((8, tc), lambda j, i: (jnp.maximum(i * sub - 1, 0), col(j))),
                  pl.BlockSpec((8, tc), lambda j, i: (jnp.minimum((i + 1) * sub, last8), col(j))),
                  pl.BlockSpec((8, tc), lambda j, i: (0, j)),
                  pl.BlockSpec((1, tc), lambda j, i: (0, j))],
        out_specs=pl.BlockSpec((CONV_TM, tc), lambda j, i: (i, j)),
        out_shape=jax.ShapeDtypeStruct((T_ALL, nblk * tc), jnp.float32),
        compiler_params=_cparams(2), name=name,
    )(x, x, x, w8, b.reshape(1, -1))


def _softplus(x):
    return jnp.maximum(x, 0.0) + jnp.log(1.0 + jnp.exp(-jnp.abs(x)))


def _ssd_kernel(*refs, reverse):
    if reverse:
        (xbc_ref, zdt_ref, bias_ref, alog_ref, h0_ref, yf_ref, z_ref, dskip_ref, gn_ref,
         y_ref, st_ref, s_scr, y_scr) = refs
    else:
        xbc_ref, zdt_ref, bias_ref, alog_ref, h0_ref, y_ref, st_ref, s_scr = refs
    f32, bf = jnp.float32, jnp.bfloat16
    q = SSD_CHUNK
    step = pl.program_id(0)
    n = (N_CHUNKS - 1 - step) if reverse else step
    row0 = n * q
    pos0 = _seq_pos_of_row(row0)
    seq_len = _seq_len_of_row(row0)
    starts_seq = (pos0 + q == seq_len) if reverse else (pos0 == 0)
    ends_seq = (pos0 == 0) if reverse else (pos0 + q == seq_len)

    @pl.when(starts_seq)
    def _():
        s_scr[...] = jnp.where(row0 < T_CTX, 0.0, h0_ref[...])

    ri = lax.broadcasted_iota(jnp.int32, (q, q), 0)
    ci = lax.broadcasted_iota(jnp.int32, (q, q), 1)
    mask = (ci >= ri) if reverse else (ci <= ri)
    lane_lo = ci < SSD_HEAD_DIM
    row_lo = ri < SSD_HEAD_DIM

    dt = _softplus(zdt_ref[...] + bias_ref[...])
    da = dt * (-jnp.exp(alog_ref[...]))
    ones = jnp.where(mask, 1.0, 0.0).astype(bf)
    d1 = da.astype(bf)
    r1 = da - d1.astype(f32)
    d2 = r1.astype(bf)
    d3 = (r1 - d2.astype(f32)).astype(bf)
    cs = (jnp.dot(ones, d1, preferred_element_type=f32)
          + jnp.dot(ones, d2, preferred_element_type=f32)
          + jnp.dot(ones, d3, preferred_element_type=f32))
    cs_t = cs.T
    tot = cs[0:1, :] if reverse else cs[q - 1:q, :]
    exp_cs = jnp.exp(cs)
    d_end = jnp.exp(tot - cs)
    exp_tot = jnp.exp(tot)

    def pair_cols(v, p):
        return jnp.where(lane_lo, v[:, 2 * p:2 * p + 1], v[:, 2 * p + 1:2 * p + 2])

    dn_t = (((1,), (1,)), ((), ()))
    for g in range(SSD_GROUPS):
        b_g = xbc_ref[:, SSD_WIDTH + g * SSD_STATE:SSD_WIDTH + (g + 1) * SSD_STATE].astype(bf)
        c_off = SSD_WIDTH + SSD_GROUPS * SSD_STATE
        c_g = xbc_ref[:, c_off + g * SSD_STATE:c_off + (g + 1) * SSD_STATE].astype(bf)
        cb = lax.dot_general(c_g, b_g, dn_t, preferred_element_type=f32)
        pairs_per_group = SSD_HEADS // SSD_GROUPS // 2
        for p in range(g * pairs_per_group, (g + 1) * pairs_per_group):
            lo, hi = p * 2 * SSD_HEAD_DIM, (p + 1) * 2 * SSD_HEAD_DIM
            xs = xbc_ref[:, lo:hi]
            xdt = xs * pair_cols(dt, p)
            y = jnp.zeros((q, 2 * SSD_HEAD_DIM), f32)
            for h, keep in ((2 * p, lane_lo), (2 * p + 1, ~lane_lo)):
                seg = cs[:, h:h + 1] - cs_t[h:h + 1, :]
                w = jnp.where(mask, jnp.exp(jnp.minimum(seg, 0.0)), 0.0) * cb
                y = y + jnp.dot(w.astype(bf), jnp.where(keep, xdt, 0.0).astype(bf),
                                preferred_element_type=f32)
            s_pair = s_scr[lo:hi, :]
            y_off = lax.dot_general(c_g, s_pair.astype(bf), dn_t, preferred_element_type=f32)
            y = y + y_off * pair_cols(exp_cs, p)
            st = jnp.dot((xdt * pair_cols(d_end, p)).T.astype(bf), b_g,
                         preferred_element_type=f32)
            keep_s = jnp.where(row_lo, exp_tot[:, 2 * p:2 * p + 1], exp_tot[:, 2 * p + 1:2 * p + 2])
            s_scr[lo:hi, :] = s_pair * keep_s + st
            if reverse:
                y_scr[:, lo:hi] = y + yf_ref[:, lo:hi] + dskip_ref[:, lo:hi] * xs
            else:
                y_ref[:, lo:hi] = y

    if reverse:
        zz = z_ref[...]
        yy = y_scr[...] * (zz * jax.nn.sigmoid(zz))
        gw = SSD_WIDTH // SSD_GROUPS
        for g in range(SSD_GROUPS):
            y_g = yy[:, g * gw:(g + 1) * gw]
            y_g = y_g * lax.rsqrt(jnp.mean(y_g * y_g, axis=-1, keepdims=True) + NORM_EPS)
            y_ref[:, g * gw:(g + 1) * gw] = (y_g * gn_ref[:, g * gw:(g + 1) * gw]).astype(y_ref.dtype)

    @pl.when(ends_seq)
    def _():
        st_ref[...] = s_scr[...]


def _ssd(xbc, z_dt, dt_bias, a_log, h0, *, reverse, finish=None):
    q = SSD_CHUNK
    d = 1 if reverse else 0
    chunk = (lambda s: N_CHUNKS - 1 - s) if reverse else (lambda s: s)
    pad_row = lambda v: jnp.pad(v, (0, V7X_LANES - SSD_HEADS)).reshape(1, V7X_LANES)
    in_specs = [pl.BlockSpec((q, SSD_CONV_CH), lambda s: (chunk(s), 0)),
                pl.BlockSpec((q, V7X_LANES), lambda s: (chunk(s), d)),
                pl.BlockSpec((1, V7X_LANES), lambda s: (0, 0)),
                pl.BlockSpec((1, V7X_LANES), lambda s: (0, 0)),
                pl.BlockSpec((None, SSD_WIDTH, SSD_STATE),
                             lambda s: (jnp.maximum(_seq_of_chunk(chunk(s)) - BATCH, 0), 0, 0))]
    args = [xbc, z_dt, pad_row(dt_bias), pad_row(a_log), h0]
    scratch = [pltpu.VMEM((SSD_WIDTH, SSD_STATE), jnp.float32)]
    y_dtype = jnp.float32
    if reverse:
        y_fwd, z_ssd, dskip_row, gn_row = finish
        in_specs += [pl.BlockSpec((q, SSD_WIDTH), lambda s: (chunk(s), 0)),
                     pl.BlockSpec((q, SSD_WIDTH), lambda s: (chunk(s), 1)),
                     pl.BlockSpec((1, SSD_WIDTH), lambda s: (0, 0)),
                     pl.BlockSpec((1, SSD_WIDTH), lambda s: (0, 0))]
        args += [y_fwd, z_ssd, dskip_row, gn_row]
        scratch.append(pltpu.VMEM((q, SSD_WIDTH), jnp.float32))
        y_dtype = jnp.bfloat16
    return pl.pallas_call(
        functools.partial(_ssd_kernel, reverse=reverse),
        grid=(N_CHUNKS,),
        in_specs=in_specs,
        out_specs=[pl.BlockSpec((q, SSD_WIDTH), lambda s: (chunk(s), 0)),
                   pl.BlockSpec((None, SSD_WIDTH, SSD_STATE),
                                lambda s: (_seq_of_chunk(chunk(s)), 0, 0))],
        out_shape=[jax.ShapeDtypeStruct((T_ALL, SSD_WIDTH), y_dtype),
                   jax.ShapeDtypeStruct((N_SEQS, SSD_WIDTH, SSD_STATE), jnp.float32)],
        scratch_shapes=scratch,
        compiler_params=_cparams(1), name="ssd_bwd" if reverse else "ssd_fwd",
    )(*args)


ATT_TQ = 256
HEAD_W = 2 * DA_HEAD_DIM


def _rope_tables():
    t = jnp.arange(DEC_SEQ)
    pos = jnp.stack([t // GRID_W, t % GRID_W], axis=-1).astype(jnp.float32)
    inv = 1.0 / (ROPE_BASE ** (jnp.arange(ROPE_FREQS, dtype=jnp.float32) / ROPE_FREQS))
    ang = pos[:, :, None] * inv
    cos, sin = jnp.cos(ang), jnp.sin(ang)
    zero = jnp.zeros_like(sin)
    lanes = lambda first, second: jnp.tile(
        jnp.stack([first, second], axis=2).reshape(DEC_SEQ, DA_HEAD_DIM), (1, 2))
    return lanes(cos, cos), lanes(-sin, zero), lanes(zero, sin)


def _attn_kernel(lam_ref, *refs, has_ctx):
    if has_ctx:
        (q_ref, k_ref, v_ref, kc_ref, vc_ref, cq, saq, sbq, ck, sak, sbk, sw_ref, o_ref) = refs
    else:
        q_ref, k_ref, v_ref, sw_ref, o_ref = refs
    f32, bf = jnp.float32, jnp.bfloat16
    half = ROPE_FREQS
    scale = DA_HEAD_DIM ** -0.5

    def rope(x, c, sa, sb):
        return (x * c[...] + pltpu.roll(x, HEAD_W - half, 1) * sa[...]
                + pltpu.roll(x, half, 1) * sb[...])

    for hd in range(q_ref.shape[1] // HEAD_W):
        cols = slice(hd * HEAD_W, (hd + 1) * HEAD_W)
        q, k = q_ref[:, cols], k_ref[:, cols]
        if has_ctx:
            q, k = rope(q, cq, saq, sbq), rope(k, ck, sak, sbk)
        lane = lax.broadcasted_iota(jnp.int32, q.shape, 1)
        kb, vb = k.astype(bf), v_ref[:, cols].astype(bf)
        if has_ctx:
            kb = jnp.concatenate([kc_ref[:, cols].astype(bf), kb], axis=0)
            vb = jnp.concatenate([vc_ref[:, cols].astype(bf), vb], axis=0)

        def exps(qm):
            s = lax.dot_general(qm.astype(bf), kb, (((1,), (1,)), ((), ())),
                                preferred_element_type=f32) * scale
            e = jnp.exp(s - jnp.max(s, axis=-1, keepdims=True))
            return e, 1.0 / jnp.sum(e, axis=-1, keepdims=True)

        e0, r0 = exps(jnp.where(lane < DA_HEAD_DIM, q, 0.0))
        e1, r1 = exps(jnp.where(lane >= DA_HEAD_DIM, q, 0.0))
        w = e0 * r0 - e1 * (lam_ref[0] * r1)
        o = jnp.dot(w.astype(bf), vb, preferred_element_type=f32)
        o = o * lax.rsqrt(jnp.mean(o * o, axis=-1, keepdims=True) + NORM_EPS) * sw_ref[...]
        o_ref[:, cols] = (o * (1.0 - DA_LAMBDA_INIT)).astype(o_ref.dtype)


def _diff_attention(z_qkv, cache_k, cache_v, lam, subln_w):
    sw = subln_w.reshape(1, HEAD_W)
    ctx_w = 2 * HEAD_W
    ctx_blk = lambda part: pl.BlockSpec(
        (SEQ, ctx_w), lambda b, h, lam: (b, part * (DA_WIDTH // ctx_w) + h))
    o_ctx = pl.pallas_call(
        functools.partial(_attn_kernel, has_ctx=False),
        grid_spec=pltpu.PrefetchScalarGridSpec(
            num_scalar_prefetch=1, grid=(BATCH, DA_WIDTH // ctx_w),
            in_specs=[ctx_blk(0), ctx_blk(1), ctx_blk(2),
                      pl.BlockSpec((1, HEAD_W), lambda b, h, lam: (0, 0))],
            out_specs=ctx_blk(0)),
        out_shape=jax.ShapeDtypeStruct((T_CTX, DA_WIDTH), jnp.bfloat16),
        compiler_params=_cparams(2), name="attn_ctx",
    )(lam, z_qkv, z_qkv, z_qkv, sw)

    tabs = _rope_tables()
    nq = DEC_SEQ // ATT_TQ
    q_blk0 = T_CTX // ATT_TQ
    kv_blk0 = T_CTX // DEC_SEQ
    kc = cache_k.reshape(DEC_BATCH * PAST_LEN, DA_WIDTH)
    vc = cache_v.reshape(DEC_BATCH * PAST_LEN, DA_WIDTH)
    q_row = lambda b, h, i, lam: (q_blk0 + b * nq + i, h)
    tab_q = pl.BlockSpec((ATT_TQ, HEAD_W), lambda b, h, i, lam: (i, 0))
    tab_k = pl.BlockSpec((DEC_SEQ, HEAD_W), lambda b, h, i, lam: (0, 0))
    o_dec = pl.pallas_call(
        functools.partial(_attn_kernel, has_ctx=True),
        grid_spec=pltpu.PrefetchScalarGridSpec(
            num_scalar_prefetch=1, grid=(DEC_BATCH, DA_HEADS, nq),
            in_specs=[pl.BlockSpec((ATT_TQ, HEAD_W), q_row),
                      pl.BlockSpec((DEC_SEQ, HEAD_W), lambda b, h, i, lam: (kv_blk0 + b, DA_HEADS + h)),
                      pl.BlockSpec((DEC_SEQ, HEAD_W), lambda b, h, i, lam: (kv_blk0 + b, 2 * DA_HEADS + h)),
                      pl.BlockSpec((PAST_LEN, HEAD_W), lambda b, h, i, lam: (b, h)),
                      pl.BlockSpec((PAST_LEN, HEAD_W), lambda b, h, i, lam: (b, h)),
                      tab_q, tab_q, tab_q, tab_k, tab_k, tab_k,
                      pl.BlockSpec((1, HEAD_W), lambda b, h, i, lam: (0, 0))],
            out_specs=pl.BlockSpec((ATT_TQ, HEAD_W), lambda b, h, i, lam: (b * nq + i, h))),
        out_shape=jax.ShapeDtypeStruct((T_DEC, DA_WIDTH), jnp.bfloat16),
        compiler_params=_cparams(3), name="attn_dec",
    )(lam, z_qkv, z_qkv, z_qkv, kc, vc, tabs[0], tabs[1], tabs[2], tabs[0], tabs[1], tabs[2], sw)
    return jnp.concatenate([o_ctx, o_dec], axis=0)


def _dft_cos_sin(L, s_offset):
    k0n = 64
    sp = jnp.arange(L, dtype=jnp.int32) + s_offset
    phase = lambda mult: ((mult[:, None] * sp[None, :]) % (4 * L)).astype(jnp.float32) * (
        math.pi / (2 * L))
    x = phase(2 * k0n * jnp.arange(L // k0n, dtype=jnp.int32))[:, None, :]
    y = phase(2 * jnp.arange(k0n, dtype=jnp.int32) + 1)[None, :, :]
    cx, sx, cy, sy = jnp.cos(x), jnp.sin(x), jnp.cos(y), jnp.sin(y)
    return (cx * cy - sx * sy).reshape(L, L), (sx * cy + cx * sy).reshape(L, L)


def _hyena_tables(L):
    bf = jnp.bfloat16
    c, s = _dft_cos_sin(L, 0)
    fwd = jnp.concatenate([c, -s], axis=0).astype(bf)
    inv = (jnp.concatenate([c.T, -s.T], axis=1) * (1.0 / L)).astype(bf)
    cf, sf = _dft_cos_sin(L, -(L // 2))
    filt_fwd = jnp.concatenate([cf, -sf], axis=0).astype(bf)
    return fwd, inv, filt_fwd


def _hyena_filter_kernel(band_ref, w1t_ref, w1c_ref, w1s_ref, b1_ref, w2_ref, b2_ref, w3_ref,
                         b3_ref, fr_ref, w4_ref, dl_ref, o_ref, h_scr, *, L):
    t = lax.broadcasted_iota(jnp.int32, (L, 1), 0).astype(jnp.float32)

    @pl.when(pl.program_id(0) == 0)
    def _():
        ang = ((2.0 * math.pi / L) * t) * band_ref[...]
        fr = fr_ref[...]
        z = ((t / L) * w1t_ref[...] + _dot3(jnp.cos(ang), w1c_ref[...])
             + _dot3(jnp.sin(ang), w1s_ref[...]))
        h = jnp.sin(fr * (z + b1_ref[...]))
        h = jnp.sin(fr * (_dot3(h, w2_ref[...]) + b2_ref[...]))
        h_scr[...] = jnp.sin(fr * (_dot3(h, w3_ref[...]) + b3_ref[...]))

    dist = jnp.abs(t - (L // 2)) / L
    o_ref[...] = _dot3(h_scr[...], w4_ref[...]) * jnp.exp(-dist * dl_ref[...])


def _hyena_filter(L, w1, b1, w2, b2, w3, b3, w4, freq):
    hid = w2.shape[0]
    tn = 512
    bands = jnp.linspace(1e-4, HY_BANDS - 1, HY_BANDS, dtype=jnp.float32)
    lane_pad = V7X_LANES - HY_BANDS
    band_row = jnp.pad(bands, (0, lane_pad)).reshape(1, V7X_LANES)
    pad_rows = lambda w: jnp.pad(w, ((0, lane_pad), (0, 0)))
    deltas = jnp.abs(jnp.linspace(math.log(HY_TARGET) / HY_SLOW_DECAY,
                                  math.log(HY_TARGET) / HY_FAST_DECAY, HY_WIDTH,
                                  dtype=jnp.float32)).reshape(1, HY_WIDTH)
    full = lambda shape: pl.BlockSpec(shape, lambda j: (0,) * len(shape))
    return pl.pallas_call(
        functools.partial(_hyena_filter_kernel, L=L),
        grid=(HY_WIDTH // tn,),
        in_specs=[full((1, V7X_LANES)), full((1, hid)), full((V7X_LANES, hid)),
                  full((V7X_LANES, hid)), full((1, hid)), full((hid, hid)), full((1, hid)),
                  full((hid, hid)), full((1, hid)), full((1, hid)),
                  pl.BlockSpec((hid, tn), lambda j: (0, j)),
                  pl.BlockSpec((1, tn), lambda j: (0, j))],
        out_specs=pl.BlockSpec((L, tn), lambda j: (0, j)),
        out_shape=jax.ShapeDtypeStruct((L, HY_WIDTH), jnp.float32),
        scratch_shapes=[pltpu.VMEM((L, hid), jnp.float32)],
        compiler_params=_cparams(1), name=f"hyena_filter_{L}",
    )(band_row, w1[0:1], pad_rows(w1[1:1 + HY_BANDS]), pad_rows(w1[1 + HY_BANDS:]),
      b1.reshape(1, hid), w2, b2.reshape(1, hid), w3, b3.reshape(1, hid), freq.reshape(1, hid),
      w4, deltas)


def _hyena_conv_kernel(*refs):
    (x0_ref, x1_ref, v_ref, fr_ref, fi_ref, hr_ref, hi_ref, ir_ref, ii_ref, fb_ref) = refs[:10]
    o_ref, vbf, acc = refs[-3:]
    f32, bf = jnp.float32, jnp.bfloat16
    kb = pl.program_id(2)

    @pl.when(kb == 0)
    def _():
        vbf[...] = (v_ref[...] * x1_ref[...]).astype(bf)
        acc[...] = jnp.zeros_like(acc)

    vr = jnp.dot(fr_ref[...], vbf[...], preferred_element_type=f32)
    vi = jnp.dot(fi_ref[...], vbf[...], preferred_element_type=f32)
    hr, hi = hr_ref[...], hi_ref[...]
    yr = (vr * hr - vi * hi).astype(bf)
    yi = (vr * hi + vi * hr).astype(bf)
    acc[...] += (jnp.dot(ir_ref[...], yr, preferred_element_type=f32)
                 + jnp.dot(ii_ref[...], yi, preferred_element_type=f32))

    @pl.when(kb == pl.num_programs(2) - 1)
    def _():
        vv = v_ref[...] * x1_ref[...]
        o_ref[...] = ((acc[...] + vv * fb_ref[...]) * x0_ref[...]).astype(o_ref.dtype)


def _hyena_conv(u, filt_spec, tables, f_bias, *, L, row_blk0, n_seq, tn, kblk):
    fwd, inv, _ = tables
    nj, nk = HY_WIDTH // tn, L // kblk
    u_spec = lambda part: pl.BlockSpec((L, tn), lambda b, j, k: (row_blk0 + b, part * nj + j))
    in_specs = [u_spec(0), u_spec(1), u_spec(2),
                pl.BlockSpec((kblk, L), lambda b, j, k: (k, 0)),
                pl.BlockSpec((kblk, L), lambda b, j, k: (nk + k, 0)),
                pl.BlockSpec((kblk, tn), lambda b, j, k: (k, j)),
                pl.BlockSpec((kblk, tn), lambda b, j, k: (nk + k, j)),
                pl.BlockSpec((L, kblk), lambda b, j, k: (0, k)),
                pl.BlockSpec((L, kblk), lambda b, j, k: (0, nk + k)),
                pl.BlockSpec((1, tn), lambda b, j, k: (0, j))]
    args = [u, u, u, fwd, fwd, filt_spec, filt_spec, inv, inv, f_bias.reshape(1, HY_WIDTH)]
    return pl.pallas_call(
        _hyena_conv_kernel,
        grid=(n_seq, nj, nk),
        in_specs=in_specs,
        out_specs=pl.BlockSpec((L, tn), lambda b, j, k: (b, j)),
        out_shape=jax.ShapeDtypeStruct((n_seq * L, HY_WIDTH), jnp.bfloat16),
        scratch_shapes=[pltpu.VMEM((L, tn), jnp.bfloat16), pltpu.VMEM((L, tn), jnp.float32)],
        compiler_params=_cparams(3), name=f"hyena_conv_{L}",
    )(*args)


def _hyena_mix(u, filt_p, f_bias):
    outs = []
    for L, row_blk0, n_seq, tn, kblk in ((SEQ, 0, BATCH, HY_WIDTH, SEQ),
                                        (DEC_SEQ, T_CTX // DEC_SEQ, DEC_BATCH, 256, 512)):
        tables = _hyena_tables(L)
        filt = _hyena_filter(L, *filt_p)
        spec = _matmul(tables[2], filt, HY_WIDTH, tm=min(1024, 2 * L), name=f"hyena_spec_{L}")
        outs.append(_hyena_conv(u, spec, tables, f_bias, L=L, row_blk0=row_blk0, n_seq=n_seq,
                                tn=tn, kblk=kblk))
    return jnp.concatenate(outs, axis=0)


def kernel(x_prompt, x_sample, c, state_l0_ssm_fwd, state_l0_ssm_bwd, cache_l0_k, cache_l0_v, c_ctx, l0_norm_mix, l0_norm_ffn, l0_ada_w, l0_ada_b, l0_w_in, l0_conv_w, l0_conv_b, l0_a_log_fwd, l0_a_log_bwd, l0_dt_bias_fwd, l0_dt_bias_bwd, l0_d_skip, l0_gnorm_w, l0_lambda_q1, l0_lambda_k1, l0_lambda_q2, l0_lambda_k2, l0_subln_w, l0_w_out, l0_router_w, l0_router_b, l0_w_gate_up, l0_b_gate_up, l0_w_down, l0_b_down, l1_norm_mix, l1_norm_ffn, l1_ada_w, l1_ada_b, l1_w_in, l1_b_in, l1_short_w, l1_short_b, l1_filt_w1, l1_filt_b1, l1_filt_w2, l1_filt_b2, l1_filt_w3, l1_filt_b3, l1_filt_w4, l1_filt_freq, l1_filt_bias, l1_w_out, l1_b_out, l1_router_w, l1_router_b, l1_w_gate_up, l1_b_gate_up, l1_w_down, l1_b_down, final_norm):
    h = jnp.concatenate([x_prompt.reshape(T_CTX, D_MODEL), x_sample.reshape(T_DEC, D_MODEL)], axis=0)
    cond8 = jnp.concatenate([c_ctx[None, :], c,
                             jnp.zeros((MOD_ROWS - N_COND, D_MODEL), jnp.float32)], axis=0)

    def router_args(rw, rb):
        pad = V7X_LANES - N_EXPERTS
        return (jnp.pad(rw, ((0, 0), (0, pad))),
                jnp.pad(rb, (0, pad), constant_values=-1e30).reshape(1, V7X_LANES))

    def ffn(h, mod, norm_g, rw, rb, w_gu, b_gu, w_dn, b_dn):
        hn, logits = _normmod(h, norm_g, mod, 3, router=router_args(rw, rb))
        return _moe(h, hn, logits, w_gu, b_gu, w_dn, b_dn, mod)

    mod0 = _adaln(cond8, l0_ada_w, l0_ada_b)
    hn = _normmod(h, l0_norm_mix, mod0, 0)
    c_dt = sum(L0_SPLITS[:4])
    c_q = c_dt + 2 * SSD_HEADS
    z_ssd = _matmul(hn, l0_w_in, c_dt, tn=512, name="l0_in_ssd")
    lane_pad = jnp.zeros((D_MODEL, V7X_LANES - SSD_HEADS), jnp.float32)
    w_dt = jnp.concatenate([l0_w_in[:, c_dt:c_dt + SSD_HEADS], lane_pad,
                            l0_w_in[:, c_dt + SSD_HEADS:c_q], lane_pad], axis=1)
    z_dt = _matmul(hn, w_dt, 2 * V7X_LANES, tn=2 * V7X_LANES, name="l0_in_dt")
    z_qkv = _matmul(hn, l0_w_in[:, c_q:], 3 * DA_WIDTH, name="l0_in_qkv")
    xbc = _dwconv(z_ssd, l0_conv_w, l0_conv_b, in_col_blocks=(0, 1, 4), tc=512, silu=True,
                  name="ssd_conv")
    h0_shape = (DEC_BATCH, SSD_WIDTH, SSD_STATE)
    y_fwd, st_fwd = _ssd(xbc, z_dt, l0_dt_bias_fwd, l0_a_log_fwd,
                         state_l0_ssm_fwd.reshape(h0_shape), reverse=False)
    y_ssd, st_bwd = _ssd(xbc, z_dt, l0_dt_bias_bwd, l0_a_log_bwd,
                         state_l0_ssm_bwd.reshape(h0_shape), reverse=True,
                         finish=(y_fwd, z_ssd, jnp.repeat(l0_d_skip, SSD_HEAD_DIM).reshape(1, -1),
                                 l0_gnorm_w.reshape(1, -1)))
    lam = (jnp.exp(jnp.sum(l0_lambda_q1 * l0_lambda_k1))
           - jnp.exp(jnp.sum(l0_lambda_q2 * l0_lambda_k2)) + DA_LAMBDA_INIT).reshape(1)
    o_att = _diff_attention(z_qkv, cache_l0_k, cache_l0_v, lam, l0_subln_w)
    h = _matmul(y_ssd, l0_w_out, D_MODEL, a2=o_att, gated=(h, mod0, 2), name="l0_out")
    h = ffn(h, mod0, l0_norm_ffn, l0_router_w, l0_router_b, l0_w_gate_up, l0_b_gate_up,
            l0_w_down, l0_b_down)
    st_shape = (BATCH, SSD_HEADS, SSD_HEAD_DIM, SSD_STATE)
    kv_shape = (BATCH, SEQ, DA_HEADS, HEAD_W)
    produced = (st_fwd[:BATCH].reshape(st_shape), st_bwd[:BATCH].reshape(st_shape),
                z_qkv[:T_CTX, DA_WIDTH:2 * DA_WIDTH].reshape(kv_shape),
                z_qkv[:T_CTX, 2 * DA_WIDTH:].reshape(kv_shape))

    mod1 = _adaln(cond8, l1_ada_w, l1_ada_b)
    hn = _normmod(h, l1_norm_mix, mod1, 0)
    u = _matmul(hn, l1_w_in, 3 * HY_WIDTH, bias=l1_b_in, name="l1_in")
    u = _dwconv(u, l1_short_w, l1_short_b, in_col_blocks=(0, 1, 2), tc=HY_WIDTH, silu=False,
                name="hyena_conv")
    filt_p = (l1_filt_w1, l1_filt_b1, l1_filt_w2, l1_filt_b2, l1_filt_w3, l1_filt_b3, l1_filt_w4,
              l1_filt_freq)
    y = _hyena_mix(u, filt_p, l1_filt_bias)
    h = _matmul(y, l1_w_out, D_MODEL, bias=l1_b_out, gated=(h, mod1, 2), name="l1_out")
    h = ffn(h, mod1, l1_norm_ffn, l1_router_w, l1_router_b, l1_w_gate_up, l1_b_gate_up,
            l1_w_down, l1_b_down)

    y_prompt = _final_norm(h, final_norm, 0, T_CTX).reshape(BATCH, SEQ, D_MODEL)
    y_sample = _final_norm(h, final_norm, T_CTX, T_DEC).reshape(DEC_BATCH, DEC_SEQ, D_MODEL)
    return (y_prompt, y_sample) + produced
```

```python
import functools
import math

import numpy as np
import jax
import jax.numpy as jnp
from jax import lax
from jax.experimental import pallas as pl
from jax.experimental.pallas import tpu as pltpu

D_MODEL = 2048
BATCH = 32
SEQ = 256
DEC_BATCH = 2
DEC_SEQ = 2048
PAST_LEN = 512
GRID_W = 64
NORM_EPS = 1e-6
SSD_WIDTH = D_MODEL // 2
SSD_HEAD_DIM = 64
SSD_HEADS = SSD_WIDTH // SSD_HEAD_DIM
SSD_GROUPS = 2
SSD_STATE = 128
SSD_CHUNK = 128
SSD_CONV_CH = SSD_WIDTH + 2 * SSD_GROUPS * SSD_STATE
DA_WIDTH = D_MODEL // 2
DA_HEAD_DIM = 64
DA_HEADS = DA_WIDTH // (2 * DA_HEAD_DIM)
DA_LAMBDA_INIT = 0.8 - 0.6 * math.exp(-0.3 * 0)
ROPE_BASE = 10000.0
ROPE_FREQS = DA_HEAD_DIM // 4
L0_SPLITS = (SSD_WIDTH, SSD_WIDTH, SSD_GROUPS * SSD_STATE, SSD_GROUPS * SSD_STATE,
             SSD_HEADS, SSD_HEADS, DA_WIDTH, DA_WIDTH, DA_WIDTH)
HY_WIDTH = D_MODEL
HY_BANDS = 16
HY_FAST_DECAY = 0.3
HY_SLOW_DECAY = 1.5
HY_TARGET = 1e-2
N_EXPERTS = 32
TOP_K = 4
SWIGLU_LIMIT = 7.0
SWIGLU_ALPHA = 1.702

T_CTX = BATCH * SEQ
T_DEC = DEC_BATCH * DEC_SEQ
T_ALL = T_CTX + T_DEC
N_COND = 1 + DEC_BATCH
MOD_ROWS = 8

V7X_LANES = 128
V7X_MXU_DIM = 256
V7X_VMEM_BYTES = 64 * 1024 * 1024
VMEM_LIMIT = V7X_VMEM_BYTES - 8 * 1024 * 1024

MOE_TM = 512
MOE_TN = 2048
MOE_ROWS = T_ALL * TOP_K + N_EXPERTS * MOE_TM
MOE_BLOCKS = MOE_ROWS // MOE_TM


def _cparams(n_grid):
    return pltpu.CompilerParams(dimension_semantics=("arbitrary",) * n_grid,
                                vmem_limit_bytes=VMEM_LIMIT)


def _cond_of_row(row):
    return jnp.where(row < T_CTX, 0, (row - T_CTX) // DEC_SEQ + 1)


def _split_bf16(x):
    hi = x.astype(jnp.bfloat16)
    lo = (x - hi.astype(jnp.float32)).astype(jnp.bfloat16)
    return hi, lo


def _dot3(a, w):
    a_hi, a_lo = _split_bf16(a)
    w_hi, w_lo = _split_bf16(w)
    f32 = jnp.float32
    return (jnp.dot(a_hi, w_hi, preferred_element_type=f32)
            + jnp.dot(a_lo, w_hi, preferred_element_type=f32)
            + jnp.dot(a_hi, w_lo, preferred_element_type=f32))


def _adaln_kernel(c_ref, w_ref, b_ref, o_ref):
    c = c_ref[...]
    a = c * jax.nn.sigmoid(c)
    o_ref[...] = _dot3(a, w_ref[...]) + b_ref[...]


def _adaln(cond8, w, b):
    n = w.shape[1]
    tn = 512
    out = pl.pallas_call(
        _adaln_kernel,
        grid=(n // tn,),
        in_specs=[pl.BlockSpec((MOD_ROWS, D_MODEL), lambda j: (0, 0)),
                  pl.BlockSpec((D_MODEL, tn), lambda j: (0, j)),
                  pl.BlockSpec((1, tn), lambda j: (0, j))],
        out_specs=pl.BlockSpec((MOD_ROWS, tn), lambda j: (0, j)),
        out_shape=jax.ShapeDtypeStruct((MOD_ROWS, n), jnp.float32),
        compiler_params=_cparams(1),
        name="adaln",
    )(cond8, w, b.reshape(1, n))
    m = out[:N_COND].reshape(N_COND, 6, D_MODEL)
    return jnp.pad(m, ((0, 0), (0, MOD_ROWS - 6), (0, 0)))


HALF_D = D_MODEL // 2


def _pack_rows(x):
    as_bits = lambda v: lax.bitcast_convert_type(v.astype(jnp.bfloat16).astype(jnp.float32),
                                                 jnp.uint32)
    lo, hi = as_bits(x[:, :HALF_D]), as_bits(x[:, HALF_D:])
    return (lo >> 16) | (hi & jnp.uint32(0xFFFF0000))


def _unpack_rows(p):
    lo = lax.bitcast_convert_type(p << 16, jnp.float32)
    hi = lax.bitcast_convert_type(p & jnp.uint32(0xFFFF0000), jnp.float32)
    return lo, hi


def _norm_rows(x, g):
    var = jnp.mean(x * x, axis=-1, keepdims=True)
    return x * lax.rsqrt(var + NORM_EPS) * g


def _normmod_kernel(h_ref, g_ref, mod_ref, o_ref, *, sh_row):
    y = _norm_rows(h_ref[...], g_ref[...])
    y = y * (1.0 + mod_ref[sh_row + 1:sh_row + 2, :]) + mod_ref[sh_row:sh_row + 1, :]
    o_ref[...] = y.astype(o_ref.dtype)


def _normmod_router_kernel(h_ref, g_ref, mod_ref, rw_ref, rb_ref, o_ref, lg_ref, *, sh_row):
    y = _norm_rows(h_ref[...], g_ref[...])
    y = y * (1.0 + mod_ref[sh_row + 1:sh_row + 2, :]) + mod_ref[sh_row:sh_row + 1, :]
    o_ref[...] = _pack_rows(y)
    lg_ref[...] = _dot3(y, rw_ref[...]) + rb_ref[...]


def _plain_norm_kernel(h_ref, g_ref, o_ref):
    o_ref[...] = _norm_rows(h_ref[...], g_ref[...])


def _normmod(h, g, mod, sh_row, router=None):
    tm = 512
    grid = (T_ALL // tm,)
    h_spec = pl.BlockSpec((tm, D_MODEL), lambda i: (i, 0))
    g_spec = pl.BlockSpec((1, D_MODEL), lambda i: (0, 0))
    mod_spec = pl.BlockSpec((None, MOD_ROWS, D_MODEL), lambda i: (_cond_of_row(i * tm), 0, 0))
    o_spec = pl.BlockSpec((tm, D_MODEL), lambda i: (i, 0))
    o_shape = jax.ShapeDtypeStruct((T_ALL, D_MODEL), jnp.bfloat16)
    if router is None:
        return pl.pallas_call(
            functools.partial(_normmod_kernel, sh_row=sh_row),
            grid=grid, in_specs=[h_spec, g_spec, mod_spec], out_specs=o_spec,
            out_shape=o_shape, compiler_params=_cparams(1), name="normmod",
        )(h, g.reshape(1, D_MODEL), mod)
    rw, rb = router
    return pl.pallas_call(
        functools.partial(_normmod_router_kernel, sh_row=sh_row),
        grid=grid,
        in_specs=[h_spec, g_spec, mod_spec,
                  pl.BlockSpec((D_MODEL, V7X_LANES), lambda i: (0, 0)),
                  pl.BlockSpec((1, V7X_LANES), lambda i: (0, 0))],
        out_specs=[pl.BlockSpec((tm, HALF_D), lambda i: (i, 0)),
                   pl.BlockSpec((tm, V7X_LANES), lambda i: (i, 0))],
        out_shape=[jax.ShapeDtypeStruct((T_ALL, HALF_D), jnp.uint32),
                   jax.ShapeDtypeStruct((T_ALL, V7X_LANES), jnp.float32)],
        compiler_params=_cparams(1), name="normmod_router",
    )(h, g.reshape(1, D_MODEL), mod, rw, rb)


def _final_norm(h, g, row0, n_rows):
    tm = 512
    blk0 = row0 // tm
    return pl.pallas_call(
        _plain_norm_kernel,
        grid=(n_rows // tm,),
        in_specs=[pl.BlockSpec((tm, D_MODEL), lambda i: (i + blk0, 0)),
                  pl.BlockSpec((1, D_MODEL), lambda i: (0, 0))],
        out_specs=pl.BlockSpec((tm, D_MODEL), lambda i: (i, 0)),
        out_shape=jax.ShapeDtypeStruct((n_rows, D_MODEL), jnp.float32),
        compiler_params=_cparams(1), name="final_norm",
    )(h, g.reshape(1, D_MODEL))


def _mm_kernel(*refs, two_a, has_bias, gated, g_row):
    it = iter(refs)
    a_ref = next(it)
    a2_ref = next(it) if two_a else None
    w_ref = next(it)
    b_ref = next(it) if has_bias else None
    res_ref, mod_ref = (next(it), next(it)) if gated else (None, None)
    o_ref, wbf_ref = next(it), next(it)

    @pl.when(pl.program_id(1) == 0)
    def _():
        wbf_ref[...] = w_ref[...].astype(jnp.bfloat16)

    k1 = a_ref.shape[1]
    acc = jnp.dot(a_ref[...], wbf_ref[:k1, :], preferred_element_type=jnp.float32)
    if two_a:
        acc = acc + jnp.dot(a2_ref[...], wbf_ref[k1:, :], preferred_element_type=jnp.float32)
    if has_bias:
        acc = acc + b_ref[...]
    if gated:
        acc = res_ref[...] + mod_ref[g_row:g_row + 1, :] * acc
    o_ref[...] = acc.astype(o_ref.dtype)


def _matmul(a, w, n_cols, *, a2=None, col_blk0=0, bias=None, gated=None,
            out_dtype=jnp.float32, tm=1024, tn=1024, name="matmul"):
    m, k1 = a.shape
    k = w.shape[0]
    assert m % tm == 0 and n_cols % tn == 0
    in_specs = [pl.BlockSpec((tm, k1), lambda j, i: (i, 0))]
    args = [a]
    if a2 is not None:
        assert k1 + a2.shape[1] == k
        in_specs.append(pl.BlockSpec((tm, k - k1), lambda j, i: (i, 0)))
        args.append(a2)
    in_specs.append(pl.BlockSpec((k, tn), lambda j, i: (0, j + col_blk0)))
    args.append(w)
    if bias is not None:
        in_specs.append(pl.BlockSpec((1, tn), lambda j, i: (0, j + col_blk0)))
        args.append(bias.reshape(1, -1))
    g_row = 0
    if gated is not None:
        res, mod, g_row = gated
        in_specs.append(pl.BlockSpec((tm, tn), lambda j, i: (i, j)))
        in_specs.append(pl.BlockSpec((None, MOD_ROWS, tn),
                                     lambda j, i: (_cond_of_row(i * tm), 0, j)))
        args += [res, mod]
    return pl.pallas_call(
        functools.partial(_mm_kernel, two_a=a2 is not None, has_bias=bias is not None,
                          gated=gated is not None, g_row=g_row),
        grid=(n_cols // tn, m // tm),
        in_specs=in_specs,
        out_specs=pl.BlockSpec((tm, tn), lambda j, i: (i, j)),
        out_shape=jax.ShapeDtypeStruct((m, n_cols), out_dtype),
        scratch_shapes=[pltpu.VMEM((k, tn), jnp.bfloat16)],
        compiler_params=_cparams(2), name=name,
    )(*args)


def _gmm_kernel(be_ref, half_ref, nv_ref, x_ref, w_ref, b_ref, *rest, swiglu):
    if swiglu:
        sel_ref, o_ref, wbf_ref = rest
    else:
        o_ref, wbf_ref = rest
    i = pl.program_id(1)
    prev = be_ref[jnp.maximum(i - 1, 0)]

    @pl.when((i == 0) | (be_ref[i] != prev))
    def _():
        wbf_ref[...] = w_ref[...].astype(jnp.bfloat16)

    def compute(n_rows):
        f32, bf = jnp.float32, jnp.bfloat16
        if n_rows < MOE_TM:
            o_ref[n_rows:, :] = jnp.zeros((MOE_TM - n_rows, o_ref.shape[1]), o_ref.dtype)
        if not swiglu:
            acc = jnp.dot(x_ref[0:n_rows, :], wbf_ref[...], preferred_element_type=f32) + b_ref[...]
            o_ref[0:n_rows, :] = acc.astype(o_ref.dtype)
            return
        x_lo, x_hi = _unpack_rows(x_ref[0:n_rows, :])
        acc = (jnp.dot(x_lo.astype(bf), wbf_ref[:HALF_D, :], preferred_element_type=f32)
               + jnp.dot(x_hi.astype(bf), wbf_ref[HALF_D:, :], preferred_element_type=f32)
               + b_ref[...])
        tn = acc.shape[1]
        wide, half = V7X_MXU_DIM, V7X_MXU_DIM // 2
        gate = jnp.minimum(acc, SWIGLU_LIMIT)
        up = pltpu.roll(jnp.clip(acc, -SWIGLU_LIMIT, SWIGLU_LIMIT), tn - 1, 1)
        act = (up + 1.0) * (gate * (0.5 + 0.5 * jnp.tanh((0.5 * SWIGLU_ALPHA) * gate)))
        lane = lax.broadcasted_iota(jnp.int32, act.shape, 1)
        act = jnp.where(lane % 2 == 0, act, 0.0).astype(jnp.bfloat16)
        for g in range(tn // wide):
            o_ref[0:n_rows, g * half:(g + 1) * half] = jnp.dot(
                act[:, g * wide:(g + 1) * wide], sel_ref[...],
                preferred_element_type=f32).astype(o_ref.dtype)

    valid = i < nv_ref[0]

    @pl.when(valid & (half_ref[i] == 0))
    def _():
        compute(MOE_TM)

    @pl.when(valid & (half_ref[i] != 0))
    def _():
        compute(MOE_TM // 2)

    @pl.when(i >= nv_ref[0])
    def _():
        o_ref[...] = jnp.zeros_like(o_ref)


def _grouped_matmul(x, w, b, block_e, block_half, n_valid, *, swiglu, tn, out_dtype, name):
    rows = x.shape[0]
    k, n = w.shape[1:]
    if swiglu:
        tn_out, n_out = tn // 2, n // 2
    else:
        tn_out, n_out = tn, n
    last_used = lambda nv: jnp.maximum(nv[0] - 1, 0)
    in_specs = [pl.BlockSpec((MOE_TM, x.shape[1]),
                             lambda j, i, be, hf, nv: (jnp.minimum(i, last_used(nv)), 0)),
                pl.BlockSpec((None, k, tn), lambda j, i, be, hf, nv: (be[i], 0, j)),
                pl.BlockSpec((None, 1, tn), lambda j, i, be, hf, nv: (be[i], 0, j))]
    args = [x, w, b.reshape(N_EXPERTS, 1, n)]
    if swiglu:
        sel = np.zeros((V7X_MXU_DIM, V7X_MXU_DIM // 2), np.float32)
        sel[2 * np.arange(V7X_MXU_DIM // 2), np.arange(V7X_MXU_DIM // 2)] = 1.0
        in_specs.append(pl.BlockSpec(sel.shape, lambda j, i, be, hf, nv: (0, 0)))
        args.append(jnp.asarray(sel, jnp.bfloat16))
    return pl.pallas_call(
        functools.partial(_gmm_kernel, swiglu=swiglu),
        grid_spec=pltpu.PrefetchScalarGridSpec(
            num_scalar_prefetch=3,
            grid=(n // tn, rows // MOE_TM),
            in_specs=in_specs,
            out_specs=pl.BlockSpec((MOE_TM, tn_out), lambda j, i, be, hf, nv: (i, j)),
            scratch_shapes=[pltpu.VMEM((k, tn), jnp.bfloat16)]),
        out_shape=jax.ShapeDtypeStruct((rows, n_out), out_dtype),
        compiler_params=_cparams(2), name=name,
    )(block_e, block_half, n_valid, *args)


DISPATCH_TM = 512


def _dispatch_kernel(pos_ref, hp_ref, x_init, x_hbm, sem):
    del x_init

    def body(r, carry):
        for k in range(TOP_K):
            pltpu.make_async_copy(hp_ref.at[pl.ds(r, 1)],
                                  x_hbm.at[pl.ds(pos_ref[0, r * TOP_K + k], 1)], sem).start()
        return carry
    lax.fori_loop(0, DISPATCH_TM, body, 0, unroll=4)
    for _ in range(TOP_K):
        pltpu.make_async_copy(hp_ref, x_hbm.at[pl.ds(0, DISPATCH_TM)], sem).wait()


def _moe_dispatch(hp, pos):
    t = hp.shape[0]
    nblk = t // DISPATCH_TM
    pos3 = pos.reshape(nblk, 1, DISPATCH_TM * TOP_K)
    return pl.pallas_call(
        _dispatch_kernel,
        grid=(nblk,),
        in_specs=[pl.BlockSpec((None, 1, DISPATCH_TM * TOP_K), lambda i: (i, 0, 0),
                               memory_space=pltpu.SMEM),
                  pl.BlockSpec((DISPATCH_TM, HALF_D), lambda i: (i, 0)),
                  pl.BlockSpec(memory_space=pl.ANY)],
        out_specs=pl.BlockSpec(memory_space=pl.ANY),
        out_shape=jax.ShapeDtypeStruct((MOE_ROWS, HALF_D), jnp.uint32),
        scratch_shapes=[pltpu.SemaphoreType.DMA(())],
        input_output_aliases={2: 0},
        compiler_params=_cparams(1), name="moe_dispatch",
    )(pos3, hp, jnp.zeros((MOE_ROWS, HALF_D), jnp.uint32))


COMBINE_TM = 256


def _combine_kernel(pos_ref, pos_next_ref, h_ref, gates_ref, mod_ref, y_hbm, o_ref, buf, sem,
                    *, g_row):
    i = pl.program_id(0)
    slot = i % 2
    n_rows = TOP_K * COMBINE_TM

    def issue(idx_ref, s):
        def body(r, carry):
            pltpu.make_async_copy(y_hbm.at[pl.ds(idx_ref[0, r], 1)],
                                  buf.at[s, pl.ds(r, 1)], sem.at[s]).start()
            return carry
        lax.fori_loop(0, n_rows, body, 0, unroll=8)

    @pl.when(i == 0)
    def _():
        issue(pos_ref, 0)

    @pl.when(i + 1 < pl.num_programs(0))
    def _():
        issue(pos_next_ref, 1 - slot)

    pltpu.make_async_copy(y_hbm.at[pl.ds(0, n_rows)], buf.at[slot], sem.at[slot]).wait()
    acc = gates_ref[:, 0:1] * buf[slot, 0:COMBINE_TM, :]
    for k in range(1, TOP_K):
        acc = acc + gates_ref[:, k:k + 1] * buf[slot, k * COMBINE_TM:(k + 1) * COMBINE_TM, :]
    o_ref[...] = h_ref[...] + mod_ref[g_row:g_row + 1, :] * acc


def _moe_combine(h, y_rows, pos, gates_pad, mod, g_row):
    t = h.shape[0]
    nblk = t // COMBINE_TM
    pos3 = pos.reshape(nblk, COMBINE_TM, TOP_K).transpose(0, 2, 1).reshape(nblk, 1, TOP_K * COMBINE_TM)
    smem_blk = lambda f: pl.BlockSpec((None, 1, TOP_K * COMBINE_TM), f, memory_space=pltpu.SMEM)
    return pl.pallas_call(
        functools.partial(_combine_kernel, g_row=g_row),
        grid=(nblk,),
        in_specs=[smem_blk(lambda i: (i, 0, 0)),
                  smem_blk(lambda i: (jnp.minimum(i + 1, nblk - 1), 0, 0)),
                  pl.BlockSpec((COMBINE_TM, D_MODEL), lambda i: (i, 0)),
                  pl.BlockSpec((COMBINE_TM, V7X_LANES), lambda i: (i, 0)),
                  pl.BlockSpec((None, MOD_ROWS, D_MODEL),
                               lambda i: (_cond_of_row(i * COMBINE_TM), 0, 0)),
                  pl.BlockSpec(memory_space=pl.ANY)],
        out_specs=pl.BlockSpec((COMBINE_TM, D_MODEL), lambda i: (i, 0)),
        out_shape=jax.ShapeDtypeStruct((t, D_MODEL), jnp.float32),
        scratch_shapes=[pltpu.VMEM((2, TOP_K * COMBINE_TM, D_MODEL), jnp.float32),
                        pltpu.SemaphoreType.DMA((2,))],
        compiler_params=_cparams(1), name="moe_combine",
    )(pos3, pos3, h, gates_pad, mod, y_rows)


ROUTE_TM = 512


def _route_kernel(lg_ref, idx_ref, gate_ref, rank_ref, cnt_ref, carry):
    f32 = jnp.float32
    i = pl.program_id(0)

    @pl.when(i == 0)
    def _():
        carry[...] = jnp.zeros_like(carry)

    lg = lg_ref[...]
    lane = lax.broadcasted_iota(jnp.int32, lg.shape, 1)
    lane_f = lane.astype(f32)
    vals, ids, hots = [], [], []
    for _ in range(TOP_K):
        m = jnp.max(lg, axis=1, keepdims=True)
        first = jnp.min(jnp.where(lg == m, lane_f, float(V7X_LANES)), axis=1, keepdims=True)
        hot = lane_f == first
        vals.append(m)
        ids.append(first)
        hots.append(hot)
        lg = jnp.where(hot, -jnp.inf, lg)
    exps = [jnp.exp(v - vals[0]) for v in vals]
    denom = exps[0]
    for e in exps[1:]:
        denom = denom + e
    chosen = jnp.zeros(lg.shape, f32)
    for hot in hots:
        chosen = chosen + jnp.where(hot, 1.0, 0.0)
    ri = lax.broadcasted_iota(jnp.int32, (ROUTE_TM, ROUTE_TM), 0)
    ci = lax.broadcasted_iota(jnp.int32, (ROUTE_TM, ROUTE_TM), 1)
    tri = jnp.where(ci <= ri, 1.0, 0.0).astype(jnp.bfloat16)
    upto = jnp.dot(tri, chosen.astype(jnp.bfloat16), preferred_element_type=f32)
    before = carry[...] + upto - chosen
    idx_out = jnp.zeros(lg.shape, f32)
    gate_out = jnp.zeros(lg.shape, f32)
    rank_out = jnp.zeros(lg.shape, f32)
    for k in range(TOP_K):
        rank_k = jnp.sum(jnp.where(hots[k], before, 0.0), axis=1, keepdims=True)
        idx_out = jnp.where(lane == k, ids[k], idx_out)
        gate_out = jnp.where(lane == k, exps[k] / denom, gate_out)
        rank_out = jnp.where(lane == k, rank_k, rank_out)
    idx_ref[...] = idx_out.astype(jnp.int32)
    gate_ref[...] = gate_out
    rank_ref[...] = rank_out.astype(jnp.int32)
    carry[...] = carry[...] + upto[ROUTE_TM - 1:ROUTE_TM, :]
    cnt_ref[...] = carry[...]


def _route(logits):
    t = logits.shape[0]
    blk = pl.BlockSpec((ROUTE_TM, V7X_LANES), lambda i: (i, 0))
    row = pl.BlockSpec((1, V7X_LANES), lambda i: (0, 0))
    sds = lambda dt: jax.ShapeDtypeStruct((t, V7X_LANES), dt)
    return pl.pallas_call(
        _route_kernel,
        grid=(t // ROUTE_TM,),
        in_specs=[blk],
        out_specs=[blk, blk, blk, row],
        out_shape=[sds(jnp.int32), sds(jnp.float32), sds(jnp.int32),
                   jax.ShapeDtypeStruct((1, V7X_LANES), jnp.float32)],
        scratch_shapes=[pltpu.VMEM((1, V7X_LANES), jnp.float32)],
        compiler_params=_cparams(1), name="moe_route",
    )(logits)


def _moe(h, hn, logits, w_gu, b_gu, w_dn, b_dn, mod):
    t = hn.shape[0]
    idx, gates_pad, rank, counts = _route(logits)
    top_idx = idx[:, :TOP_K]
    counts = counts[0, :N_EXPERTS].astype(jnp.int32)
    padded = (counts + MOE_TM - 1) // MOE_TM * MOE_TM
    padded_end = jnp.cumsum(padded)
    padded_start = padded_end - padded
    pos = padded_start[top_idx] + rank[:, :TOP_K]
    block_e = jnp.minimum(
        jnp.searchsorted(padded_end, jnp.arange(MOE_BLOCKS) * MOE_TM, side='right'),
        N_EXPERTS - 1).astype(jnp.int32)
    n_valid = (padded_end[-1] // MOE_TM).astype(jnp.int32).reshape(1)
    rows_used = (padded_start + counts)[block_e] - jnp.arange(MOE_BLOCKS) * MOE_TM
    block_half = (rows_used <= MOE_TM // 2).astype(jnp.int32)

    x_rows = _moe_dispatch(hn, pos)
    act = _grouped_matmul(x_rows, w_gu, b_gu, block_e, block_half, n_valid, swiglu=True, tn=MOE_TN,
                          out_dtype=jnp.bfloat16, name="moe_gate_up")
    y_rows = _grouped_matmul(act, w_dn, b_dn, block_e, block_half, n_valid, swiglu=False, tn=MOE_TN,
                             out_dtype=jnp.float32, name="moe_down")
    return _moe_combine(h, y_rows, pos, gates_pad, mod, 5)


CONV_TM = 256
N_CHUNKS = T_ALL // SSD_CHUNK
CTX_CHUNKS = T_CTX // SSD_CHUNK
N_SEQS = BATCH + DEC_BATCH


def _seq_len_of_row(row):
    return jnp.where(row < T_CTX, SEQ, DEC_SEQ)


def _seq_pos_of_row(row):
    return jnp.where(row < T_CTX, row % SEQ, (row - T_CTX) % DEC_SEQ)


def _seq_of_chunk(n):
    return jnp.where(n < CTX_CHUNKS, n // (SEQ // SSD_CHUNK),
                     BATCH + (n - CTX_CHUNKS) // (DEC_SEQ // SSD_CHUNK))


def _dwconv_kernel(x_ref, prev_ref, next_ref, w_ref, b_ref, o_ref, *, width, silu):
    row0 = pl.program_id(1) * CONV_TM
    pos0 = _seq_pos_of_row(row0)
    has_prev = pos0 != 0
    has_next = pos0 + CONV_TM != _seq_len_of_row(row0)
    x = x_ref[...]
    rid = lax.broadcasted_iota(jnp.int32, x.shape, 0)
    prev = jnp.where(has_prev, prev_ref[7:8, :], 0.0)
    nxt0 = jnp.where(has_next, next_ref[0:1, :], 0.0)
    acc = w_ref[1:2, :] * x + b_ref[...]
    xm1 = jnp.where(rid == 0, prev, pltpu.roll(x, 1, 0))
    acc = acc + w_ref[0:1, :] * xm1
    xp1 = jnp.where(rid == CONV_TM - 1, nxt0, pltpu.roll(x, CONV_TM - 1, 0))
    acc = acc + w_ref[2:3, :] * xp1
    if width == 4:
        nxt1 = jnp.where(has_next, next_ref[1:2, :], 0.0)
        xp2 = pltpu.roll(x, CONV_TM - 2, 0)
        xp2 = jnp.where(rid == CONV_TM - 2, nxt0, jnp.where(rid == CONV_TM - 1, nxt1, xp2))
        acc = acc + w_ref[3:4, :] * xp2
    if silu:
        acc = acc * jax.nn.sigmoid(acc)
    o_ref[...] = acc


def _dwconv(x, w, b, *, in_col_blocks, tc, silu, name):
    width = w.shape[0]
    nblk = len(in_col_blocks)
    assert w.shape[1] == nblk * tc and width in (3, 4)
    cols = np.asarray(in_col_blocks, np.int32)
    if np.array_equal(cols, np.arange(nblk)):
        col = lambda j: j
    else:
        assert nblk == 3 and cols[0] == 0 and cols[1] == 1
        col = lambda j: jnp.where(j < 2, j, int(cols[2]))
    w8 = jnp.pad(w, ((0, 8 - width), (0, 0)))
    sub = CONV_TM // 8
    last8 = T_ALL // 8 - 1
    return pl.pallas_call(
        functools.partial(_dwconv_kernel, width=width, silu=silu),
        grid=(nblk, T_ALL // CONV_TM),
        in_specs=[pl.BlockSpec((CONV_TM, tc), lambda j, i: (i, col(j))),
                  pl.BlockSpec((8, tc), lambda j, i: (jnp.maximum(i * sub - 1, 0), col(j))),
                  pl.BlockSpec((8, tc), lambda j, i: (jnp.minimum((i + 1) * sub, last8), col(j))),
                  pl.BlockSpec((8, tc), lambda j, i: (0, j)),
                  pl.BlockSpec((1, tc), lambda j, i: (0, j))],
        out_specs=pl.BlockSpec((CONV_TM, tc), lambda j, i: (i, j)),
        out_shape=jax.ShapeDtypeStruct((T_ALL, nblk * tc), jnp.float32),
        compiler_params=_cparams(2), name=name,
    )(x, x, x, w8, b.reshape(1, -1))


def _softplus(x):
    return jnp.maximum(x, 0.0) + jnp.log(1.0 + jnp.exp(-jnp.abs(x)))


def _ssd_kernel(*refs, reverse):
    if reverse:
        (xbc_ref, zdt_ref, bias_ref, alog_ref, h0_ref, yf_ref, z_ref, dskip_ref, gn_ref,
         y_ref, st_ref, s_scr, y_scr) = refs
    else:
        xbc_ref, zdt_ref, bias_ref, alog_ref, h0_ref, y_ref, st_ref, s_scr = refs
    f32, bf = jnp.float32, jnp.bfloat16
    q = SSD_CHUNK
    step = pl.program_id(0)
    n = (N_CHUNKS - 1 - step) if reverse else step
    row0 = n * q
    pos0 = _seq_pos_of_row(row0)
    seq_len = _seq_len_of_row(row0)
    starts_seq = (pos0 + q == seq_len) if reverse else (pos0 == 0)
    ends_seq = (pos0 == 0) if reverse else (pos0 + q == seq_len)

    @pl.when(starts_seq)
    def _():
        s_scr[...] = jnp.where(row0 < T_CTX, 0.0, h0_ref[...])

    ri = lax.broadcasted_iota(jnp.int32, (q, q), 0)
    ci = lax.broadcasted_iota(jnp.int32, (q, q), 1)
    mask = (ci >= ri) if reverse else (ci <= ri)
    lane_lo = ci < SSD_HEAD_DIM
    row_lo = ri < SSD_HEAD_DIM

    dt = _softplus(zdt_ref[...] + bias_ref[...])
    da = dt * (-jnp.exp(alog_ref[...]))
    ones = jnp.where(mask, 1.0, 0.0).astype(bf)
    d1 = da.astype(bf)
    r1 = da - d1.astype(f32)
    d2 = r1.astype(bf)
    d3 = (r1 - d2.astype(f32)).astype(bf)
    cs = (jnp.dot(ones, d1, preferred_element_type=f32)
          + jnp.dot(ones, d2, preferred_element_type=f32)
          + jnp.dot(ones, d3, preferred_element_type=f32))
    cs_t = cs.T
    tot = cs[0:1, :] if reverse else cs[q - 1:q, :]
    exp_cs = jnp.exp(cs)
    d_end = jnp.exp(tot - cs)
    exp_tot = jnp.exp(tot)

    def pair_cols(v, p):
        return jnp.where(lane_lo, v[:, 2 * p:2 * p + 1], v[:, 2 * p + 1:2 * p + 2])

    dn_t = (((1,), (1,)), ((), ()))
    for g in range(SSD_GROUPS):
        b_g = xbc_ref[:, SSD_WIDTH + g * SSD_STATE:SSD_WIDTH + (g + 1) * SSD_STATE].astype(bf)
        c_off = SSD_WIDTH + SSD_GROUPS * SSD_STATE
        c_g = xbc_ref[:, c_off + g * SSD_STATE:c_off + (g + 1) * SSD_STATE].astype(bf)
        cb = lax.dot_general(c_g, b_g, dn_t, preferred_element_type=f32)
        pairs_per_group = SSD_HEADS // SSD_GROUPS // 2
        for p in range(g * pairs_per_group, (g + 1) * pairs_per_group):
            lo, hi = p * 2 * SSD_HEAD_DIM, (p + 1) * 2 * SSD_HEAD_DIM
            xs = xbc_ref[:, lo:hi]
            xdt = xs * pair_cols(dt, p)
            y = jnp.zeros((q, 2 * SSD_HEAD_DIM), f32)
            for h, keep in ((2 * p, lane_lo), (2 * p + 1, ~lane_lo)):
                seg = cs[:, h:h + 1] - cs_t[h:h + 1, :]
                w = jnp.where(mask, jnp.exp(jnp.minimum(seg, 0.0)), 0.0) * cb
                y = y + jnp.dot(w.astype(bf), jnp.where(keep, xdt, 0.0).astype(bf),
                                preferred_element_type=f32)
            s_pair = s_scr[lo:hi, :]
            y_off = lax.dot_general(c_g, s_pair.astype(bf), dn_t, preferred_element_type=f32)
            y = y + y_off * pair_cols(exp_cs, p)
            st = jnp.dot((xdt * pair_cols(d_end, p)).T.astype(bf), b_g,
                         preferred_element_type=f32)
            keep_s = jnp.where(row_lo, exp_tot[:, 2 * p:2 * p + 1], exp_tot[:, 2 * p + 1:2 * p + 2])
            s_scr[lo:hi, :] = s_pair * keep_s + st
            if reverse:
                y_scr[:, lo:hi] = y + yf_ref[:, lo:hi] + dskip_ref[:, lo:hi] * xs
            else:
                y_ref[:, lo:hi] = y

    if reverse:
        zz = z_ref[...]
        yy = y_scr[...] * (zz * jax.nn.sigmoid(zz))
        gw = SSD_WIDTH // SSD_GROUPS
        for g in range(SSD_GROUPS):
            y_g = yy[:, g * gw:(g + 1) * gw]
            y_g = y_g * lax.rsqrt(jnp.mean(y_g * y_g, axis=-1, keepdims=True) + NORM_EPS)
            y_ref[:, g * gw:(g + 1) * gw] = (y_g * gn_ref[:, g * gw:(g + 1) * gw]).astype(y_ref.dtype)

    @pl.when(ends_seq)
    def _():
        st_ref[...] = s_scr[...]


def _ssd(xbc, z_dt, dt_bias, a_log, h0, *, reverse, finish=None):
    q = SSD_CHUNK
    d = 1 if reverse else 0
    chunk = (lambda s: N_CHUNKS - 1 - s) if reverse else (lambda s: s)
    pad_row = lambda v: jnp.pad(v, (0, V7X_LANES - SSD_HEADS)).reshape(1, V7X_LANES)
    in_specs = [pl.BlockSpec((q, SSD_CONV_CH), lambda s: (chunk(s), 0)),
                pl.BlockSpec((q, V7X_LANES), lambda s: (chunk(s), d)),
                pl.BlockSpec((1, V7X_LANES), lambda s: (0, 0)),
                pl.BlockSpec((1, V7X_LANES), lambda s: (0, 0)),
                pl.BlockSpec((None, SSD_WIDTH, SSD_STATE),
                             lambda s: (jnp.maximum(_seq_of_chunk(chunk(s)) - BATCH, 0), 0, 0))]
    args = [xbc, z_dt, pad_row(dt_bias), pad_row(a_log), h0]
    scratch = [pltpu.VMEM((SSD_WIDTH, SSD_STATE), jnp.float32)]
    y_dtype = jnp.float32
    if reverse:
        y_fwd, z_ssd, dskip_row, gn_row = finish
        in_specs += [pl.BlockSpec((q, SSD_WIDTH), lambda s: (chunk(s), 0)),
                     pl.BlockSpec((q, SSD_WIDTH), lambda s: (chunk(s), 1)),
                     pl.BlockSpec((1, SSD_WIDTH), lambda s: (0, 0)),
                     pl.BlockSpec((1, SSD_WIDTH), lambda s: (0, 0))]
        args += [y_fwd, z_ssd, dskip_row, gn_row]
        scratch.append(pltpu.VMEM((q, SSD_WIDTH), jnp.float32))
        y_dtype = jnp.bfloat16
    return pl.pallas_call(
        functools.partial(_ssd_kernel, reverse=reverse),
        grid=(N_CHUNKS,),
        in_specs=in_specs,
        out_specs=[pl.BlockSpec((q, SSD_WIDTH), lambda s: (chunk(s), 0)),
                   pl.BlockSpec((None, SSD_WIDTH, SSD_STATE),
                                lambda s: (_seq_of_chunk(chunk(s)), 0, 0))],
        out_shape=[jax.ShapeDtypeStruct((T_ALL, SSD_WIDTH), y_dtype),
                   jax.ShapeDtypeStruct((N_SEQS, SSD_WIDTH, SSD_STATE), jnp.float32)],
        scratch_shapes=scratch,
        compiler_params=_cparams(1), name="ssd_bwd" if reverse else "ssd_fwd",
    )(*args)


ATT_TQ = 256
HEAD_W = 2 * DA_HEAD_DIM


def _rope_tables():
    t = jnp.arange(DEC_SEQ)
    pos = jnp.stack([t // GRID_W, t % GRID_W], axis=-1).astype(jnp.float32)
    inv = 1.0 / (ROPE_BASE ** (jnp.arange(ROPE_FREQS, dtype=jnp.float32) / ROPE_FREQS))
    ang = pos[:, :, None] * inv
    cos, sin = jnp.cos(ang), jnp.sin(ang)
    zero = jnp.zeros_like(sin)
    lanes = lambda first, second: jnp.tile(
        jnp.stack([first, second], axis=2).reshape(DEC_SEQ, DA_HEAD_DIM), (1, 2))
    return lanes(cos, cos), lanes(-sin, zero), lanes(zero, sin)


def _attn_kernel(lam_ref, *refs, has_ctx):
    if has_ctx:
        (q_ref, k_ref, v_ref, kc_ref, vc_ref, cq, saq, sbq, ck, sak, sbk, sw_ref, o_ref) = refs
    else:
        q_ref, k_ref, v_ref, sw_ref, o_ref = refs
    f32, bf = jnp.float32, jnp.bfloat16
    half = ROPE_FREQS
    scale = DA_HEAD_DIM ** -0.5

    def rope(x, c, sa, sb):
        return (x * c[...] + pltpu.roll(x, HEAD_W - half, 1) * sa[...]
                + pltpu.roll(x, half, 1) * sb[...])

    for hd in range(q_ref.shape[1] // HEAD_W):
        cols = slice(hd * HEAD_W, (hd + 1) * HEAD_W)
        q, k = q_ref[:, cols], k_ref[:, cols]
        if has_ctx:
            q, k = rope(q, cq, saq, sbq), rope(k, ck, sak, sbk)
        lane = lax.broadcasted_iota(jnp.int32, q.shape, 1)
        kb, vb = k.astype(bf), v_ref[:, cols].astype(bf)
        if has_ctx:
            kb = jnp.concatenate([kc_ref[:, cols].astype(bf), kb], axis=0)
            vb = jnp.concatenate([vc_ref[:, cols].astype(bf), vb], axis=0)

        def exps(qm):
            s = lax.dot_general((qm * scale).astype(bf), kb, (((1,), (1,)), ((), ())),
                                preferred_element_type=f32)
            e = jnp.exp(s - jnp.max(s, axis=-1, keepdims=True))
            return e, 1.0 / jnp.sum(e, axis=-1, keepdims=True)

        e0, r0 = exps(jnp.where(lane < DA_HEAD_DIM, q, 0.0))
        e1, r1 = exps(jnp.where(lane >= DA_HEAD_DIM, q, 0.0))
        w = e0 * r0 - e1 * (lam_ref[0] * r1)
        o = jnp.dot(w.astype(bf), vb, preferred_element_type=f32)
        o = o * lax.rsqrt(jnp.mean(o * o, axis=-1, keepdims=True) + NORM_EPS) * sw_ref[...]
        o_ref[:, cols] = (o * (1.0 - DA_LAMBDA_INIT)).astype(o_ref.dtype)


def _diff_attention(z_qkv, cache_k, cache_v, lam, subln_w):
    sw = subln_w.reshape(1, HEAD_W)
    ctx_w = 2 * HEAD_W
    ctx_blk = lambda part: pl.BlockSpec(
        (SEQ, ctx_w), lambda b, h, lam: (b, part * (DA_WIDTH // ctx_w) + h))
    o_ctx = pl.pallas_call(
        functools.partial(_attn_kernel, has_ctx=False),
        grid_spec=pltpu.PrefetchScalarGridSpec(
            num_scalar_prefetch=1, grid=(BATCH, DA_WIDTH // ctx_w),
            in_specs=[ctx_blk(0), ctx_blk(1), ctx_blk(2),
                      pl.BlockSpec((1, HEAD_W), lambda b, h, lam: (0, 0))],
            out_specs=ctx_blk(0)),
        out_shape=jax.ShapeDtypeStruct((T_CTX, DA_WIDTH), jnp.bfloat16),
        compiler_params=_cparams(2), name="attn_ctx",
    )(lam, z_qkv, z_qkv, z_qkv, sw)

    tabs = _rope_tables()
    nq = DEC_SEQ // ATT_TQ
    q_blk0 = T_CTX // ATT_TQ
    kv_blk0 = T_CTX // DEC_SEQ
    kc = cache_k.reshape(DEC_BATCH * PAST_LEN, DA_WIDTH)
    vc = cache_v.reshape(DEC_BATCH * PAST_LEN, DA_WIDTH)
    q_row = lambda b, h, i, lam: (q_blk0 + b * nq + i, h)
    tab_q = pl.BlockSpec((ATT_TQ, HEAD_W), lambda b, h, i, lam: (i, 0))
    tab_k = pl.BlockSpec((DEC_SEQ, HEAD_W), lambda b, h, i, lam: (0, 0))
    o_dec = pl.pallas_call(
        functools.partial(_attn_kernel, has_ctx=True),
        grid_spec=pltpu.PrefetchScalarGridSpec(
            num_scalar_prefetch=1, grid=(DEC_BATCH, DA_HEADS, nq),
            in_specs=[pl.BlockSpec((ATT_TQ, HEAD_W), q_row),
                      pl.BlockSpec((DEC_SEQ, HEAD_W), lambda b, h, i, lam: (kv_blk0 + b, DA_HEADS + h)),
                      pl.BlockSpec((DEC_SEQ, HEAD_W), lambda b, h, i, lam: (kv_blk0 + b, 2 * DA_HEADS + h)),
                      pl.BlockSpec((PAST_LEN, HEAD_W), lambda b, h, i, lam: (b, h)),
                      pl.BlockSpec((PAST_LEN, HEAD_W), lambda b, h, i, lam: (b, h)),
                      tab_q, tab_q, tab_q, tab_k, tab_k, tab_k,
                      pl.BlockSpec((1, HEAD_W), lambda b, h, i, lam: (0, 0))],
            out_specs=pl.BlockSpec((ATT_TQ, HEAD_W), lambda b, h, i, lam: (b * nq + i, h))),
        out_shape=jax.ShapeDtypeStruct((T_DEC, DA_WIDTH), jnp.bfloat16),
        compiler_params=_cparams(3), name="attn_dec",
    )(lam, z_qkv, z_qkv, z_qkv, kc, vc, tabs[0], tabs[1], tabs[2], tabs[0], tabs[1], tabs[2], sw)
    return jnp.concatenate([o_ctx, o_dec], axis=0)


def _dft_cos_sin(L, s_offset):
    k0n = 64
    sp = jnp.arange(L, dtype=jnp.int32) + s_offset
    phase = lambda mult: ((mult[:, None] * sp[None, :]) % (4 * L)).astype(jnp.float32) * (
        math.pi / (2 * L))
    x = phase(2 * k0n * jnp.arange(L // k0n, dtype=jnp.int32))[:, None, :]
    y = phase(2 * jnp.arange(k0n, dtype=jnp.int32) + 1)[None, :, :]
    cx, sx, cy, sy = jnp.cos(x), jnp.sin(x), jnp.cos(y), jnp.sin(y)
    return (cx * cy - sx * sy).reshape(L, L), (sx * cy + cx * sy).reshape(L, L)


def _hyena_tables(L):
    bf = jnp.bfloat16
    c, s = _dft_cos_sin(L, 0)
    fwd = jnp.concatenate([c, -s], axis=0).astype(bf)
    inv = (jnp.concatenate([c.T, -s.T], axis=1) * (1.0 / L)).astype(bf)
    cf, sf = _dft_cos_sin(L, -(L // 2))
    filt_fwd = jnp.concatenate([cf, -sf], axis=0).astype(bf)
    return fwd, inv, filt_fwd


def _hyena_filter_kernel(band_ref, w1t_ref, w1c_ref, w1s_ref, b1_ref, w2_ref, b2_ref, w3_ref,
                         b3_ref, fr_ref, w4_ref, dl_ref, o_ref, h_scr, *, L):
    t = lax.broadcasted_iota(jnp.int32, (L, 1), 0).astype(jnp.float32)

    @pl.when(pl.program_id(0) == 0)
    def _():
        ang = ((2.0 * math.pi / L) * t) * band_ref[...]
        fr = fr_ref[...]
        z = ((t / L) * w1t_ref[...] + _dot3(jnp.cos(ang), w1c_ref[...])
             + _dot3(jnp.sin(ang), w1s_ref[...]))
        h = jnp.sin(fr * (z + b1_ref[...]))
        h = jnp.sin(fr * (_dot3(h, w2_ref[...]) + b2_ref[...]))
        h_scr[...] = jnp.sin(fr * (_dot3(h, w3_ref[...]) + b3_ref[...]))

    dist = jnp.abs(t - (L // 2)) / L
    o_ref[...] = _dot3(h_scr[...], w4_ref[...]) * jnp.exp(-dist * dl_ref[...])


def _hyena_filter(L, w1, b1, w2, b2, w3, b3, w4, freq):
    hid = w2.shape[0]
    tn = 512
    bands = jnp.linspace(1e-4, HY_BANDS - 1, HY_BANDS, dtype=jnp.float32)
    lane_pad = V7X_LANES - HY_BANDS
    band_row = jnp.pad(bands, (0, lane_pad)).reshape(1, V7X_LANES)
    pad_rows = lambda w: jnp.pad(w, ((0, lane_pad), (0, 0)))
    deltas = jnp.abs(jnp.linspace(math.log(HY_TARGET) / HY_SLOW_DECAY,
                                  math.log(HY_TARGET) / HY_FAST_DECAY, HY_WIDTH,
                                  dtype=jnp.float32)).reshape(1, HY_WIDTH)
    full = lambda shape: pl.BlockSpec(shape, lambda j: (0,) * len(shape))
    return pl.pallas_call(
        functools.partial(_hyena_filter_kernel, L=L),
        grid=(HY_WIDTH // tn,),
        in_specs=[full((1, V7X_LANES)), full((1, hid)), full((V7X_LANES, hid)),
                  full((V7X_LANES, hid)), full((1, hid)), full((hid, hid)), full((1, hid)),
                  full((hid, hid)), full((1, hid)), full((1, hid)),
                  pl.BlockSpec((hid, tn), lambda j: (0, j)),
                  pl.BlockSpec((1, tn), lambda j: (0, j))],
        out_specs=pl.BlockSpec((L, tn), lambda j: (0, j)),
        out_shape=jax.ShapeDtypeStruct((L, HY_WIDTH), jnp.float32),
        scratch_shapes=[pltpu.VMEM((L, hid), jnp.float32)],
        compiler_params=_cparams(1), name=f"hyena_filter_{L}",
    )(band_row, w1[0:1], pad_rows(w1[1:1 + HY_BANDS]), pad_rows(w1[1 + HY_BANDS:]),
      b1.reshape(1, hid), w2, b2.reshape(1, hid), w3, b3.reshape(1, hid), freq.reshape(1, hid),
      w4, deltas)


def _hyena_conv_kernel(*refs):
    (x0_ref, x1_ref, v_ref, fr_ref, fi_ref, hr_ref, hi_ref, ir_ref, ii_ref, fb_ref) = refs[:10]
    o_ref, vbf, acc = refs[-3:]
    f32, bf = jnp.float32, jnp.bfloat16
    kb = pl.program_id(2)

    @pl.when(kb == 0)
    def _():
        vbf[...] = (v_ref[...] * x1_ref[...]).astype(bf)
        acc[...] = jnp.zeros_like(acc)

    vr = jnp.dot(fr_ref[...], vbf[...], preferred_element_type=f32)
    vi = jnp.dot(fi_ref[...], vbf[...], preferred_element_type=f32)
    hr, hi = hr_ref[...], hi_ref[...]
    yr = (vr * hr - vi * hi).astype(bf)
    yi = (vr * hi + vi * hr).astype(bf)
    acc[...] += (jnp.dot(ir_ref[...], yr, preferred_element_type=f32)
                 + jnp.dot(ii_ref[...], yi, preferred_element_type=f32))

    @pl.when(kb == pl.num_programs(2) - 1)
    def _():
        vv = v_ref[...] * x1_ref[...]
        o_ref[...] = ((acc[...] + vv * fb_ref[...]) * x0_ref[...]).astype(o_ref.dtype)


def _hyena_conv(u, filt_spec, tables, f_bias, *, L, row_blk0, n_seq, tn, kblk):
    fwd, inv, _ = tables
    nj, nk = HY_WIDTH // tn, L // kblk
    u_spec = lambda part: pl.BlockSpec((L, tn), lambda b, j, k: (row_blk0 + b, part * nj + j))
    in_specs = [u_spec(0), u_spec(1), u_spec(2),
                pl.BlockSpec((kblk, L), lambda b, j, k: (k, 0)),
                pl.BlockSpec((kblk, L), lambda b, j, k: (nk + k, 0)),
                pl.BlockSpec((kblk, tn), lambda b, j, k: (k, j)),
                pl.BlockSpec((kblk, tn), lambda b, j, k: (nk + k, j)),
                pl.BlockSpec((L, kblk), lambda b, j, k: (0, k)),
                pl.BlockSpec((L, kblk), lambda b, j, k: (0, nk + k)),
                pl.BlockSpec((1, tn), lambda b, j, k: (0, j))]
    args = [u, u, u, fwd, fwd, filt_spec, filt_spec, inv, inv, f_bias.reshape(1, HY_WIDTH)]
    return pl.pallas_call(
        _hyena_conv_kernel,
        grid=(n_seq, nj, nk),
        in_specs=in_specs,
        out_specs=pl.BlockSpec((L, tn), lambda b, j, k: (b, j)),
        out_shape=jax.ShapeDtypeStruct((n_seq * L, HY_WIDTH), jnp.bfloat16),
        scratch_shapes=[pltpu.VMEM((L, tn), jnp.bfloat16), pltpu.VMEM((L, tn), jnp.float32)],
        compiler_params=_cparams(3), name=f"hyena_conv_{L}",
    )(*args)


def _hyena_mix(u, filt_p, f_bias):
    outs = []
    for L, row_blk0, n_seq, tn, kblk in ((SEQ, 0, BATCH, HY_WIDTH, SEQ),
                                        (DEC_SEQ, T_CTX // DEC_SEQ, DEC_BATCH, 256, 512)):
        tables = _hyena_tables(L)
        filt = _hyena_filter(L, *filt_p)
        spec = _matmul(tables[2], filt, HY_WIDTH, tm=min(1024, 2 * L), name=f"hyena_spec_{L}")
        outs.append(_hyena_conv(u, spec, tables, f_bias, L=L, row_blk0=row_blk0, n_seq=n_seq,
                                tn=tn, kblk=kblk))
    return jnp.concatenate(outs, axis=0)


def kernel(x_prompt, x_sample, c, state_l0_ssm_fwd, state_l0_ssm_bwd, cache_l0_k, cache_l0_v, c_ctx, l0_norm_mix, l0_norm_ffn, l0_ada_w, l0_ada_b, l0_w_in, l0_conv_w, l0_conv_b, l0_a_log_fwd, l0_a_log_bwd, l0_dt_bias_fwd, l0_dt_bias_bwd, l0_d_skip, l0_gnorm_w, l0_lambda_q1, l0_lambda_k1, l0_lambda_q2, l0_lambda_k2, l0_subln_w, l0_w_out, l0_router_w, l0_router_b, l0_w_gate_up, l0_b_gate_up, l0_w_down, l0_b_down, l1_norm_mix, l1_norm_ffn, l1_ada_w, l1_ada_b, l1_w_in, l1_b_in, l1_short_w, l1_short_b, l1_filt_w1, l1_filt_b1, l1_filt_w2, l1_filt_b2, l1_filt_w3, l1_filt_b3, l1_filt_w4, l1_filt_freq, l1_filt_bias, l1_w_out, l1_b_out, l1_router_w, l1_router_b, l1_w_gate_up, l1_b_gate_up, l1_w_down, l1_b_down, final_norm):
    h = jnp.concatenate([x_prompt.reshape(T_CTX, D_MODEL), x_sample.reshape(T_DEC, D_MODEL)], axis=0)
    cond8 = jnp.concatenate([c_ctx[None, :], c,
                             jnp.zeros((MOD_ROWS - N_COND, D_MODEL), jnp.float32)], axis=0)

    def router_args(rw, rb):
        pad = V7X_LANES - N_EXPERTS
        return (jnp.pad(rw, ((0, 0), (0, pad))),
                jnp.pad(rb, (0, pad), constant_values=-1e30).reshape(1, V7X_LANES))

    def ffn(h, mod, norm_g, rw, rb, w_gu, b_gu, w_dn, b_dn):
        hn, logits = _normmod(h, norm_g, mod, 3, router=router_args(rw, rb))
        return _moe(h, hn, logits, w_gu, b_gu, w_dn, b_dn, mod)

    mod0 = _adaln(cond8, l0_ada_w, l0_ada_b)
    hn = _normmod(h, l0_norm_mix, mod0, 0)
    c_dt = sum(L0_SPLITS[:4])
    c_q = c_dt + 2 * SSD_HEADS
    z_ssd = _matmul(hn, l0_w_in, c_dt, tn=512, name="l0_in_ssd")
    lane_pad = jnp.zeros((D_MODEL, V7X_LANES - SSD_HEADS), jnp.float32)
    w_dt = jnp.concatenate([l0_w_in[:, c_dt:c_dt + SSD_HEADS], lane_pad,
                            l0_w_in[:, c_dt + SSD_HEADS:c_q], lane_pad], axis=1)
    z_dt = _matmul(hn, w_dt, 2 * V7X_LANES, tn=2 * V7X_LANES, name="l0_in_dt")
    z_qkv = _matmul(hn, l0_w_in[:, c_q:], 3 * DA_WIDTH, name="l0_in_qkv")
    xbc = _dwconv(z_ssd, l0_conv_w, l0_conv_b, in_col_blocks=(0, 1, 4), tc=512, silu=True,
                  name="ssd_conv")
    h0_shape = (DEC_BATCH, SSD_WIDTH, SSD_STATE)
    y_fwd, st_fwd = _ssd(xbc, z_dt, l0_dt_bias_fwd, l0_a_log_fwd,
                         state_l0_ssm_fwd.reshape(h0_shape), reverse=False)
    y_ssd, st_bwd = _ssd(xbc, z_dt, l0_dt_bias_bwd, l0_a_log_bwd,
                         state_l0_ssm_bwd.reshape(h0_shape), reverse=True,
                         finish=(y_fwd, z_ssd, jnp.repeat(l0_d_skip, SSD_HEAD_DIM).reshape(1, -1),
                                 l0_gnorm_w.reshape(1, -1)))
    lam = (jnp.exp(jnp.sum(l0_lambda_q1 * l0_lambda_k1))
           - jnp.exp(jnp.sum(l0_lambda_q2 * l0_lambda_k2)) + DA_LAMBDA_INIT).reshape(1)
    o_att = _diff_attention(z_qkv, cache_l0_k, cache_l0_v, lam, l0_subln_w)
    h = _matmul(y_ssd, l0_w_out, D_MODEL, a2=o_att, gated=(h, mod0, 2), name="l0_out")
    h = ffn(h, mod0, l0_norm_ffn, l0_router_w, l0_router_b, l0_w_gate_up, l0_b_gate_up,
            l0_w_down, l0_b_down)
    st_shape = (BATCH, SSD_HEADS, SSD_HEAD_DIM, SSD_STATE)
    kv_shape = (BATCH, SEQ, DA_HEADS, HEAD_W)
    produced = (st_fwd[:BATCH].reshape(st_shape), st_bwd[:BATCH].reshape(st_shape),
                z_qkv[:T_CTX, DA_WIDTH:2 * DA_WIDTH].reshape(kv_shape),
                z_qkv[:T_CTX, 2 * DA_WIDTH:].reshape(kv_shape))

    mod1 = _adaln(cond8, l1_ada_w, l1_ada_b)
    hn = _normmod(h, l1_norm_mix, mod1, 0)
    u = _matmul(hn, l1_w_in, 3 * HY_WIDTH, bias=l1_b_in, name="l1_in")
    u = _dwconv(u, l1_short_w, l1_short_b, in_col_blocks=(0, 1, 2), tc=HY_WIDTH, silu=False,
                name="hyena_conv")
    filt_p = (l1_filt_w1, l1_filt_b1, l1_filt_w2, l1_filt_b2, l1_filt_w3, l1_filt_b3, l1_filt_w4,
              l1_filt_freq)
    y = _hyena_mix(u, filt_p, l1_filt_bias)
    h = _matmul(y, l1_w_out, D_MODEL, bias=l1_b_out, gated=(h, mod1, 2), name="l1_out")
    h = ffn(h, mod1, l1_norm_ffn, l1_router_w, l1_router_b, l1_w_gate_up, l1_b_gate_up,
            l1_w_down, l1_b_down)

    y_prompt = _final_norm(h, final_norm, 0, T_CTX).reshape(BATCH, SEQ, D_MODEL)
    y_sample = _final_norm(h, final_norm, T_CTX, T_DEC).reshape(DEC_BATCH, DEC_SEQ, D_MODEL)
    return (y_prompt, y_sample) + produced
```
